```python
import math
import jax, jax.numpy as jnp
from jax import lax
import numpy as np

D_MODEL = 2048
BATCH = 4
SEQ = 2048
DEPTH = 2
DEC_BATCH = 128
DEC_SEQ = 1
PAST_LEN = 16384
PAGE_SIZE = 128

N_GROUPS = 4
W_GROUP = D_MODEL // N_GROUPS
N_PROJ = 11
CONV_A_WIDTH = 31
CONV_B_WIDTH = 3
CONV_C_WIDTH = 4
CONV_FFN_WIDTH = 3
LRU_HEADS = 4
LRU_HEAD_DIM = W_GROUP // LRU_HEADS
LRU_C = 8.0
RET_HEADS = 4
RET_HEAD_DIM = W_GROUP // RET_HEADS
RET_CHUNK = 128
ROPE_BASE = 10000.0
D_FF = 5632
EPS = 1e-6

kernel_name = "hymba_style_conformer_shortconv_rglru_retention_decoder_step"


def rmsnorm(x, g):
    xf = x.astype(jnp.float32)
    y = xf * lax.rsqrt(jnp.mean(xf * xf, axis=-1, keepdims=True) + EPS)
    return (y * g.astype(jnp.float32)).astype(x.dtype)


def layernorm(x, g, b):
    xf = x.astype(jnp.float32)
    mu = jnp.mean(xf, axis=-1, keepdims=True)
    xc = xf - mu
    y = xc * lax.rsqrt(jnp.mean(xc * xc, axis=-1, keepdims=True) + EPS)
    return (y * g.astype(jnp.float32) + b.astype(jnp.float32)).astype(x.dtype)


def causal_dwconv(x, buf, w):
    width, ch = w.shape
    xp = jnp.concatenate([buf.astype(x.dtype), x], axis=1)
    y = lax.conv_general_dilated(xp, w[:, None, :].astype(x.dtype), window_strides=(1,), padding='VALID',
                                 dimension_numbers=('NWC', 'WIO', 'NWC'), feature_group_count=ch)
    new_buf = xp[:, xp.shape[1] - (width - 1):]
    return y, new_buf


def rope(t, pos):
    half = t.shape[-1] // 2
    inv = ROPE_BASE ** (-jnp.arange(half, dtype=jnp.float32) / half)
    ang = pos.astype(jnp.float32)[:, None] * inv[None, :]
    cos = jnp.cos(ang)[None, :, None, :]
    sin = jnp.sin(ang)[None, :, None, :]
    t1, t2 = t[..., :half], t[..., half:]
    return jnp.concatenate([t1 * cos - t2 * sin, t1 * sin + t2 * cos], axis=-1)


def rglru(xr, r, i, lam, h0):
    log_a = -LRU_C * r * jax.nn.softplus(-lam)
    a = jnp.exp(log_a)
    b = jnp.sqrt(jnp.maximum(-jnp.expm1(2.0 * log_a), 0.0)) * (i * xr)

    def combine(c1, c2):
        a1, b1 = c1
        a2, b2 = c2
        return a1 * a2, a2 * b1 + b2

    a_cum, b_cum = lax.associative_scan(combine, (a, b), axis=1)
    h = a_cum * h0[:, None, :] + b_cum
    return h, h[:, -1]


def retention(q, k, v, s0):
    bsz, seqlen, nh, dh = q.shape
    chunk = RET_CHUNK if seqlen % RET_CHUNK == 0 else seqlen
    n = seqlen // chunk
    log_g = jnp.log(1.0 - 2.0 ** (-5.0 - jnp.arange(nh, dtype=jnp.float32)))
    idx = jnp.arange(chunk, dtype=jnp.float32)
    rel = idx[:, None] - idx[None, :]
    decay_mask = jnp.where(rel >= 0, jnp.exp(log_g[:, None, None] * jnp.maximum(rel, 0.0)), 0.0)
    inner_decay = jnp.exp(log_g[:, None] * (idx + 1.0))[None, :, :, None]
    state_decay = jnp.exp(log_g[:, None] * (chunk - 1.0 - idx))[None, :, :, None]
    chunk_decay = jnp.exp(log_g * chunk)[None, :, None, None]

    def to_chunks(t):
        return t.reshape(bsz, n, chunk, nh, dh).transpose(1, 0, 3, 2, 4)

    def step(s, qkv):
        qc, kc, vc = qkv
        scores = jnp.einsum('bhid,bhjd->bhij', qc, kc) * decay_mask
        o = jnp.einsum('bhij,bhjv->bhiv', scores, vc) + jnp.einsum('bhid,bhdv->bhiv', qc, s) * inner_decay
        s = s * chunk_decay + jnp.einsum('bhjd,bhjv->bhdv', kc * state_decay, vc)
        return s, o

    s_fin, o = lax.scan(step, s0, (to_chunks(q), to_chunks(k), to_chunks(v)))
    o = o.transpose(1, 0, 3, 2, 4).reshape(bsz, seqlen, nh, dh)
    return o, s_fin


def hybrid_layer(x, pos, st, wl):
    conv_a, conv_b, conv_c, lru_h, ret_s, conv_ffn = st
    (g_mix, w_in, w_conv_a, ln_a_g, ln_a_b, w_conv_b, w_conv_c, b_conv_c, w_gate_a, b_gate_a,
     w_gate_x, b_gate_x, lru_lambda, w_out, g_ffn, w_ffn_gate, w_ffn_up, w_conv_ffn, w_ffn_down) = wl
    bsz, seqlen, _ = x.shape
    f32 = jnp.float32
    h = rmsnorm(x, g_mix)
    z = h @ w_in
    a_v, a_g, b_b, b_c, b_x, c_g, c_x, d_q, d_k, d_v, d_g = jnp.split(z, N_PROJ, axis=-1)

    y_a, new_conv_a = causal_dwconv(a_v * jax.nn.sigmoid(a_g), conv_a, w_conv_a)
    y_a = jax.nn.silu(layernorm(y_a, ln_a_g, ln_a_b))

    y_b, new_conv_b = causal_dwconv(b_c * b_x, conv_b, w_conv_b)
    y_b = b_b * y_b

    xr, new_conv_c = causal_dwconv(c_x, conv_c, w_conv_c)
    xr = (xr + b_conv_c).astype(f32)
    xh = xr.reshape(bsz, seqlen, LRU_HEADS, LRU_HEAD_DIM)
    r = jax.nn.sigmoid(jnp.einsum('blhi,hij->blhj', xh, w_gate_a.astype(f32)).reshape(bsz, seqlen, W_GROUP) + b_gate_a.astype(f32))
    ig = jax.nn.sigmoid(jnp.einsum('blhi,hij->blhj', xh, w_gate_x.astype(f32)).reshape(bsz, seqlen, W_GROUP) + b_gate_x.astype(f32))
    h_seq, new_h = rglru(xr, r, ig, lru_lambda.astype(f32), lru_h.astype(f32))
    y_c = h_seq.astype(x.dtype) * jax.nn.gelu(c_g)

    q = rope(d_q.astype(f32).reshape(bsz, seqlen, RET_HEADS, RET_HEAD_DIM), pos)
    k = rope(d_k.astype(f32).reshape(bsz, seqlen, RET_HEADS, RET_HEAD_DIM), pos) * (RET_HEAD_DIM ** -0.5)
    v = d_v.astype(f32).reshape(bsz, seqlen, RET_HEADS, RET_HEAD_DIM)
    o, new_s = retention(q, k, v, ret_s.astype(f32))
    o = o * lax.rsqrt(jnp.mean(o * o, axis=-1, keepdims=True) + EPS)
    y_d = jax.nn.silu(d_g) * o.reshape(bsz, seqlen, W_GROUP).astype(x.dtype)

    mix = jnp.concatenate([y_a, y_b, y_c, y_d], axis=-1)
    x = x + mix @ w_out

    h2 = rmsnorm(x, g_ffn)
    gpre, new_conv_ffn = causal_dwconv(h2 @ w_ffn_gate, conv_ffn, w_conv_ffn)
    x = x + (jax.nn.silu(gpre) * (h2 @ w_ffn_up)) @ w_ffn_down
    dt = x.dtype
    return x, (new_conv_a.astype(dt), new_conv_b.astype(dt), new_conv_c.astype(dt), new_h.astype(dt),
               new_s.astype(dt), new_conv_ffn.astype(dt))


def run_trunk(x, pos, states, weights, g_final):
    new = [[] for _ in states]
    for l in range(DEPTH):
        x, ns = hybrid_layer(x, pos, tuple(s[l] for s in states), tuple(w[l] for w in weights))
        for lst, s in zip(new, ns):
            lst.append(s)
    return rmsnorm(x, g_final), tuple(jnp.stack(lst) for lst in new)


def setup_inputs(seed: int = 0) -> dict:
    key = jax.random.key(seed)
    ks = jax.random.split(key, 32)
    f32 = jnp.float32
    nrm = lambda k, shape, s: jax.random.normal(k, shape, f32) * s
    u = jax.random.uniform(ks[20], (DEPTH, W_GROUP), f32, 0.9, 0.999) ** (1.0 / LRU_C)
    return {
        'x_prompt': nrm(ks[0], (BATCH, SEQ, D_MODEL), 1.0),
        'x_sample': nrm(ks[1], (DEC_BATCH, DEC_SEQ, D_MODEL), 1.0),
        'state_conv_a': nrm(ks[2], (DEPTH, DEC_BATCH, CONV_A_WIDTH - 1, W_GROUP), 0.5),
        'state_conv_b': nrm(ks[3], (DEPTH, DEC_BATCH, CONV_B_WIDTH - 1, W_GROUP), 0.5),
        'state_conv_c': nrm(ks[4], (DEPTH, DEC_BATCH, CONV_C_WIDTH - 1, W_GROUP), 0.5),
        'state_lru_c': nrm(ks[5], (DEPTH, DEC_BATCH, W_GROUP), 0.5),
        'state_ret_d': nrm(ks[6], (DEPTH, DEC_BATCH, RET_HEADS, RET_HEAD_DIM, RET_HEAD_DIM), 0.5),
        'state_conv_ffn': nrm(ks[7], (DEPTH, DEC_BATCH, CONV_FFN_WIDTH - 1, D_FF), 0.5),
        'g_mix': 1.0 + nrm(ks[8], (DEPTH, D_MODEL), 0.02),
        'w_in': nrm(ks[9], (DEPTH, D_MODEL, N_PROJ * W_GROUP), D_MODEL ** -0.5),
        'w_conv_a': nrm(ks[10], (DEPTH, CONV_A_WIDTH, W_GROUP), CONV_A_WIDTH ** -0.5),
        'ln_a_g': 1.0 + nrm(ks[11], (DEPTH, W_GROUP), 0.02),
        'ln_a_b': nrm(ks[12], (DEPTH, W_GROUP), 0.02),
        'w_conv_b': nrm(ks[13], (DEPTH, CONV_B_WIDTH, W_GROUP), CONV_B_WIDTH ** -0.5),
        'w_conv_c': nrm(ks[14], (DEPTH, CONV_C_WIDTH, W_GROUP), CONV_C_WIDTH ** -0.5),
        'b_conv_c': nrm(ks[15], (DEPTH, W_GROUP), 0.02),
        'w_gate_a': nrm(ks[16], (DEPTH, LRU_HEADS, LRU_HEAD_DIM, LRU_HEAD_DIM), LRU_HEAD_DIM ** -0.5),
        'b_gate_a': nrm(ks[17], (DEPTH, W_GROUP), 0.1),
        'w_gate_x': nrm(ks[18], (DEPTH, LRU_HEADS, LRU_HEAD_DIM, LRU_HEAD_DIM), LRU_HEAD_DIM ** -0.5),
        'b_gate_x': nrm(ks[19], (DEPTH, W_GROUP), 0.1),
        'lru_lambda': jnp.log(u / (1.0 - u)),
        'w_out': nrm(ks[21], (DEPTH, D_MODEL, D_MODEL), D_MODEL ** -0.5),
        'g_ffn': 1.0 + nrm(ks[22], (DEPTH, D_MODEL), 0.02),
        'w_ffn_gate': nrm(ks[23], (DEPTH, D_MODEL, D_FF), D_MODEL ** -0.5),
        'w_ffn_up': nrm(ks[24], (DEPTH, D_MODEL, D_FF), D_MODEL ** -0.5),
        'w_conv_ffn': nrm(ks[25], (DEPTH, CONV_FFN_WIDTH, D_FF), CONV_FFN_WIDTH ** -0.5),
        'w_ffn_down': nrm(ks[26], (DEPTH, D_FF, D_MODEL), D_FF ** -0.5),
        'g_final': 1.0 + nrm(ks[27], (D_MODEL,), 0.02),
    }


def reference(x_prompt, x_sample, state_conv_a, state_conv_b, state_conv_c, state_lru_c, state_ret_d, state_conv_ffn,
              g_mix, w_in, w_conv_a, ln_a_g, ln_a_b, w_conv_b, w_conv_c, b_conv_c, w_gate_a, b_gate_a, w_gate_x, b_gate_x,
              lru_lambda, w_out, g_ffn, w_ffn_gate, w_ffn_up, w_conv_ffn, w_ffn_down, g_final):
    weights = (g_mix, w_in, w_conv_a, ln_a_g, ln_a_b, w_conv_b, w_conv_c, b_conv_c, w_gate_a, b_gate_a,
               w_gate_x, b_gate_x, lru_lambda, w_out, g_ffn, w_ffn_gate, w_ffn_up, w_conv_ffn, w_ffn_down)
    bp = x_prompt.shape[0]
    dt = x_prompt.dtype
    zero_states = (
        jnp.zeros((DEPTH, bp, CONV_A_WIDTH - 1, W_GROUP), dt),
        jnp.zeros((DEPTH, bp, CONV_B_WIDTH - 1, W_GROUP), dt),
        jnp.zeros((DEPTH, bp, CONV_C_WIDTH - 1, W_GROUP), dt),
        jnp.zeros((DEPTH, bp, W_GROUP), dt),
        jnp.zeros((DEPTH, bp, RET_HEADS, RET_HEAD_DIM, RET_HEAD_DIM), dt),
        jnp.zeros((DEPTH, bp, CONV_FFN_WIDTH - 1, D_FF), dt),
    )
    pos_p = jnp.arange(x_prompt.shape[1], dtype=jnp.int32)
    pos_s = PAST_LEN + jnp.arange(x_sample.shape[1], dtype=jnp.int32)
    y_prompt, (p_conv_a, p_conv_b, p_conv_c, p_lru_c, p_ret_d, p_conv_ffn) = run_trunk(
        x_prompt, pos_p, zero_states, weights, g_final)
    y_sample, (s_conv_a, s_conv_b, s_conv_c, s_lru_c, s_ret_d, s_conv_ffn) = run_trunk(
        x_sample, pos_s, (state_conv_a, state_conv_b, state_conv_c, state_lru_c, state_ret_d, state_conv_ffn),
        weights, g_final)
    return (y_prompt, y_sample, p_conv_a, p_conv_b, p_conv_c, p_lru_c, p_ret_d, p_conv_ffn,
            s_conv_a, s_conv_b, s_conv_c, s_lru_c, s_ret_d, s_conv_ffn)
```

```python
import functools

import jax
import jax.numpy as jnp
from jax import lax
from jax.experimental import pallas as pl
from jax.experimental.pallas import tpu as pltpu

F32 = jnp.float32
BF16 = jnp.bfloat16

D_MODEL = 2048
N_GROUPS = 4
W_GROUP = D_MODEL // N_GROUPS
N_PROJ = 11
CONV_A_WIDTH = 31
CONV_B_WIDTH = 3
CONV_C_WIDTH = 4
CONV_FFN_WIDTH = 3
LRU_HEADS = 4
LRU_C = 8.0
RET_HEADS = 4
HEAD_DIM = W_GROUP // RET_HEADS
RET_CHUNK = 128
ROPE_BASE = 10000.0
D_FF = 5632
EPS = 1e-6
PAST_LEN = 16384

SUBLANES = 8
VMEM_LIMIT_BYTES = 56 * 1024 * 1024

TM = 1024
TN = 512
TM_DOWN = 512
TL = 256
CONV_A_ROWS = 32
CONV_A_HALO = 32
BT = 16


def _cparams(*semantics):
    return pltpu.CompilerParams(dimension_semantics=semantics, vmem_limit_bytes=VMEM_LIMIT_BYTES)


def _rms_scale(x):
    return x * lax.rsqrt(jnp.mean(x * x, axis=-1, keepdims=True) + EPS)


def _dot(a, b):
    return jnp.dot(a, b, preferred_element_type=F32)


def _dot_nt(a, b):
    return lax.dot_general(a, b, (((1,), (1,)), ((), ())), preferred_element_type=F32)


def _dot_tn(a, b):
    return lax.dot_general(a, b, (((0,), (0,)), ((), ())), preferred_element_type=F32)


def _rmsnorm_kernel(x_ref, g_ref, o_ref):
    o_ref[...] = (_rms_scale(x_ref[...]) * g_ref[...]).astype(o_ref.dtype)


def _rmsnorm(x, g_all, layer, out_dtype, tm):
    rows = x.shape[0]
    return pl.pallas_call(
        _rmsnorm_kernel,
        grid=(rows // tm,),
        in_specs=[pl.BlockSpec((tm, D_MODEL), lambda i: (i, 0)),
                  pl.BlockSpec((None, 1, D_MODEL), lambda i: (layer, 0, 0))],
        out_specs=pl.BlockSpec((tm, D_MODEL), lambda i: (i, 0)),
        out_shape=jax.ShapeDtypeStruct((rows, D_MODEL), out_dtype),
        compiler_params=_cparams("arbitrary"),
        name="rmsnorm",
    )(x, g_all)


def _inproj_kernel(hp_ref, xs_ref, g_ref, w_ref, zp_ref, zs_ref, wb_ref):
    @pl.when(pl.program_id(1) == 0)
    def _():
        wb_ref[...] = w_ref[...].astype(BF16)
        hs = (_rms_scale(xs_ref[...]) * g_ref[...]).astype(BF16)
        zs_ref[...] = _dot(hs, wb_ref[...])

    zp_ref[...] = _dot(hp_ref[...], wb_ref[...])


def _inproj(hp, xs, g_all, w_all, layer):
    rows, n_s = hp.shape[0], xs.shape[0]
    n_out = w_all.shape[2]
    return pl.pallas_call(
        _inproj_kernel,
        grid=(n_out // TN, rows // TM),
        in_specs=[pl.BlockSpec((TM, D_MODEL), lambda j, i: (i, 0)),
                  pl.BlockSpec((n_s, D_MODEL), lambda j, i: (0, 0)),
                  pl.BlockSpec((None, 1, D_MODEL), lambda j, i: (layer, 0, 0)),
                  pl.BlockSpec((None, D_MODEL, TN), lambda j, i: (layer, 0, j))],
        out_specs=[pl.BlockSpec((TM, TN), lambda j, i: (i, j)),
                   pl.BlockSpec((n_s, TN), lambda j, i: (0, j))],
        out_shape=[jax.ShapeDtypeStruct((rows, n_out), F32),
                   jax.ShapeDtypeStruct((n_s, n_out), F32)],
        scratch_shapes=[pltpu.VMEM((D_MODEL, TN), BF16)],
        compiler_params=_cparams("arbitrary", "arbitrary"),
        name="inproj",
    )(hp, xs, g_all, w_all)


def _resproj_kernel(ap_ref, as_ref, w_ref, xp_ref, xs_ref, op_ref, os_ref, wb_ref):
    @pl.when(pl.program_id(1) == 0)
    def _():
        wb_ref[...] = w_ref[...].astype(BF16)
        os_ref[...] = xs_ref[...] + _dot(as_ref[...], wb_ref[...])

    op_ref[...] = xp_ref[...] + _dot(ap_ref[...], wb_ref[...])


def _resproj(ap, a_s, w_all, layer, xp, xs, tm, name):
    rows, n_s = ap.shape[0], a_s.shape[0]
    k_dim = ap.shape[1]
    return pl.pallas_call(
        _resproj_kernel,
        grid=(D_MODEL // TN, rows // tm),
        in_specs=[pl.BlockSpec((tm, k_dim), lambda j, i: (i, 0)),
                  pl.BlockSpec((n_s, k_dim), lambda j, i: (0, 0)),
                  pl.BlockSpec((None, k_dim, TN), lambda j, i: (layer, 0, j)),
                  pl.BlockSpec((tm, TN), lambda j, i: (i, j)),
                  pl.BlockSpec((n_s, TN), lambda j, i: (0, j))],
        out_specs=[pl.BlockSpec((tm, TN), lambda j, i: (i, j)),
                   pl.BlockSpec((n_s, TN), lambda j, i: (0, j))],
        out_shape=[jax.ShapeDtypeStruct((rows, D_MODEL), F32),
                   jax.ShapeDtypeStruct((n_s, D_MODEL), F32)],
        scratch_shapes=[pltpu.VMEM((k_dim, TN), BF16)],
        compiler_params=_cparams("arbitrary", "arbitrary"),
        name=name,
    )(ap, a_s, w_all, xp, xs)


def _ffn_up_kernel(tiles_per_seq, hp_ref, xs_ref, g_ref, wg_ref, wu_ref, wc_ref, st_ref,
                   hidp_ref, hids_ref, pst_ref, sst_ref, wgb_ref, wub_ref, carry_ref, gbuf_ref):
    i = pl.program_id(1)
    w0 = wc_ref[0:1, :]
    w1 = wc_ref[1:2, :]
    w2 = wc_ref[2:3, :]

    @pl.when(i == 0)
    def _():
        wgb_ref[...] = wg_ref[...].astype(BF16)
        wub_ref[...] = wu_ref[...].astype(BF16)
        hs = (_rms_scale(xs_ref[...]) * g_ref[...]).astype(BF16)
        gate = _dot(hs, wgb_ref[...])
        up = _dot(hs, wub_ref[...])
        s0 = st_ref[:, 0, :]
        s1 = st_ref[:, 1, :]
        gpre = w0 * s0 + w1 * s1 + w2 * gate
        hids_ref[...] = (jax.nn.silu(gpre) * up).astype(BF16)
        sst_ref[:, 0, :] = s1
        sst_ref[:, 1, :] = gate

    tm = hp_ref.shape[0]
    hp = hp_ref[...]
    gate = _dot(hp, wgb_ref[...])
    up = _dot(hp, wub_ref[...])
    first = (i % tiles_per_seq) == 0

    @pl.when(first)
    def _():
        gbuf_ref[0:SUBLANES, :] = jnp.zeros((SUBLANES, gate.shape[1]), F32)

    @pl.when(jnp.logical_not(first))
    def _():
        gbuf_ref[0:SUBLANES, :] = carry_ref[...]

    gbuf_ref[SUBLANES:SUBLANES + tm, :] = gate
    carry_ref[...] = gate[tm - SUBLANES:tm, :]
    g1 = gbuf_ref[pl.ds(SUBLANES - 1, tm), :]
    g2 = gbuf_ref[pl.ds(SUBLANES - 2, tm), :]
    gpre = w0 * g2 + w1 * g1 + w2 * gate
    hidp_ref[...] = (jax.nn.silu(gpre) * up).astype(BF16)
    pst_ref[0] = gbuf_ref[pl.ds(SUBLANES + tm - 2, 2), :]


def _ffn_up(hp, xs, g_all, wg_all, wu_all, wc_all, st_all, layer, n_seq):
    rows, n_s = hp.shape[0], xs.shape[0]
    tiles_per_seq = rows // n_seq // TM
    w_spec = pl.BlockSpec((None, D_MODEL, TN), lambda j, i: (layer, 0, j))
    return pl.pallas_call(
        functools.partial(_ffn_up_kernel, tiles_per_seq),
        grid=(D_FF // TN, rows // TM),
        in_specs=[pl.BlockSpec((TM, D_MODEL), lambda j, i: (i, 0)),
                  pl.BlockSpec((n_s, D_MODEL), lambda j, i: (0, 0)),
                  pl.BlockSpec((None, 1, D_MODEL), lambda j, i: (layer, 0, 0)),
                  w_spec, w_spec,
                  pl.BlockSpec((None, CONV_FFN_WIDTH, TN), lambda j, i: (layer, 0, j)),
                  pl.BlockSpec((None, n_s, CONV_FFN_WIDTH - 1, TN), lambda j, i: (layer, 0, 0, j))],
        out_specs=[pl.BlockSpec((TM, TN), lambda j, i: (i, j)),
                   pl.BlockSpec((n_s, TN), lambda j, i: (0, j)),
                   pl.BlockSpec((1, CONV_FFN_WIDTH - 1, TN), lambda j, i: (i // tiles_per_seq, 0, j)),
                   pl.BlockSpec((n_s, CONV_FFN_WIDTH - 1, TN), lambda j, i: (0, 0, j))],
        out_shape=[jax.ShapeDtypeStruct((rows, D_FF), BF16),
                   jax.ShapeDtypeStruct((n_s, D_FF), BF16),
                   jax.ShapeDtypeStruct((n_seq, CONV_FFN_WIDTH - 1, D_FF), F32),
                   jax.ShapeDtypeStruct((n_s, CONV_FFN_WIDTH - 1, D_FF), F32)],
        scratch_shapes=[pltpu.VMEM((D_MODEL, TN), BF16),
                        pltpu.VMEM((D_MODEL, TN), BF16),
                        pltpu.VMEM((SUBLANES, TN), F32),
                        pltpu.VMEM((SUBLANES + TM, TN), F32)],
        compiler_params=_cparams("arbitrary", "arbitrary"),
        name="ffn_up",
    )(hp, xs, g_all, wg_all, wu_all, wc_all, st_all)


def _layernorm_silu(y, g, b):
    mu = jnp.mean(y, axis=-1, keepdims=True)
    yc = y - mu
    yn = yc * lax.rsqrt(jnp.mean(yc * yc, axis=-1, keepdims=True) + EPS)
    return jax.nn.silu(yn * g + b)


def _lru_gates(xr, wga_ref, bga_ref, wgx_ref, bgx_ref, lam_ref):
    ra, rx = [], []
    for h in range(LRU_HEADS):
        xh = xr[:, h * HEAD_DIM:(h + 1) * HEAD_DIM].astype(BF16)
        ra.append(_dot(xh, wga_ref[h].astype(BF16)))
        rx.append(_dot(xh, wgx_ref[h].astype(BF16)))
    r = jax.nn.sigmoid(jnp.concatenate(ra, axis=-1) + bga_ref[...])
    ig = jax.nn.sigmoid(jnp.concatenate(rx, axis=-1) + bgx_ref[...])
    log_a = -LRU_C * r * jax.nn.softplus(-lam_ref[...])
    a = jnp.exp(log_a)
    one_minus_a2 = -jnp.tanh(log_a) * (a * a + 1.0)
    b = jnp.sqrt(jnp.maximum(one_minus_a2, 0.0)) * (ig * xr)
    return a, b


def _rope(t, cc, ss):
    return t * cc + pltpu.roll(t, HEAD_DIM // 2, axis=1) * ss


def _mix_prompt_kernel(av_ref, ag_ref, bb_ref, bc_ref, bx_ref, cg_ref, cx_ref, dq_ref, dk_ref, dv_ref, dg_ref,
                       wa_ref, lng_ref, lnb_ref, wb_ref, wc_ref, bcc_ref,
                       wga_ref, bga_ref, wgx_ref, bgx_ref, lam_ref,
                       cc_ref, ss_ref, dmask_ref, inner_ref, sdec_ref, cdec_ref,
                       mix_ref, pca_ref, pcb_ref, pcc_ref, plru_ref, pret_ref,
                       ub_ref, vb_ref, xb_ref, as_ref, bs_ref, hl_ref, s_ref):
    tl = av_ref.shape[1]
    w = W_GROUP

    @pl.when(pl.program_id(1) == 0)
    def _():
        ub_ref[0, 0:CONV_A_HALO, :] = jnp.zeros((CONV_A_HALO, w), F32)
        vb_ref[0:SUBLANES, :] = jnp.zeros((SUBLANES, w), F32)
        xb_ref[0:SUBLANES, :] = jnp.zeros((SUBLANES, w), F32)
        hl_ref[...] = jnp.zeros(hl_ref.shape, F32)
        s_ref[...] = jnp.zeros(s_ref.shape, F32)

    ub_ref[0, CONV_A_HALO:CONV_A_HALO + tl, :] = av_ref[0] * jax.nn.sigmoid(ag_ref[0])
    n_shift_rows = tl + CONV_A_HALO - SUBLANES
    for s in range(1, SUBLANES):
        ub_ref[s, 0:n_shift_rows, :] = ub_ref[0, s:s + n_shift_rows, :]
    tap0 = CONV_A_HALO - (CONV_A_WIDTH - 1)
    n_sub = CONV_A_ROWS // SUBLANES

    def conv_a_chunk(r, carry):
        base = pl.multiple_of(r * CONV_A_ROWS, CONV_A_ROWS)
        acc = [jnp.zeros((SUBLANES, w), F32) for _ in range(n_sub)]
        for k in range(CONV_A_WIDTH):
            blk, s = divmod(tap0 + k, SUBLANES)
            wk = wa_ref[k]
            for j in range(n_sub):
                start = pl.multiple_of(base + (blk + j) * SUBLANES, SUBLANES)
                acc[j] = acc[j] + wk * ub_ref[s, pl.ds(start, SUBLANES), :]
        y = jnp.concatenate(acc, axis=0)
        ya = _layernorm_silu(y, lng_ref[...], lnb_ref[...])
        mix_ref[0, pl.ds(base, CONV_A_ROWS), 0:w] = ya.astype(BF16)
        return carry

    lax.fori_loop(0, tl // CONV_A_ROWS, conv_a_chunk, 0)
    pca_ref[0] = ub_ref[0, pl.ds(CONV_A_HALO + tl - (CONV_A_WIDTH - 1), CONV_A_WIDTH - 1), :]
    ub_ref[0, 0:CONV_A_HALO, :] = ub_ref[0, tl:tl + CONV_A_HALO, :]

    v = bc_ref[0] * bx_ref[0]
    vb_ref[SUBLANES:SUBLANES + tl, :] = v
    yb = (wb_ref[0:1, :] * vb_ref[pl.ds(SUBLANES - 2, tl), :]
          + wb_ref[1:2, :] * vb_ref[pl.ds(SUBLANES - 1, tl), :]
          + wb_ref[2:3, :] * v)
    mix_ref[0, :, w:2 * w] = (bb_ref[0] * yb).astype(BF16)
    pcb_ref[0] = vb_ref[pl.ds(SUBLANES + tl - 2, 2), :]
    vb_ref[0:SUBLANES, :] = vb_ref[tl:tl + SUBLANES, :]

    cx = cx_ref[0]
    xb_ref[SUBLANES:SUBLANES + tl, :] = cx
    xr = (wc_ref[0:1, :] * xb_ref[pl.ds(SUBLANES - 3, tl), :]
          + wc_ref[1:2, :] * xb_ref[pl.ds(SUBLANES - 2, tl), :]
          + wc_ref[2:3, :] * xb_ref[pl.ds(SUBLANES - 1, tl), :]
          + wc_ref[3:4, :] * cx) + bcc_ref[...]
    pcc_ref[0] = xb_ref[pl.ds(SUBLANES + tl - 3, 3), :]
    xb_ref[0:SUBLANES, :] = xb_ref[tl:tl + SUBLANES, :]
    a, b = _lru_gates(xr, wga_ref, bga_ref, wgx_ref, bgx_ref, lam_ref)
    nblk = tl // SUBLANES
    a3 = a.reshape(nblk, SUBLANES, w)
    b3 = b.reshape(nblk, SUBLANES, w)
    row = lax.broadcasted_iota(jnp.int32, a3.shape, 1)
    for d in (1, 2, 4):
        keep = row >= d
        a_sh = jnp.where(keep, pltpu.roll(a3, d, axis=1), 1.0)
        b_sh = jnp.where(keep, pltpu.roll(b3, d, axis=1), 0.0)
        b3 = a3 * b_sh + b3
        a3 = a3 * a_sh
    as_ref[...] = a3
    bs_ref[...] = b3

    def lru_block(i, h_in):
        h = as_ref[i] * h_in + bs_ref[i]
        bs_ref[i] = h
        return jnp.broadcast_to(h[SUBLANES - 1:SUBLANES, :], (SUBLANES, w))

    h_last = lax.fori_loop(0, nblk, lru_block, hl_ref[...])
    hl_ref[...] = h_last
    plru_ref[0] = h_last[0:1, :]
    h_seq = bs_ref[...].reshape(tl, w)
    mix_ref[0, :, 2 * w:3 * w] = (h_seq * jax.nn.gelu(cg_ref[0])).astype(BF16)

    scale = HEAD_DIM ** -0.5
    for c in range(tl // RET_CHUNK):
        rows = slice(c * RET_CHUNK, (c + 1) * RET_CHUNK)
        cc = cc_ref[rows, :]
        ss = ss_ref[rows, :]
        for h in range(RET_HEADS):
            cols = slice(h * HEAD_DIM, (h + 1) * HEAD_DIM)
            qr = _rope(dq_ref[0, rows, cols], cc, ss)
            kr = _rope(dk_ref[0, rows, cols], cc, ss) * scale
            vv = dv_ref[0, rows, cols].astype(BF16)
            qb = qr.astype(BF16)
            scores = _dot_nt(qb, kr.astype(BF16)) * dmask_ref[h]
            s_old = s_ref[h]
            o = _dot(scores.astype(BF16), vv) + _dot(qb, s_old.astype(BF16)) * inner_ref[h]
            s_ref[h] = s_old * cdec_ref[h] + _dot_tn((kr * sdec_ref[h]).astype(BF16), vv)
            on = o * lax.rsqrt(jnp.mean(o * o, axis=-1, keepdims=True) + EPS)
            mix_ref[0, rows, 3 * w + h * HEAD_DIM:3 * w + (h + 1) * HEAD_DIM] = (
                jax.nn.silu(dg_ref[0, rows, cols]) * on).astype(BF16)
    pret_ref[0] = s_ref[...]


def _retention_tables(chunk):
    log_g = jnp.log(1.0 - 2.0 ** (-5.0 - jnp.arange(RET_HEADS, dtype=F32)))
    idx = jnp.arange(chunk, dtype=F32)
    rel = idx[:, None] - idx[None, :]
    dmask = jnp.where(rel >= 0, jnp.exp(log_g[:, None, None] * jnp.maximum(rel, 0.0)), 0.0)
    inner = jnp.exp(log_g[:, None] * (idx + 1.0))
    sdec = jnp.exp(log_g[:, None] * (chunk - 1.0 - idx))
    cdec = jnp.exp(log_g * chunk)
    return dmask, inner, sdec, cdec


def _rope_tables(pos):
    half = HEAD_DIM // 2
    inv = ROPE_BASE ** (-jnp.arange(half, dtype=F32) / half)
    ang = pos.astype(F32)[:, None] * inv[None, :]
    cos, sin = jnp.cos(ang), jnp.sin(ang)
    return jnp.concatenate([cos, cos], axis=-1), jnp.concatenate([-sin, sin], axis=-1)


def _mix_prompt(z, layer, p):
    n_seq, seq = z.shape[0], z.shape[1]
    w = W_GROUP
    dmask, inner, sdec, cdec = _retention_tables(RET_CHUNK)
    inner = jnp.broadcast_to(inner[:, :, None], (RET_HEADS, RET_CHUNK, HEAD_DIM))
    sdec = jnp.broadcast_to(sdec[:, :, None], (RET_HEADS, RET_CHUNK, HEAD_DIM))
    cdec = jnp.broadcast_to(cdec[:, None, None], (RET_HEADS, 1, HEAD_DIM))
    cc, ss = _rope_tables(jnp.arange(seq, dtype=jnp.int32))
    wa8 = jnp.broadcast_to(p["w_conv_a"][layer][:, None, :], (CONV_A_WIDTH, SUBLANES, w))

    def z_spec(k):
        return pl.BlockSpec((1, TL, w), lambda b, t: (b, t, k))

    def vec_spec():
        return pl.BlockSpec((None, 1, w), lambda b, t: (layer, 0, 0))

    def full_spec(shape):
        nd = len(shape)
        return pl.BlockSpec(shape, lambda b, t: (0,) * nd)

    def layer_spec(shape):
        nd = len(shape)
        return pl.BlockSpec((None,) + shape, lambda b, t: (layer,) + (0,) * nd)

    in_specs = [z_spec(k) for k in range(N_PROJ)] + [
        full_spec((CONV_A_WIDTH, SUBLANES, w)), vec_spec(), vec_spec(),
        layer_spec((CONV_B_WIDTH, w)), layer_spec((CONV_C_WIDTH, w)), vec_spec(),
        layer_spec((LRU_HEADS, HEAD_DIM, HEAD_DIM)), vec_spec(),
        layer_spec((LRU_HEADS, HEAD_DIM, HEAD_DIM)), vec_spec(), vec_spec(),
        pl.BlockSpec((TL, HEAD_DIM), lambda b, t: (t, 0)),
        pl.BlockSpec((TL, HEAD_DIM), lambda b, t: (t, 0)),
        full_spec((RET_HEADS, RET_CHUNK, RET_CHUNK)), full_spec((RET_HEADS, RET_CHUNK, HEAD_DIM)),
        full_spec((RET_HEADS, RET_CHUNK, HEAD_DIM)), full_spec((RET_HEADS, 1, HEAD_DIM)),
    ]
    out_specs = [
        pl.BlockSpec((1, TL, D_MODEL), lambda b, t: (b, t, 0)),
        pl.BlockSpec((1, CONV_A_WIDTH - 1, w), lambda b, t: (b, 0, 0)),
        pl.BlockSpec((1, CONV_B_WIDTH - 1, w), lambda b, t: (b, 0, 0)),
        pl.BlockSpec((1, CONV_C_WIDTH - 1, w), lambda b, t: (b, 0, 0)),
        pl.BlockSpec((1, 1, w), lambda b, t: (b, 0, 0)),
        pl.BlockSpec((1, RET_HEADS, HEAD_DIM, HEAD_DIM), lambda b, t: (b, 0, 0, 0)),
    ]
    out_shape = [
        jax.ShapeDtypeStruct((n_seq, seq, D_MODEL), BF16),
        jax.ShapeDtypeStruct((n_seq, CONV_A_WIDTH - 1, w), F32),
        jax.ShapeDtypeStruct((n_seq, CONV_B_WIDTH - 1, w), F32),
        jax.ShapeDtypeStruct((n_seq, CONV_C_WIDTH - 1, w), F32),
        jax.ShapeDtypeStruct((n_seq, 1, w), F32),
        jax.ShapeDtypeStruct((n_seq, RET_HEADS, HEAD_DIM, HEAD_DIM), F32),
    ]
    scratch = [
        pltpu.VMEM((SUBLANES, CONV_A_HALO + TL, w), F32),
        pltpu.VMEM((SUBLANES + TL, w), F32),
        pltpu.VMEM((SUBLANES + TL, w), F32),
        pltpu.VMEM((TL // SUBLANES, SUBLANES, w), F32),
        pltpu.VMEM((TL // SUBLANES, SUBLANES, w), F32),
        pltpu.VMEM((SUBLANES, w), F32),
        pltpu.VMEM((RET_HEADS, HEAD_DIM, HEAD_DIM), F32),
    ]
    return pl.pallas_call(
        _mix_prompt_kernel,
        grid=(n_seq, seq // TL),
        in_specs=in_specs,
        out_specs=out_specs,
        out_shape=out_shape,
        scratch_shapes=scratch,
        compiler_params=_cparams("arbitrary", "arbitrary"),
        name="mix_prompt",
    )(*([z] * N_PROJ), wa8, p["ln_a_g"], p["ln_a_b"], p["w_conv_b"], p["w_conv_c"], p["b_conv_c"],
      p["w_gate_a"], p["b_gate_a"], p["w_gate_x"], p["b_gate_x"], p["lru_lambda"],
      cc, ss, dmask, inner, sdec, cdec)


def _mix_sample_kernel(z_ref, sca_ref, scb_ref, scc_ref, slru_ref, sret_ref,
                       wa_ref, lng_ref, lnb_ref, wb_ref, wc_ref, bcc_ref,
                       wga_ref, bga_ref, wgx_ref, bgx_ref, lam_ref,
                       cc_ref, ss_ref, gdec_ref,
                       mix_ref, oca_ref, ocb_ref, occ_ref, olru_ref, oret_ref,
                       o_ref):
    w = W_GROUP
    bt = z_ref.shape[0]

    def zs(k):
        return z_ref[:, k * w:(k + 1) * w]

    u = zs(0) * jax.nn.sigmoid(zs(1))
    y = wa_ref[CONV_A_WIDTH - 1:CONV_A_WIDTH, :] * u
    for k in range(CONV_A_WIDTH - 1):
        y = y + wa_ref[k:k + 1, :] * sca_ref[:, k, :]
    mix_ref[:, 0:w] = _layernorm_silu(y, lng_ref[...], lnb_ref[...]).astype(BF16)
    oca_ref[:, 0:CONV_A_WIDTH - 2, :] = sca_ref[:, 1:CONV_A_WIDTH - 1, :]
    oca_ref[:, CONV_A_WIDTH - 2, :] = u

    v = zs(3) * zs(4)
    yb = wb_ref[0:1, :] * scb_ref[:, 0, :] + wb_ref[1:2, :] * scb_ref[:, 1, :] + wb_ref[2:3, :] * v
    mix_ref[:, w:2 * w] = (zs(2) * yb).astype(BF16)
    ocb_ref[:, 0, :] = scb_ref[:, 1, :]
    ocb_ref[:, 1, :] = v

    cx = zs(6)
    xr = (wc_ref[0:1, :] * scc_ref[:, 0, :] + wc_ref[1:2, :] * scc_ref[:, 1, :]
          + wc_ref[2:3, :] * scc_ref[:, 2, :] + wc_ref[3:4, :] * cx) + bcc_ref[...]
    occ_ref[:, 0, :] = scc_ref[:, 1, :]
    occ_ref[:, 1, :] = scc_ref[:, 2, :]
    occ_ref[:, 2, :] = cx
    a, b = _lru_gates(xr, wga_ref, bga_ref, wgx_ref, bgx_ref, lam_ref)
    h_new = a * slru_ref[...] + b
    olru_ref[...] = h_new
    mix_ref[:, 2 * w:3 * w] = (h_new * jax.nn.gelu(zs(5))).astype(BF16)

    scale = HEAD_DIM ** -0.5
    row0 = lax.broadcasted_iota(jnp.int32, (SUBLANES, HEAD_DIM), 0) == 0
    for h in range(RET_HEADS):
        cols = slice(h * HEAD_DIM, (h + 1) * HEAD_DIM)
        qr = _rope(z_ref[:, 7 * w + h * HEAD_DIM:7 * w + (h + 1) * HEAD_DIM], cc_ref[...], ss_ref[...])
        kr = _rope(z_ref[:, 8 * w + h * HEAD_DIM:8 * w + (h + 1) * HEAD_DIM], cc_ref[...], ss_ref[...]) * scale
        vv = z_ref[:, 9 * w + h * HEAD_DIM:9 * w + (h + 1) * HEAD_DIM]
        g = gdec_ref[h]
        for bi in range(bt):
            k8 = jnp.where(row0, jnp.broadcast_to(kr[bi:bi + 1, :], (SUBLANES, HEAD_DIM)), 0.0).astype(BF16)
            v8 = jnp.broadcast_to(vv[bi:bi + 1, :], (SUBLANES, HEAD_DIM)).astype(BF16)
            q8 = jnp.broadcast_to(qr[bi:bi + 1, :], (SUBLANES, HEAD_DIM)).astype(BF16)
            s_new = sret_ref[bi, h] * g + _dot_tn(k8, v8)
            oret_ref[bi, h] = s_new
            o8 = _dot(q8, s_new.astype(BF16))
            o_ref[bi:bi + 1, cols] = o8[0:1, :]
    for h in range(RET_HEADS):
        cols = slice(h * HEAD_DIM, (h + 1) * HEAD_DIM)
        o = o_ref[:, cols]
        on = o * lax.rsqrt(jnp.mean(o * o, axis=-1, keepdims=True) + EPS)
        mix_ref[:, 3 * w + h * HEAD_DIM:3 * w + (h + 1) * HEAD_DIM] = (
            jax.nn.silu(z_ref[:, 10 * w + h * HEAD_DIM:10 * w + (h + 1) * HEAD_DIM]) * on).astype(BF16)


def _mix_sample(z, layer, p, st):
    n_s = z.shape[0]
    w = W_GROUP
    _, _, _, gdec = _retention_tables(1)
    gdec = jnp.broadcast_to(gdec[:, None, None], (RET_HEADS, 1, HEAD_DIM))
    cc, ss = _rope_tables(PAST_LEN + jnp.arange(1, dtype=jnp.int32))

    def vec_spec():
        return pl.BlockSpec((None, 1, w), lambda i: (layer, 0, 0))

    def layer_spec(shape):
        nd = len(shape)
        return pl.BlockSpec((None,) + shape, lambda i: (layer,) + (0,) * nd)

    def state_spec(shape):
        nd = len(shape)
        return pl.BlockSpec((None, BT) + shape, lambda i: (layer, i) + (0,) * nd)

    def out_state_spec(shape):
        nd = len(shape)
        return pl.BlockSpec((BT,) + shape, lambda i: (i,) + (0,) * nd)

    state_shapes = [(CONV_A_WIDTH - 1, w), (CONV_B_WIDTH - 1, w), (CONV_C_WIDTH - 1, w), (w,),
                    (RET_HEADS, HEAD_DIM, HEAD_DIM)]
    in_specs = [pl.BlockSpec((BT, N_PROJ * w), lambda i: (i, 0))] + [state_spec(s) for s in state_shapes] + [
        layer_spec((CONV_A_WIDTH, w)), vec_spec(), vec_spec(),
        layer_spec((CONV_B_WIDTH, w)), layer_spec((CONV_C_WIDTH, w)), vec_spec(),
        layer_spec((LRU_HEADS, HEAD_DIM, HEAD_DIM)), vec_spec(),
        layer_spec((LRU_HEADS, HEAD_DIM, HEAD_DIM)), vec_spec(), vec_spec(),
        pl.BlockSpec((1, HEAD_DIM), lambda i: (0, 0)), pl.BlockSpec((1, HEAD_DIM), lambda i: (0, 0)),
        pl.BlockSpec((RET_HEADS, 1, HEAD_DIM), lambda i: (0, 0, 0)),
    ]
    out_specs = [pl.BlockSpec((BT, D_MODEL), lambda i: (i, 0))] + [out_state_spec(s) for s in state_shapes]
    out_shape = [jax.ShapeDtypeStruct((n_s, D_MODEL), BF16)] + [
        jax.ShapeDtypeStruct((n_s,) + s, F32) for s in state_shapes]
    return pl.pallas_call(
        _mix_sample_kernel,
        grid=(n_s // BT,),
        in_specs=in_specs,
        out_specs=out_specs,
        out_shape=out_shape,
        scratch_shapes=[pltpu.VMEM((BT, w), F32)],
        compiler_params=_cparams("arbitrary"),
        name="mix_sample",
    )(z, *st, p["w_conv_a"], p["ln_a_g"], p["ln_a_b"], p["w_conv_b"], p["w_conv_c"], p["b_conv_c"],
      p["w_gate_a"], p["b_gate_a"], p["w_gate_x"], p["b_gate_x"], p["lru_lambda"], cc, ss, gdec)


def kernel(x_prompt, x_sample, state_conv_a, state_conv_b, state_conv_c, state_lru_c, state_ret_d, state_conv_ffn,
           g_mix, w_in, w_conv_a, ln_a_g, ln_a_b, w_conv_b, w_conv_c, b_conv_c, w_gate_a, b_gate_a, w_gate_x,
           b_gate_x, lru_lambda, w_out, g_ffn, w_ffn_gate, w_ffn_up, w_conv_ffn, w_ffn_down, g_final):
    n_seq, seq, _ = x_prompt.shape
    n_s = x_sample.shape[0]
    depth = w_in.shape[0]
    assert x_sample.shape[1] == 1 and seq % TL == 0 and seq % TM == 0 and n_s % BT == 0
    def rows3(v):
        return v.reshape(depth, 1, v.shape[-1])

    p = dict(w_conv_a=w_conv_a, ln_a_g=rows3(ln_a_g), ln_a_b=rows3(ln_a_b), w_conv_b=w_conv_b,
             w_conv_c=w_conv_c, b_conv_c=rows3(b_conv_c), w_gate_a=w_gate_a, b_gate_a=rows3(b_gate_a),
             w_gate_x=w_gate_x, b_gate_x=rows3(b_gate_x), lru_lambda=rows3(lru_lambda))
    g_mix, g_ffn, g_final = rows3(g_mix), rows3(g_ffn), g_final.reshape(1, 1, D_MODEL)
    xp = x_prompt.reshape(n_seq * seq, D_MODEL)
    xs = x_sample.reshape(n_s, D_MODEL)
    sample_states = (state_conv_a, state_conv_b, state_conv_c, state_lru_c, state_ret_d)
    new_p = [[] for _ in range(6)]
    new_s = [[] for _ in range(6)]
    for layer in range(depth):
        hp = _rmsnorm(xp, g_mix, layer, BF16, TM)
        zp, zs = _inproj(hp, xs, g_mix, w_in, layer)
        mixp, pca, pcb, pcc, plru, pret = _mix_prompt(zp.reshape(n_seq, seq, N_PROJ * W_GROUP), layer, p)
        mixs, sca, scb, scc, slru, sret = _mix_sample(zs, layer, p, sample_states)
        xp, xs = _resproj(mixp.reshape(n_seq * seq, D_MODEL), mixs, w_out, layer, xp, xs, TM, "outproj")
        h2p = _rmsnorm(xp, g_ffn, layer, BF16, TM)
        hidp, hids, pcf, scf = _ffn_up(h2p, xs, g_ffn, w_ffn_gate, w_ffn_up, w_conv_ffn, state_conv_ffn,
                                       layer, n_seq)
        xp, xs = _resproj(hidp, hids, w_ffn_down, layer, xp, xs, TM_DOWN, "ffn_down")
        for lst, s in zip(new_p, (pca, pcb, pcc, plru.reshape(n_seq, W_GROUP), pret, pcf)):
            lst.append(s)
        for lst, s in zip(new_s, (sca, scb, scc, slru, sret, scf)):
            lst.append(s)
    y_prompt = _rmsnorm(xp, g_final, 0, F32, TM).reshape(n_seq, seq, D_MODEL)
    y_sample = _rmsnorm(xs, g_final, 0, F32, n_s).reshape(n_s, 1, D_MODEL)
    return (y_prompt, y_sample, *(jnp.stack(l) for l in new_p), *(jnp.stack(l) for l in new_s))
```

```python
import functools

import jax
import jax.numpy as jnp
from jax import lax
from jax.experimental import pallas as pl
from jax.experimental.pallas import tpu as pltpu

F32 = jnp.float32
BF16 = jnp.bfloat16

D_MODEL = 2048
N_GROUPS = 4
W_GROUP = D_MODEL // N_GROUPS
N_PROJ = 11
CONV_A_WIDTH = 31
CONV_B_WIDTH = 3
CONV_C_WIDTH = 4
CONV_FFN_WIDTH = 3
LRU_HEADS = 4
LRU_C = 8.0
RET_HEADS = 4
HEAD_DIM = W_GROUP // RET_HEADS
RET_CHUNK = 128
ROPE_BASE = 10000.0
D_FF = 5632
EPS = 1e-6
PAST_LEN = 16384

SUBLANES = 8
VMEM_LIMIT_BYTES = 56 * 1024 * 1024

TM = 2048
TN = 512
TM_DOWN = 512
FFN_ROWS = 256
MM_ROWS = 512
TM_NORM = 512
TL = 256
CONV_A_ROWS = 32
CONV_A_HALO = 32
BT = 16


def _cparams(*semantics):
    return pltpu.CompilerParams(dimension_semantics=semantics, vmem_limit_bytes=VMEM_LIMIT_BYTES)


def _rms_scale(x):
    return x * lax.rsqrt(jnp.mean(x * x, axis=-1, keepdims=True) + EPS)


def _row_chunks(n_rows):
    chunk = min(MM_ROWS, n_rows // 2)
    return [slice(r, r + chunk) for r in range(0, n_rows, chunk)]


def _skip_refs(body, n_in, n_skip):
    def wrapped(*refs):
        return body(*refs[:n_in], *refs[n_in + n_skip:])
    return wrapped


def _dot(a, b):
    return jnp.dot(a, b, preferred_element_type=F32)


def _dot_nt(a, b):
    return lax.dot_general(a, b, (((1,), (1,)), ((), ())), preferred_element_type=F32)


def _dot_tn(a, b):
    return lax.dot_general(a, b, (((0,), (0,)), ((), ())), preferred_element_type=F32)


def _rmsnorm_kernel(x_ref, g_ref, o_ref):
    o_ref[...] = (_rms_scale(x_ref[...]) * g_ref[...]).astype(o_ref.dtype)


def _rmsnorm(x, g_all, layer, out_dtype, tm):
    rows = x.shape[0]
    return pl.pallas_call(
        _rmsnorm_kernel,
        grid=(rows // tm,),
        in_specs=[pl.BlockSpec((tm, D_MODEL), lambda i: (i, 0)),
                  pl.BlockSpec((None, 1, D_MODEL), lambda i: (layer, 0, 0))],
        out_specs=pl.BlockSpec((tm, D_MODEL), lambda i: (i, 0)),
        out_shape=jax.ShapeDtypeStruct((rows, D_MODEL), out_dtype),
        compiler_params=_cparams("arbitrary"),
        name="rmsnorm",
    )(x, g_all)


def _inproj_kernel(hp_ref, xs_ref, g_ref, w_ref, zp_ref, zs_ref, wb_ref):
    @pl.when(pl.program_id(1) == 0)
    def _():
        wb_ref[...] = w_ref[...].astype(BF16)
        hs = (_rms_scale(xs_ref[...]) * g_ref[...]).astype(BF16)
        zs_ref[...] = _dot(hs, wb_ref[...])

    for rows in _row_chunks(hp_ref.shape[0]):
        zp_ref[rows, :] = _dot(hp_ref[rows, :], wb_ref[...])


def _inproj(hp, xs, g_all, w_all, layer):
    rows, n_s = hp.shape[0], xs.shape[0]
    n_out = w_all.shape[2]
    return pl.pallas_call(
        _inproj_kernel,
        grid=(n_out // TN, rows // TM),
        in_specs=[pl.BlockSpec((TM, D_MODEL), lambda j, i: (i, 0)),
                  pl.BlockSpec((n_s, D_MODEL), lambda j, i: (0, 0)),
                  pl.BlockSpec((None, 1, D_MODEL), lambda j, i: (layer, 0, 0)),
                  pl.BlockSpec((None, D_MODEL, TN), lambda j, i: (layer, 0, j))],
        out_specs=[pl.BlockSpec((TM, TN), lambda j, i: (i, j)),
                   pl.BlockSpec((n_s, TN), lambda j, i: (0, j))],
        out_shape=[jax.ShapeDtypeStruct((rows, n_out), F32),
                   jax.ShapeDtypeStruct((n_s, n_out), F32)],
        scratch_shapes=[pltpu.VMEM((D_MODEL, TN), BF16)],
        compiler_params=_cparams("arbitrary", "arbitrary"),
        name="inproj",
    )(hp, xs, g_all, w_all)


def _resproj_kernel(ap_ref, as_ref, w_ref, xp_ref, xs_ref, op_ref, os_ref, wb_ref):
    @pl.when(pl.program_id(1) == 0)
    def _():
        wb_ref[...] = w_ref[...].astype(BF16)
        os_ref[...] = xs_ref[...] + _dot(as_ref[...], wb_ref[...])

    for rows in _row_chunks(ap_ref.shape[0]):
        op_ref[rows, :] = xp_ref[rows, :] + _dot(ap_ref[rows, :], wb_ref[...])


def _resproj(ap, a_s, w_all, layer, xp, xs, tm, name):
    rows, n_s = ap.shape[0], a_s.shape[0]
    k_dim = ap.shape[1]
    return pl.pallas_call(
        _resproj_kernel,
        grid=(D_MODEL // TN, rows // tm),
        in_specs=[pl.BlockSpec((tm, k_dim), lambda j, i: (i, 0)),
                  pl.BlockSpec((n_s, k_dim), lambda j, i: (0, 0)),
                  pl.BlockSpec((None, k_dim, TN), lambda j, i: (layer, 0, j)),
                  pl.BlockSpec((tm, TN), lambda j, i: (i, j)),
                  pl.BlockSpec((n_s, TN), lambda j, i: (0, j))],
        out_specs=[pl.BlockSpec((tm, TN), lambda j, i: (i, j)),
                   pl.BlockSpec((n_s, TN), lambda j, i: (0, j))],
        out_shape=[jax.ShapeDtypeStruct((rows, D_MODEL), F32),
                   jax.ShapeDtypeStruct((n_s, D_MODEL), F32)],
        scratch_shapes=[pltpu.VMEM((k_dim, TN), BF16)],
        compiler_params=_cparams("arbitrary", "arbitrary"),
        name=name,
    )(ap, a_s, w_all, xp, xs)


def _ffn_up_kernel(tiles_per_seq, hp_ref, xs_ref, g_ref, wg_ref, wu_ref, wc_ref, st_ref,
                   hidp_ref, hids_ref, pst_ref, sst_ref, wgb_ref, wub_ref, carry_ref):
    i = pl.program_id(1)
    w0 = wc_ref[0:1, :]
    w1 = wc_ref[1:2, :]
    w2 = wc_ref[2:3, :]

    @pl.when(i == 0)
    def _():
        wgb_ref[...] = wg_ref[...].astype(BF16)
        wub_ref[...] = wu_ref[...].astype(BF16)
        hs = (_rms_scale(xs_ref[...]) * g_ref[...]).astype(BF16)
        gate = _dot(hs, wgb_ref[...])
        up = _dot(hs, wub_ref[...])
        s0 = st_ref[:, 0, :]
        s1 = st_ref[:, 1, :]
        gpre = w0 * s0 + w1 * s1 + w2 * gate
        hids_ref[...] = (jax.nn.silu(gpre) * up).astype(BF16)
        sst_ref[:, 0, :] = s1
        sst_ref[:, 1, :] = gate

    tm = hp_ref.shape[0]
    first = (i % tiles_per_seq) == 0
    prev = jnp.where(first, 0.0, carry_ref[...])
    row = lax.broadcasted_iota(jnp.int32, prev.shape, 0)
    for c in range(tm // FFN_ROWS):
        rows = slice(c * FFN_ROWS, (c + 1) * FFN_ROWS)
        hp = hp_ref[rows, :]
        gate = _dot(hp, wgb_ref[...])
        up = _dot(hp, wub_ref[...])
        r1 = pltpu.roll(gate, 1, axis=0)
        r2 = pltpu.roll(gate, 2, axis=0)
        top1 = jnp.where(row < 1, pltpu.roll(prev, 1, axis=0), r1[0:SUBLANES])
        top2 = jnp.where(row < 2, pltpu.roll(prev, 2, axis=0), r2[0:SUBLANES])
        g1 = jnp.concatenate([top1, r1[SUBLANES:]], axis=0)
        g2 = jnp.concatenate([top2, r2[SUBLANES:]], axis=0)
        gpre = w0 * g2 + w1 * g1 + w2 * gate
        hidp_ref[rows, :] = (jax.nn.silu(gpre) * up).astype(BF16)
        prev = gate[FFN_ROWS - SUBLANES:, :]
    carry_ref[...] = prev
    pst_ref[0] = prev[SUBLANES - (CONV_FFN_WIDTH - 1):, :]


def _ffn_up(hp, xs, g_all, wg_all, wu_all, wc_all, st_all, layer, n_seq, stacked):
    rows, n_s = hp.shape[0], xs.shape[0]
    depth = st_all.shape[0]
    tiles_per_seq = rows // n_seq // TM
    w_spec = pl.BlockSpec((None, D_MODEL, TN), lambda j, i: (layer, 0, j))
    in_specs = [pl.BlockSpec((TM, D_MODEL), lambda j, i: (i, 0)),
                pl.BlockSpec((n_s, D_MODEL), lambda j, i: (0, 0)),
                pl.BlockSpec((None, 1, D_MODEL), lambda j, i: (layer, 0, 0)),
                w_spec, w_spec,
                pl.BlockSpec((None, CONV_FFN_WIDTH, TN), lambda j, i: (layer, 0, j)),
                pl.BlockSpec((None, n_s, CONV_FFN_WIDTH - 1, TN), lambda j, i: (layer, 0, 0, j))]
    n_in = len(in_specs)
    stacked = () if stacked is None else (stacked,)
    in_specs = in_specs + [pl.BlockSpec(memory_space=pl.ANY)] * len(stacked)
    return pl.pallas_call(
        _skip_refs(functools.partial(_ffn_up_kernel, tiles_per_seq), n_in, len(stacked)),
        grid=(D_FF // TN, rows // TM),
        in_specs=in_specs,
        out_specs=[pl.BlockSpec((TM, TN), lambda j, i: (i, j)),
                   pl.BlockSpec((n_s, TN), lambda j, i: (0, j)),
                   pl.BlockSpec((1, CONV_FFN_WIDTH - 1, TN), lambda j, i: (i // tiles_per_seq, 0, j)),
                   pl.BlockSpec((None, n_s, CONV_FFN_WIDTH - 1, TN), lambda j, i: (layer, 0, 0, j))],
        out_shape=[jax.ShapeDtypeStruct((rows, D_FF), BF16),
                   jax.ShapeDtypeStruct((n_s, D_FF), BF16),
                   jax.ShapeDtypeStruct((n_seq, CONV_FFN_WIDTH - 1, D_FF), F32),
                   jax.ShapeDtypeStruct((depth, n_s, CONV_FFN_WIDTH - 1, D_FF), F32)],
        input_output_aliases={n_in + k: 3 for k in range(len(stacked))},
        scratch_shapes=[pltpu.VMEM((D_MODEL, TN), BF16),
                        pltpu.VMEM((D_MODEL, TN), BF16),
                        pltpu.VMEM((SUBLANES, TN), F32)],
        compiler_params=_cparams("arbitrary", "arbitrary"),
        name="ffn_up",
    )(hp, xs, g_all, wg_all, wu_all, wc_all, st_all, *stacked)


def _layernorm_silu(y, g, b):
    mu = jnp.mean(y, axis=-1, keepdims=True)
    yc = y - mu
    yn = yc * lax.rsqrt(jnp.mean(yc * yc, axis=-1, keepdims=True) + EPS)
    return jax.nn.silu(yn * g + b)


def _lru_gates(xr, wga_ref, bga_ref, wgx_ref, bgx_ref, lam_ref):
    ra, rx = [], []
    for h in range(LRU_HEADS):
        xh = xr[:, h * HEAD_DIM:(h + 1) * HEAD_DIM].astype(BF16)
        ra.append(_dot(xh, wga_ref[h].astype(BF16)))
        rx.append(_dot(xh, wgx_ref[h].astype(BF16)))
    r = jax.nn.sigmoid(jnp.concatenate(ra, axis=-1) + bga_ref[...])
    ig = jax.nn.sigmoid(jnp.concatenate(rx, axis=-1) + bgx_ref[...])
    log_a = -LRU_C * r * jax.nn.softplus(-lam_ref[...])
    a = jnp.exp(log_a)
    one_minus_a2 = -jnp.tanh(log_a) * (a * a + 1.0)
    b = jnp.sqrt(jnp.maximum(one_minus_a2, 0.0)) * (ig * xr)
    return a, b


def _rope(t, cc, ss):
    return t * cc + pltpu.roll(t, HEAD_DIM // 2, axis=1) * ss


def _mix_prompt_kernel(av_ref, ag_ref, bb_ref, bc_ref, bx_ref, cg_ref, cx_ref, dq_ref, dk_ref, dv_ref, dg_ref,
                       wa_ref, lng_ref, lnb_ref, wb_ref, wc_ref, bcc_ref,
                       wga_ref, bga_ref, wgx_ref, bgx_ref, lam_ref,
                       cc_ref, ss_ref, dmask_ref, inner_ref, sdec_ref, cdec_ref,
                       mix_ref, pca_ref, pcb_ref, pcc_ref, plru_ref, pret_ref,
                       ub_ref, vb_ref, xb_ref, as_ref, bs_ref, hl_ref, s_ref):
    tl = av_ref.shape[1]
    w = W_GROUP

    @pl.when(pl.program_id(1) == 0)
    def _():
        ub_ref[0, 0:CONV_A_HALO, :] = jnp.zeros((CONV_A_HALO, w), F32)
        vb_ref[0:SUBLANES, :] = jnp.zeros((SUBLANES, w), F32)
        xb_ref[0:SUBLANES, :] = jnp.zeros((SUBLANES, w), F32)
        hl_ref[...] = jnp.zeros(hl_ref.shape, F32)
        s_ref[...] = jnp.zeros(s_ref.shape, F32)

    ub_ref[0, CONV_A_HALO:CONV_A_HALO + tl, :] = av_ref[0] * jax.nn.sigmoid(ag_ref[0])
    n_shift_rows = tl + CONV_A_HALO - SUBLANES
    for s in range(1, SUBLANES):
        ub_ref[s, 0:n_shift_rows, :] = ub_ref[0, s:s + n_shift_rows, :]
    tap0 = CONV_A_HALO - (CONV_A_WIDTH - 1)
    n_sub = CONV_A_ROWS // SUBLANES

    def conv_a_chunk(r, carry):
        base = pl.multiple_of(r * CONV_A_ROWS, CONV_A_ROWS)
        acc = [jnp.zeros((SUBLANES, w), F32) for _ in range(n_sub)]
        for k in range(CONV_A_WIDTH):
            blk, s = divmod(tap0 + k, SUBLANES)
            wk = wa_ref[k]
            for j in range(n_sub):
                start = pl.multiple_of(base + (blk + j) * SUBLANES, SUBLANES)
                acc[j] = acc[j] + wk * ub_ref[s, pl.ds(start, SUBLANES), :]
        y = jnp.concatenate(acc, axis=0)
        ya = _layernorm_silu(y, lng_ref[...], lnb_ref[...])
        mix_ref[0, pl.ds(base, CONV_A_ROWS), 0:w] = ya.astype(BF16)
        return carry

    lax.fori_loop(0, tl // CONV_A_ROWS, conv_a_chunk, 0)
    pca_ref[0] = ub_ref[0, pl.ds(CONV_A_HALO + tl - (CONV_A_WIDTH - 1), CONV_A_WIDTH - 1), :]
    ub_ref[0, 0:CONV_A_HALO, :] = ub_ref[0, tl:tl + CONV_A_HALO, :]

    v = bc_ref[0] * bx_ref[0]
    vb_ref[SUBLANES:SUBLANES + tl, :] = v
    yb = (wb_ref[0:1, :] * vb_ref[pl.ds(SUBLANES - 2, tl), :]
          + wb_ref[1:2, :] * vb_ref[pl.ds(SUBLANES - 1, tl), :]
          + wb_ref[2:3, :] * v)
    mix_ref[0, :, w:2 * w] = (bb_ref[0] * yb).astype(BF16)
    pcb_ref[0] = vb_ref[pl.ds(SUBLANES + tl - 2, 2), :]
    vb_ref[0:SUBLANES, :] = vb_ref[tl:tl + SUBLANES, :]

    cx = cx_ref[0]
    xb_ref[SUBLANES:SUBLANES + tl, :] = cx
    xr = (wc_ref[0:1, :] * xb_ref[pl.ds(SUBLANES - 3, tl), :]
          + wc_ref[1:2, :] * xb_ref[pl.ds(SUBLANES - 2, tl), :]
          + wc_ref[2:3, :] * xb_ref[pl.ds(SUBLANES - 1, tl), :]
          + wc_ref[3:4, :] * cx) + bcc_ref[...]
    pcc_ref[0] = xb_ref[pl.ds(SUBLANES + tl - 3, 3), :]
    xb_ref[0:SUBLANES, :] = xb_ref[tl:tl + SUBLANES, :]
    a, b = _lru_gates(xr, wga_ref, bga_ref, wgx_ref, bgx_ref, lam_ref)
    nblk = tl // SUBLANES
    a3 = a.reshape(nblk, SUBLANES, w)
    b3 = b.reshape(nblk, SUBLANES, w)
    row = lax.broadcasted_iota(jnp.int32, a3.shape, 1)
    for d in (1, 2, 4):
        keep = row >= d
        a_sh = jnp.where(keep, pltpu.roll(a3, d, axis=1), 1.0)
        b_sh = jnp.where(keep, pltpu.roll(b3, d, axis=1), 0.0)
        b3 = a3 * b_sh + b3
        a3 = a3 * a_sh
    as_ref[...] = a3
    bs_ref[...] = b3

    def lru_block(i, h_in):
        h = as_ref[i] * h_in + bs_ref[i]
        bs_ref[i] = h
        return jnp.broadcast_to(h[SUBLANES - 1:SUBLANES, :], (SUBLANES, w))

    h_last = lax.fori_loop(0, nblk, lru_block, hl_ref[...])
    hl_ref[...] = h_last
    plru_ref[0] = h_last[0:1, :]
    h_seq = bs_ref[...].reshape(tl, w)
    mix_ref[0, :, 2 * w:3 * w] = (h_seq * jax.nn.gelu(cg_ref[0])).astype(BF16)

    scale = HEAD_DIM ** -0.5
    for c in range(tl // RET_CHUNK):
        rows = slice(c * RET_CHUNK, (c + 1) * RET_CHUNK)
        cc = cc_ref[rows, :]
        ss = ss_ref[rows, :]
        for h in range(RET_HEADS):
            cols = slice(h * HEAD_DIM, (h + 1) * HEAD_DIM)
            qr = _rope(dq_ref[0, rows, cols], cc, ss)
            kr = _rope(dk_ref[0, rows, cols], cc, ss) * scale
            vv = dv_ref[0, rows, cols].astype(BF16)
            qb = qr.astype(BF16)
            scores = _dot_nt(qb, kr.astype(BF16)) * dmask_ref[h]
            s_old = s_ref[h]
            o = _dot(scores.astype(BF16), vv) + _dot(qb, s_old.astype(BF16)) * inner_ref[h]
            s_ref[h] = s_old * cdec_ref[h] + _dot_tn((kr * sdec_ref[h]).astype(BF16), vv)
            on = o * lax.rsqrt(jnp.mean(o * o, axis=-1, keepdims=True) + EPS)
            mix_ref[0, rows, 3 * w + h * HEAD_DIM:3 * w + (h + 1) * HEAD_DIM] = (
                jax.nn.silu(dg_ref[0, rows, cols]) * on).astype(BF16)
    pret_ref[0] = s_ref[...]


def _retention_tables(chunk):
    log_g = jnp.log(1.0 - 2.0 ** (-5.0 - jnp.arange(RET_HEADS, dtype=F32)))
    idx = jnp.arange(chunk, dtype=F32)
    rel = idx[:, None] - idx[None, :]
    dmask = jnp.where(rel >= 0, jnp.exp(log_g[:, None, None] * jnp.maximum(rel, 0.0)), 0.0)
    inner = jnp.exp(log_g[:, None] * (idx + 1.0))
    sdec = jnp.exp(log_g[:, None] * (chunk - 1.0 - idx))
    cdec = jnp.exp(log_g * chunk)
    return dmask, inner, sdec, cdec


def _rope_tables(pos):
    half = HEAD_DIM // 2
    inv = ROPE_BASE ** (-jnp.arange(half, dtype=F32) / half)
    ang = pos.astype(F32)[:, None] * inv[None, :]
    cos, sin = jnp.cos(ang), jnp.sin(ang)
    return jnp.concatenate([cos, cos], axis=-1), jnp.concatenate([-sin, sin], axis=-1)


def _mix_prompt(z, layer, p):
    n_seq, seq = z.shape[0], z.shape[1]
    w = W_GROUP
    dmask, inner, sdec, cdec = _retention_tables(RET_CHUNK)
    inner = jnp.broadcast_to(inner[:, :, None], (RET_HEADS, RET_CHUNK, HEAD_DIM))
    sdec = jnp.broadcast_to(sdec[:, :, None], (RET_HEADS, RET_CHUNK, HEAD_DIM))
    cdec = jnp.broadcast_to(cdec[:, None, None], (RET_HEADS, 1, HEAD_DIM))
    cc, ss = _rope_tables(jnp.arange(seq, dtype=jnp.int32))
    wa8 = jnp.broadcast_to(p["w_conv_a"][layer][:, None, :], (CONV_A_WIDTH, SUBLANES, w))

    def z_spec(k):
        return pl.BlockSpec((1, TL, w), lambda b, t: (b, t, k))

    def vec_spec():
        return pl.BlockSpec((None, 1, w), lambda b, t: (layer, 0, 0))

    def full_spec(shape):
        nd = len(shape)
        return pl.BlockSpec(shape, lambda b, t: (0,) * nd)

    def layer_spec(shape):
        nd = len(shape)
        return pl.BlockSpec((None,) + shape, lambda b, t: (layer,) + (0,) * nd)

    in_specs = [z_spec(k) for k in range(N_PROJ)] + [
        full_spec((CONV_A_WIDTH, SUBLANES, w)), vec_spec(), vec_spec(),
        layer_spec((CONV_B_WIDTH, w)), layer_spec((CONV_C_WIDTH, w)), vec_spec(),
        layer_spec((LRU_HEADS, HEAD_DIM, HEAD_DIM)), vec_spec(),
        layer_spec((LRU_HEADS, HEAD_DIM, HEAD_DIM)), vec_spec(), vec_spec(),
        pl.BlockSpec((TL, HEAD_DIM), lambda b, t: (t, 0)),
        pl.BlockSpec((TL, HEAD_DIM), lambda b, t: (t, 0)),
        full_spec((RET_HEADS, RET_CHUNK, RET_CHUNK)), full_spec((RET_HEADS, RET_CHUNK, HEAD_DIM)),
        full_spec((RET_HEADS, RET_CHUNK, HEAD_DIM)), full_spec((RET_HEADS, 1, HEAD_DIM)),
    ]
    out_specs = [
        pl.BlockSpec((1, TL, D_MODEL), lambda b, t: (b, t, 0)),
        pl.BlockSpec((1, CONV_A_WIDTH - 1, w), lambda b, t: (b, 0, 0)),
        pl.BlockSpec((1, CONV_B_WIDTH - 1, w), lambda b, t: (b, 0, 0)),
        pl.BlockSpec((1, CONV_C_WIDTH - 1, w), lambda b, t: (b, 0, 0)),
        pl.BlockSpec((1, 1, w), lambda b, t: (b, 0, 0)),
        pl.BlockSpec((1, RET_HEADS, HEAD_DIM, HEAD_DIM), lambda b, t: (b, 0, 0, 0)),
    ]
    out_shape = [
        jax.ShapeDtypeStruct((n_seq, seq, D_MODEL), BF16),
        jax.ShapeDtypeStruct((n_seq, CONV_A_WIDTH - 1, w), F32),
        jax.ShapeDtypeStruct((n_seq, CONV_B_WIDTH - 1, w), F32),
        jax.ShapeDtypeStruct((n_seq, CONV_C_WIDTH - 1, w), F32),
        jax.ShapeDtypeStruct((n_seq, 1, w), F32),
        jax.ShapeDtypeStruct((n_seq, RET_HEADS, HEAD_DIM, HEAD_DIM), F32),
    ]
    scratch = [
        pltpu.VMEM((SUBLANES, CONV_A_HALO + TL, w), F32),
        pltpu.VMEM((SUBLANES + TL, w), F32),
        pltpu.VMEM((SUBLANES + TL, w), F32),
        pltpu.VMEM((TL // SUBLANES, SUBLANES, w), F32),
        pltpu.VMEM((TL // SUBLANES, SUBLANES, w), F32),
        pltpu.VMEM((SUBLANES, w), F32),
        pltpu.VMEM((RET_HEADS, HEAD_DIM, HEAD_DIM), F32),
    ]
    return pl.pallas_call(
        _mix_prompt_kernel,
        grid=(n_seq, seq // TL),
        in_specs=in_specs,
        out_specs=out_specs,
        out_shape=out_shape,
        scratch_shapes=scratch,
        compiler_params=_cparams("arbitrary", "arbitrary"),
        name="mix_prompt",
    )(*([z] * N_PROJ), wa8, p["ln_a_g"], p["ln_a_b"], p["w_conv_b"], p["w_conv_c"], p["b_conv_c"],
      p["w_gate_a"], p["b_gate_a"], p["w_gate_x"], p["b_gate_x"], p["lru_lambda"],
      cc, ss, dmask, inner, sdec, cdec)


def _mix_sample_kernel(z_ref, sca_ref, scb_ref, scc_ref, slru_ref, sret_ref,
                       wa_ref, lng_ref, lnb_ref, wb_ref, wc_ref, bcc_ref,
                       wga_ref, bga_ref, wgx_ref, bgx_ref, lam_ref,
                       cc_ref, ss_ref, gdec_ref,
                       mix_ref, oca_ref, ocb_ref, occ_ref, olru_ref, oret_ref,
                       o_ref):
    w = W_GROUP
    bt = z_ref.shape[0]

    def zs(k):
        return z_ref[:, k * w:(k + 1) * w]

    u = zs(0) * jax.nn.sigmoid(zs(1))
    y = wa_ref[CONV_A_WIDTH - 1:CONV_A_WIDTH, :] * u
    for k in range(CONV_A_WIDTH - 1):
        y = y + wa_ref[k:k + 1, :] * sca_ref[:, k, :]
    mix_ref[:, 0:w] = _layernorm_silu(y, lng_ref[...], lnb_ref[...]).astype(BF16)
    oca_ref[:, 0:CONV_A_WIDTH - 2, :] = sca_ref[:, 1:CONV_A_WIDTH - 1, :]
    oca_ref[:, CONV_A_WIDTH - 2, :] = u

    v = zs(3) * zs(4)
    yb = wb_ref[0:1, :] * scb_ref[:, 0, :] + wb_ref[1:2, :] * scb_ref[:, 1, :] + wb_ref[2:3, :] * v
    mix_ref[:, w:2 * w] = (zs(2) * yb).astype(BF16)
    ocb_ref[:, 0, :] = scb_ref[:, 1, :]
    ocb_ref[:, 1, :] = v

    cx = zs(6)
    xr = (wc_ref[0:1, :] * scc_ref[:, 0, :] + wc_ref[1:2, :] * scc_ref[:, 1, :]
          + wc_ref[2:3, :] * scc_ref[:, 2, :] + wc_ref[3:4, :] * cx) + bcc_ref[...]
    occ_ref[:, 0, :] = scc_ref[:, 1, :]
    occ_ref[:, 1, :] = scc_ref[:, 2, :]
    occ_ref[:, 2, :] = cx
    a, b = _lru_gates(xr, wga_ref, bga_ref, wgx_ref, bgx_ref, lam_ref)
    h_new = a * slru_ref[...] + b
    olru_ref[...] = h_new
    mix_ref[:, 2 * w:3 * w] = (h_new * jax.nn.gelu(zs(5))).astype(BF16)

    scale = HEAD_DIM ** -0.5
    row0 = lax.broadcasted_iota(jnp.int32, (SUBLANES, HEAD_DIM), 0) == 0
    for h in range(RET_HEADS):
        cols = slice(h * HEAD_DIM, (h + 1) * HEAD_DIM)
        qr = _rope(z_ref[:, 7 * w + h * HEAD_DIM:7 * w + (h + 1) * HEAD_DIM], cc_ref[...], ss_ref[...])
        kr = _rope(z_ref[:, 8 * w + h * HEAD_DIM:8 * w + (h + 1) * HEAD_DIM], cc_ref[...], ss_ref[...]) * scale
        vv = z_ref[:, 9 * w + h * HEAD_DIM:9 * w + (h + 1) * HEAD_DIM]
        g = gdec_ref[h]
        for bi in range(bt):
            k8 = jnp.where(row0, jnp.broadcast_to(kr[bi:bi + 1, :], (SUBLANES, HEAD_DIM)), 0.0).astype(BF16)
            v8 = jnp.broadcast_to(vv[bi:bi + 1, :], (SUBLANES, HEAD_DIM)).astype(BF16)
            q8 = jnp.broadcast_to(qr[bi:bi + 1, :], (SUBLANES, HEAD_DIM)).astype(BF16)
            s_new = sret_ref[bi, h] * g + _dot_tn(k8, v8)
            oret_ref[bi, h] = s_new
            o8 = _dot(q8, s_new.astype(BF16))
            o_ref[bi:bi + 1, cols] = o8[0:1, :]
    for h in range(RET_HEADS):
        cols = slice(h * HEAD_DIM, (h + 1) * HEAD_DIM)
        o = o_ref[:, cols]
        on = o * lax.rsqrt(jnp.mean(o * o, axis=-1, keepdims=True) + EPS)
        mix_ref[:, 3 * w + h * HEAD_DIM:3 * w + (h + 1) * HEAD_DIM] = (
            jax.nn.silu(z_ref[:, 10 * w + h * HEAD_DIM:10 * w + (h + 1) * HEAD_DIM]) * on).astype(BF16)


def _mix_sample(z, layer, p, st, stacked):
    n_s = z.shape[0]
    depth = st[0].shape[0]
    w = W_GROUP
    _, _, _, gdec = _retention_tables(1)
    gdec = jnp.broadcast_to(gdec[:, None, None], (RET_HEADS, 1, HEAD_DIM))
    cc, ss = _rope_tables(PAST_LEN + jnp.arange(1, dtype=jnp.int32))

    def vec_spec():
        return pl.BlockSpec((None, 1, w), lambda i: (layer, 0, 0))

    def layer_spec(shape):
        nd = len(shape)
        return pl.BlockSpec((None,) + shape, lambda i: (layer,) + (0,) * nd)

    def state_spec(shape):
        nd = len(shape)
        return pl.BlockSpec((None, BT) + shape, lambda i: (layer, i) + (0,) * nd)

    def out_state_spec(shape):
        nd = len(shape)
        return pl.BlockSpec((None, BT) + shape, lambda i: (layer, i) + (0,) * nd)

    state_shapes = [(CONV_A_WIDTH - 1, w), (CONV_B_WIDTH - 1, w), (CONV_C_WIDTH - 1, w), (w,),
                    (RET_HEADS, HEAD_DIM, HEAD_DIM)]
    in_specs = [pl.BlockSpec((BT, N_PROJ * w), lambda i: (i, 0))] + [state_spec(s) for s in state_shapes] + [
        layer_spec((CONV_A_WIDTH, w)), vec_spec(), vec_spec(),
        layer_spec((CONV_B_WIDTH, w)), layer_spec((CONV_C_WIDTH, w)), vec_spec(),
        layer_spec((LRU_HEADS, HEAD_DIM, HEAD_DIM)), vec_spec(),
        layer_spec((LRU_HEADS, HEAD_DIM, HEAD_DIM)), vec_spec(), vec_spec(),
        pl.BlockSpec((1, HEAD_DIM), lambda i: (0, 0)), pl.BlockSpec((1, HEAD_DIM), lambda i: (0, 0)),
        pl.BlockSpec((RET_HEADS, 1, HEAD_DIM), lambda i: (0, 0, 0)),
    ]
    out_specs = [pl.BlockSpec((BT, D_MODEL), lambda i: (i, 0))] + [out_state_spec(s) for s in state_shapes]
    out_shape = [jax.ShapeDtypeStruct((n_s, D_MODEL), BF16)] + [
        jax.ShapeDtypeStruct((depth, n_s) + s, F32) for s in state_shapes]
    n_in = len(in_specs)
    stacked = () if stacked is None else tuple(stacked)
    in_specs = in_specs + [pl.BlockSpec(memory_space=pl.ANY)] * len(stacked)
    return pl.pallas_call(
        _skip_refs(_mix_sample_kernel, n_in, len(stacked)),
        grid=(n_s // BT,),
        in_specs=in_specs,
        input_output_aliases={n_in + k: 1 + k for k in range(len(stacked))},
        out_specs=out_specs,
        out_shape=out_shape,
        scratch_shapes=[pltpu.VMEM((BT, w), F32)],
        compiler_params=_cparams("arbitrary"),
        name="mix_sample",
    )(z, *st, p["w_conv_a"], p["ln_a_g"], p["ln_a_b"], p["w_conv_b"], p["w_conv_c"], p["b_conv_c"],
      p["w_gate_a"], p["b_gate_a"], p["w_gate_x"], p["b_gate_x"], p["lru_lambda"], cc, ss, gdec, *stacked)


def kernel(x_prompt, x_sample, state_conv_a, state_conv_b, state_conv_c, state_lru_c, state_ret_d, state_conv_ffn,
           g_mix, w_in, w_conv_a, ln_a_g, ln_a_b, w_conv_b, w_conv_c, b_conv_c, w_gate_a, b_gate_a, w_gate_x,
           b_gate_x, lru_lambda, w_out, g_ffn, w_ffn_gate, w_ffn_up, w_conv_ffn, w_ffn_down, g_final):
    n_seq, seq, _ = x_prompt.shape
    n_s = x_sample.shape[0]
    depth = w_in.shape[0]
    assert x_sample.shape[1] == 1 and seq % TL == 0 and seq % TM == 0 and n_s % BT == 0
    def rows3(v):
        return v.reshape(depth, 1, v.shape[-1])

    p = dict(w_conv_a=w_conv_a, ln_a_g=rows3(ln_a_g), ln_a_b=rows3(ln_a_b), w_conv_b=w_conv_b,
             w_conv_c=w_conv_c, b_conv_c=rows3(b_conv_c), w_gate_a=w_gate_a, b_gate_a=rows3(b_gate_a),
             w_gate_x=w_gate_x, b_gate_x=rows3(b_gate_x), lru_lambda=rows3(lru_lambda))
    g_mix, g_ffn, g_final = rows3(g_mix), rows3(g_ffn), g_final.reshape(1, 1, D_MODEL)
    xp = x_prompt.reshape(n_seq * seq, D_MODEL)
    xs = x_sample.reshape(n_s, D_MODEL)
    sample_states = (state_conv_a, state_conv_b, state_conv_c, state_lru_c, state_ret_d)
    new_p = [[] for _ in range(6)]
    mix_states, scf = None, None
    for layer in range(depth):
        hp = _rmsnorm(xp, g_mix, layer, BF16, TM_NORM)
        zp, zs = _inproj(hp, xs, g_mix, w_in, layer)
        mixp, pca, pcb, pcc, plru, pret = _mix_prompt(zp.reshape(n_seq, seq, N_PROJ * W_GROUP), layer, p)
        mixs, *mix_states = _mix_sample(zs, layer, p, sample_states, mix_states)
        xp, xs = _resproj(mixp.reshape(n_seq * seq, D_MODEL), mixs, w_out, layer, xp, xs, TM, "outproj")
        h2p = _rmsnorm(xp, g_ffn, layer, BF16, TM_NORM)
        hidp, hids, pcf, scf = _ffn_up(h2p, xs, g_ffn, w_ffn_gate, w_ffn_up, w_conv_ffn, state_conv_ffn,
                                       layer, n_seq, scf)
        xp, xs = _resproj(hidp, hids, w_ffn_down, layer, xp, xs, TM_DOWN, "ffn_down")
        for lst, s in zip(new_p, (pca, pcb, pcc, plru.reshape(n_seq, W_GROUP), pret, pcf)):
            lst.append(s)
    y_prompt = _rmsnorm(xp, g_final, 0, F32, TM_NORM).reshape(n_seq, seq, D_MODEL)
    y_sample = _rmsnorm(xs, g_final, 0, F32, n_s).reshape(n_s, 1, D_MODEL)
    return (y_prompt, y_sample, *(jnp.stack(l) for l in new_p), *mix_states, scf)
```

```python
import functools

import jax
import jax.numpy as jnp
from jax import lax
from jax.experimental import pallas as pl
from jax.experimental.pallas import tpu as pltpu

F32 = jnp.float32
BF16 = jnp.bfloat16

D_MODEL = 2048
N_GROUPS = 4
W_GROUP = D_MODEL // N_GROUPS
N_PROJ = 11
CONV_A_WIDTH = 31
CONV_B_WIDTH = 3
CONV_C_WIDTH = 4
CONV_FFN_WIDTH = 3
LRU_HEADS = 4
LRU_C = 8.0
RET_HEADS = 4
HEAD_DIM = W_GROUP // RET_HEADS
RET_CHUNK = 128
ROPE_BASE = 10000.0
D_FF = 5632
EPS = 1e-6
PAST_LEN = 16384

SUBLANES = 8
VMEM_LIMIT_BYTES = 56 * 1024 * 1024

TM = 2048
TN = 512
TM_DOWN = 512
FFN_ROWS = 256
MM_ROWS = 512
TM_NORM = 512
TL = 128
W_CAST_ROWS = 128
CONV_A_ROWS = 32
CONV_A_HALO = 32
BT = 16


def _cparams(*semantics):
    return pltpu.CompilerParams(dimension_semantics=semantics, vmem_limit_bytes=VMEM_LIMIT_BYTES)


def _rms_scale(x):
    return x * lax.rsqrt(jnp.mean(x * x, axis=-1, keepdims=True) + EPS)


def _row_chunks(n_rows):
    chunk = min(MM_ROWS, n_rows // 2)
    return [slice(r, r + chunk) for r in range(0, n_rows, chunk)]


def _skip_refs(body, n_in, n_skip):
    def wrapped(*refs):
        return body(*refs[:n_in], *refs[n_in + n_skip:])
    return wrapped


def _dot(a, b):
    return jnp.dot(a, b, preferred_element_type=F32)


def _dot_nt(a, b):
    return lax.dot_general(a, b, (((1,), (1,)), ((), ())), preferred_element_type=F32)


def _dot_tn(a, b):
    return lax.dot_general(a, b, (((0,), (0,)), ((), ())), preferred_element_type=F32)


def _rmsnorm_kernel(x_ref, g_ref, o_ref):
    o_ref[...] = (_rms_scale(x_ref[...]) * g_ref[...]).astype(o_ref.dtype)


def _rmsnorm(x, g_all, layer, out_dtype, tm):
    rows = x.shape[0]
    return pl.pallas_call(
        _rmsnorm_kernel,
        grid=(rows // tm,),
        in_specs=[pl.BlockSpec((tm, D_MODEL), lambda i: (i, 0)),
                  pl.BlockSpec((None, 1, D_MODEL), lambda i: (layer, 0, 0))],
        out_specs=pl.BlockSpec((tm, D_MODEL), lambda i: (i, 0)),
        out_shape=jax.ShapeDtypeStruct((rows, D_MODEL), out_dtype),
        compiler_params=_cparams("arbitrary"),
        name="rmsnorm",
    )(x, g_all)


def _sample_inproj_kernel(xs_ref, g_ref, w_ref, zs_ref):
    hs = (_rms_scale(xs_ref[...]) * g_ref[...]).astype(BF16)
    zs_ref[...] = _dot(hs, w_ref[...].astype(BF16))


def _sample_inproj(xs, g_all, w_all, layer):
    n_s = xs.shape[0]
    n_out = w_all.shape[2]
    return pl.pallas_call(
        _sample_inproj_kernel,
        grid=(n_out // TN,),
        in_specs=[pl.BlockSpec((n_s, D_MODEL), lambda j: (0, 0)),
                  pl.BlockSpec((None, 1, D_MODEL), lambda j: (layer, 0, 0)),
                  pl.BlockSpec((None, D_MODEL, TN), lambda j: (layer, 0, j))],
        out_specs=pl.BlockSpec((n_s, TN), lambda j: (0, j)),
        out_shape=jax.ShapeDtypeStruct((n_s, n_out), F32),
        compiler_params=_cparams("arbitrary"),
        name="sample_inproj",
    )(xs, g_all, w_all)


def _sample_outproj_kernel(a_ref, w_ref, xs_ref, os_ref):
    os_ref[...] = xs_ref[...] + _dot(a_ref[...], w_ref[...].astype(BF16))


def _sample_outproj(a_s, w_all, layer, xs):
    n_s = xs.shape[0]
    return pl.pallas_call(
        _sample_outproj_kernel,
        grid=(D_MODEL // TN,),
        in_specs=[pl.BlockSpec((n_s, D_MODEL), lambda j: (0, 0)),
                  pl.BlockSpec((None, D_MODEL, TN), lambda j: (layer, 0, j)),
                  pl.BlockSpec((n_s, TN), lambda j: (0, j))],
        out_specs=pl.BlockSpec((n_s, TN), lambda j: (0, j)),
        out_shape=jax.ShapeDtypeStruct((n_s, D_MODEL), F32),
        compiler_params=_cparams("arbitrary"),
        name="sample_outproj",
    )(a_s, w_all, xs)


def _resproj_kernel(ap_ref, as_ref, w_ref, xp_ref, xs_ref, op_ref, os_ref, wb_ref):
    @pl.when(pl.program_id(1) == 0)
    def _():
        wb_ref[...] = w_ref[...].astype(BF16)
        os_ref[...] = xs_ref[...] + _dot(as_ref[...], wb_ref[...])

    for rows in _row_chunks(ap_ref.shape[0]):
        op_ref[rows, :] = xp_ref[rows, :] + _dot(ap_ref[rows, :], wb_ref[...])


def _resproj(ap, a_s, w_all, layer, xp, xs, tm, name):
    rows, n_s = ap.shape[0], a_s.shape[0]
    k_dim = ap.shape[1]
    return pl.pallas_call(
        _resproj_kernel,
        grid=(D_MODEL // TN, rows // tm),
        in_specs=[pl.BlockSpec((tm, k_dim), lambda j, i: (i, 0)),
                  pl.BlockSpec((n_s, k_dim), lambda j, i: (0, 0)),
                  pl.BlockSpec((None, k_dim, TN), lambda j, i: (layer, 0, j)),
                  pl.BlockSpec((tm, TN), lambda j, i: (i, j)),
                  pl.BlockSpec((n_s, TN), lambda j, i: (0, j))],
        out_specs=[pl.BlockSpec((tm, TN), lambda j, i: (i, j)),
                   pl.BlockSpec((n_s, TN), lambda j, i: (0, j))],
        out_shape=[jax.ShapeDtypeStruct((rows, D_MODEL), F32),
                   jax.ShapeDtypeStruct((n_s, D_MODEL), F32)],
        scratch_shapes=[pltpu.VMEM((k_dim, TN), BF16)],
        compiler_params=_cparams("arbitrary", "arbitrary"),
        name=name,
    )(ap, a_s, w_all, xp, xs)


def _ffn_up_kernel(tiles_per_seq, hp_ref, xs_ref, g_ref, wg_ref, wu_ref, wc_ref, st_ref,
                   hidp_ref, hids_ref, pst_ref, sst_ref, wgb_ref, wub_ref, carry_ref):
    i = pl.program_id(1)
    w0 = wc_ref[0:1, :]
    w1 = wc_ref[1:2, :]
    w2 = wc_ref[2:3, :]

    @pl.when(i == 0)
    def _():
        wgb_ref[...] = wg_ref[...].astype(BF16)
        wub_ref[...] = wu_ref[...].astype(BF16)
        hs = (_rms_scale(xs_ref[...]) * g_ref[...]).astype(BF16)
        gate = _dot(hs, wgb_ref[...])
        up = _dot(hs, wub_ref[...])
        s0 = st_ref[:, 0, :]
        s1 = st_ref[:, 1, :]
        gpre = w0 * s0 + w1 * s1 + w2 * gate
        hids_ref[...] = (jax.nn.silu(gpre) * up).astype(BF16)
        sst_ref[:, 0, :] = s1
        sst_ref[:, 1, :] = gate

    tm = hp_ref.shape[0]
    first = (i % tiles_per_seq) == 0
    prev = jnp.where(first, 0.0, carry_ref[...])
    row = lax.broadcasted_iota(jnp.int32, prev.shape, 0)
    for c in range(tm // FFN_ROWS):
        rows = slice(c * FFN_ROWS, (c + 1) * FFN_ROWS)
        hp = hp_ref[rows, :]
        gate = _dot(hp, wgb_ref[...])
        up = _dot(hp, wub_ref[...])
        gpre = w0 * _shift_rows(gate, prev, 2, row) + w1 * _shift_rows(gate, prev, 1, row) + w2 * gate
        hidp_ref[rows, :] = (jax.nn.silu(gpre) * up).astype(BF16)
        prev = gate[FFN_ROWS - SUBLANES:, :]
    carry_ref[...] = prev
    pst_ref[0] = prev[SUBLANES - (CONV_FFN_WIDTH - 1):, :]


def _ffn_up(hp, xs, g_all, wg_all, wu_all, wc_all, st_all, layer, n_seq, stacked):
    rows, n_s = hp.shape[0], xs.shape[0]
    depth = st_all.shape[0]
    tiles_per_seq = rows // n_seq // TM
    w_spec = pl.BlockSpec((None, D_MODEL, TN), lambda j, i: (layer, 0, j))
    in_specs = [pl.BlockSpec((TM, D_MODEL), lambda j, i: (i, 0)),
                pl.BlockSpec((n_s, D_MODEL), lambda j, i: (0, 0)),
                pl.BlockSpec((None, 1, D_MODEL), lambda j, i: (layer, 0, 0)),
                w_spec, w_spec,
                pl.BlockSpec((None, CONV_FFN_WIDTH, TN), lambda j, i: (layer, 0, j)),
                pl.BlockSpec((None, n_s, CONV_FFN_WIDTH - 1, TN), lambda j, i: (layer, 0, 0, j))]
    n_in = len(in_specs)
    stacked = () if stacked is None else (stacked,)
    in_specs = in_specs + [pl.BlockSpec(memory_space=pl.ANY)] * len(stacked)
    return pl.pallas_call(
        _skip_refs(functools.partial(_ffn_up_kernel, tiles_per_seq), n_in, len(stacked)),
        grid=(D_FF // TN, rows // TM),
        in_specs=in_specs,
        out_specs=[pl.BlockSpec((TM, TN), lambda j, i: (i, j)),
                   pl.BlockSpec((n_s, TN), lambda j, i: (0, j)),
                   pl.BlockSpec((1, CONV_FFN_WIDTH - 1, TN), lambda j, i: (i // tiles_per_seq, 0, j)),
                   pl.BlockSpec((None, n_s, CONV_FFN_WIDTH - 1, TN), lambda j, i: (layer, 0, 0, j))],
        out_shape=[jax.ShapeDtypeStruct((rows, D_FF), BF16),
                   jax.ShapeDtypeStruct((n_s, D_FF), BF16),
                   jax.ShapeDtypeStruct((n_seq, CONV_FFN_WIDTH - 1, D_FF), F32),
                   jax.ShapeDtypeStruct((depth, n_s, CONV_FFN_WIDTH - 1, D_FF), F32)],
        input_output_aliases={n_in + k: 3 for k in range(len(stacked))},
        scratch_shapes=[pltpu.VMEM((D_MODEL, TN), BF16),
                        pltpu.VMEM((D_MODEL, TN), BF16),
                        pltpu.VMEM((SUBLANES, TN), F32)],
        compiler_params=_cparams("arbitrary", "arbitrary"),
        name="ffn_up",
    )(hp, xs, g_all, wg_all, wu_all, wc_all, st_all, *stacked)


def _shift_rows(x, prev, d, row):
    r = pltpu.roll(x, d, axis=0)
    top = jnp.where(row < d, pltpu.roll(prev, d, axis=0), r[0:SUBLANES])
    return jnp.concatenate([top, r[SUBLANES:]], axis=0)


def _layernorm_silu(y, g, b):
    mu = jnp.mean(y, axis=-1, keepdims=True)
    yc = y - mu
    yn = yc * lax.rsqrt(jnp.mean(yc * yc, axis=-1, keepdims=True) + EPS)
    return jax.nn.silu(yn * g + b)


def _lru_gate_dots(xr, wga_ref, wgx_ref):
    ra, rx = [], []
    for h in range(LRU_HEADS):
        xh = xr[:, h * HEAD_DIM:(h + 1) * HEAD_DIM].astype(BF16)
        ra.append(_dot(xh, wga_ref[h].astype(BF16)))
        rx.append(_dot(xh, wgx_ref[h].astype(BF16)))
    return jnp.concatenate(ra, axis=-1), jnp.concatenate(rx, axis=-1)


def _lru_decay(xr, ra, rx, bga_ref, bgx_ref, lam_ref):
    r = jax.nn.sigmoid(ra + bga_ref[...])
    ig = jax.nn.sigmoid(rx + bgx_ref[...])
    log_a = -LRU_C * r * jax.nn.softplus(-lam_ref[...])
    a = jnp.exp(log_a)
    one_minus_a2 = -jnp.tanh(log_a) * (a * a + 1.0)
    b = jnp.sqrt(jnp.maximum(one_minus_a2, 0.0)) * (ig * xr)
    return a, b


def _lru_gates(xr, wga_ref, bga_ref, wgx_ref, bgx_ref, lam_ref):
    ra, rx = _lru_gate_dots(xr, wga_ref, wgx_ref)
    return _lru_decay(xr, ra, rx, bga_ref, bgx_ref, lam_ref)


def _rope(t, cc, ss):
    return t * cc + pltpu.roll(t, HEAD_DIM // 2, axis=1) * ss


def _retention_tables(chunk):
    log_g = jnp.log(1.0 - 2.0 ** (-5.0 - jnp.arange(RET_HEADS, dtype=F32)))
    idx = jnp.arange(chunk, dtype=F32)
    rel = idx[:, None] - idx[None, :]
    dmask = jnp.where(rel >= 0, jnp.exp(log_g[:, None, None] * jnp.maximum(rel, 0.0)), 0.0)
    inner = jnp.exp(log_g[:, None] * (idx + 1.0))
    sdec = jnp.exp(log_g[:, None] * (chunk - 1.0 - idx))
    cdec = jnp.exp(log_g * chunk)
    return dmask, inner, sdec, cdec


def _rope_tables(pos):
    half = HEAD_DIM // 2
    inv = ROPE_BASE ** (-jnp.arange(half, dtype=F32) / half)
    ang = pos.astype(F32)[:, None] * inv[None, :]
    cos, sin = jnp.cos(ang), jnp.sin(ang)
    return jnp.concatenate([cos, cos], axis=-1), jnp.concatenate([-sin, sin], axis=-1)


def _conv_a_rows(ub_ref, wa_ref, base):
    n_sub = CONV_A_ROWS // SUBLANES
    tap0 = CONV_A_HALO - (CONV_A_WIDTH - 1)
    acc = [None] * n_sub
    for s in range(SUBLANES):
        taps = [k for k in range(CONV_A_WIDTH) if (tap0 + k) % SUBLANES == s]
        wts = {k: wa_ref[k] for k in taps}
        tiles = {}
        for k in taps:
            blk = (tap0 + k) // SUBLANES
            for j in range(n_sub):
                t = blk + j
                if t not in tiles:
                    tiles[t] = ub_ref[s, base + t * SUBLANES:base + (t + 1) * SUBLANES, :]
                term = wts[k] * tiles[t]
                acc[j] = term if acc[j] is None else acc[j] + term
    return jnp.concatenate(acc, axis=0)


def _mixer_layer_kernel(n_cast, n_tiles, tiles_per_seq,
                        xp_ref, xn_ref, g1_ref, win_ref, wout_ref, g2_ref,
                        wa_ref, lng_ref, lnb_ref, wb_ref, wc_ref, bcc_ref,
                        wga_ref, bga_ref, wgx_ref, bgx_ref, lam_ref,
                        cc_ref, ss_ref, dmask_ref, inner_ref, sdec_ref, cdec_ref,
                        xo_ref, h2_ref, pca_ref, pcb_ref, pcc_ref, plru_ref, pret_ref,
                        wbin_ref, wbout_ref, h_ref, z_ref, mix_ref, mixn_ref, ub_ref, vb_ref, xb_ref, hl_ref, s_ref):
    step = pl.program_id(0)
    w = W_GROUP
    tl = xn_ref.shape[0]

    @pl.when(step == 0)
    def _():
        mix_ref[...] = jnp.zeros(mix_ref.shape, BF16)
        ub_ref[...] = jnp.zeros(ub_ref.shape, F32)
        vb_ref[...] = jnp.zeros(vb_ref.shape, F32)
        xb_ref[...] = jnp.zeros(xb_ref.shape, F32)
        hl_ref[...] = jnp.zeros(hl_ref.shape, F32)
        s_ref[...] = jnp.zeros(s_ref.shape, F32)

    @pl.when(step < n_cast)
    def _():
        r0 = pl.multiple_of(step * W_CAST_ROWS, W_CAST_ROWS)
        wbin_ref[pl.ds(r0, W_CAST_ROWS), :] = win_ref[...].astype(BF16)
        wbout_ref[pl.ds(r0, W_CAST_ROWS), :] = wout_ref[...].astype(BF16)

    @pl.when(step == n_cast - 1)
    def _():
        h_ref[...] = (_rms_scale(xn_ref[...]) * g1_ref[...]).astype(BF16)

    @pl.when(step >= n_cast)
    def _():
        m = step - n_cast
        valid = m < n_tiles
        first = (m % tiles_per_seq) == 0
        row = lax.broadcasted_iota(jnp.int32, (SUBLANES, w), 0)

        def zcol(k, hd=None):
            if hd is None:
                return z_ref[:, k * w:(k + 1) * w]
            return z_ref[:, k * w + hd * HEAD_DIM:k * w + (hd + 1) * HEAD_DIM]

        def inproj(k):
            z_ref[:, k * w:(k + 1) * w] = _dot(h_ref[...], wbin_ref[:, k * w:(k + 1) * w])

        def outproj_cols(j):
            cols = slice(j * w, (j + 1) * w)
            xc = xp_ref[:, cols] + _dot(mix_ref[...], wbout_ref[:, cols])
            xo_ref[:, cols] = xc
            return xc

        def keep(new, old_ref):
            return jnp.where(valid, new, old_ref[0])

        assert tl == RET_CHUNK
        heads = range(RET_HEADS)

        inproj(6)
        inproj(7)
        cx = zcol(6)
        c_prev = jnp.where(first, 0.0, xb_ref[...])
        xr = (wc_ref[0:1, :] * _shift_rows(cx, c_prev, 3, row)
              + wc_ref[1:2, :] * _shift_rows(cx, c_prev, 2, row)
              + wc_ref[2:3, :] * _shift_rows(cx, c_prev, 1, row)
              + wc_ref[3:4, :] * cx) + bcc_ref[...]
        xb_ref[...] = cx[tl - SUBLANES:, :]
        pcc_ref[0] = keep(cx[tl - (CONV_C_WIDTH - 1):, :], pcc_ref)
        inproj(8)
        ra, rx = _lru_gate_dots(xr, wga_ref, wgx_ref)
        inproj(9)
        scale = HEAD_DIM ** -0.5
        qb, kb, ksd, vv, s_old = [], [], [], [], []
        for hd in heads:
            qr = _rope(zcol(7, hd), cc_ref[...], ss_ref[...])
            kr = _rope(zcol(8, hd), cc_ref[...], ss_ref[...]) * scale
            qb.append(qr.astype(BF16))
            kb.append(kr.astype(BF16))
            ksd.append((kr * sdec_ref[hd]).astype(BF16))
            vv.append(zcol(9, hd).astype(BF16))
            s_old.append(jnp.where(first, 0.0, s_ref[hd]))
        x_new = [outproj_cols(0)]
        a, b = _lru_decay(xr, ra, rx, bga_ref, bgx_ref, lam_ref)
        scores = [_dot_nt(qb[hd], kb[hd]) for hd in heads]
        q_s = [_dot(qb[hd], s_old[hd].astype(BF16)) for hd in heads]
        k_v = [_dot_tn(ksd[hd], vv[hd]) for hd in heads]
        inproj(5)
        nblk = tl // SUBLANES
        a3 = a.reshape(nblk, SUBLANES, w)
        b3 = b.reshape(nblk, SUBLANES, w)
        row3 = lax.broadcasted_iota(jnp.int32, a3.shape, 1)
        for d in (1, 2, 4):
            keep_rows = row3 >= d
            a_sh = jnp.where(keep_rows, pltpu.roll(a3, d, axis=1), 1.0)
            b_sh = jnp.where(keep_rows, pltpu.roll(b3, d, axis=1), 0.0)
            b3 = a3 * b_sh + b3
            a3 = a3 * a_sh
        h_in = jnp.where(first, 0.0, hl_ref[...])
        hs = []
        for i in range(nblk):
            h_blk = a3[i] * h_in + b3[i]
            hs.append(h_blk)
            h_in = jnp.broadcast_to(h_blk[SUBLANES - 1:SUBLANES, :], (SUBLANES, w))
        hl_ref[...] = h_in
        plru_ref[0] = keep(h_in[0:1, :], plru_ref)
        x_new.append(outproj_cols(1))
        mixn_ref[:, 2 * w:3 * w] = (jnp.concatenate(hs, axis=0) * jax.nn.gelu(zcol(5))).astype(BF16)
        masked = [(scores[hd] * dmask_ref[hd]).astype(BF16) for hd in heads]
        s_v = [_dot(masked[hd], vv[hd]) for hd in heads]
        inproj(0)
        inproj(1)
        u = zcol(0) * jax.nn.sigmoid(zcol(1))
        ub_ref[0, 0:CONV_A_HALO, :] = jnp.where(first, 0.0, ub_ref[0, tl:tl + CONV_A_HALO, :])
        ub_ref[0, CONV_A_HALO:CONV_A_HALO + tl, :] = u
        n_shift_rows = tl + CONV_A_HALO - SUBLANES
        for s in range(1, SUBLANES):
            ub_ref[s, 0:n_shift_rows, :] = ub_ref[0, s:s + n_shift_rows, :]
        pca_ref[0] = keep(u[tl - (CONV_A_WIDTH - 1):, :], pca_ref)
        x_new.append(outproj_cols(2))

        def conv_a_chunk(base):
            ya = _layernorm_silu(_conv_a_rows(ub_ref, wa_ref, base), lng_ref[...], lnb_ref[...])
            mixn_ref[base:base + CONV_A_ROWS, 0:w] = ya.astype(BF16)

        assert tl == 4 * CONV_A_ROWS
        conv_a_chunk(0)
        inproj(10)
        conv_a_chunk(CONV_A_ROWS)
        x_new.append(outproj_cols(3))
        x_new = jnp.concatenate(x_new, axis=1)
        h2_ref[...] = (_rms_scale(x_new) * g2_ref[...]).astype(BF16)
        for hd in heads:
            o = s_v[hd] + q_s[hd] * inner_ref[hd]
            s_new = s_old[hd] * cdec_ref[hd] + k_v[hd]
            s_ref[hd] = s_new
            pret_ref[0, hd] = jnp.where(valid, s_new, pret_ref[0, hd])
            on = o * lax.rsqrt(jnp.mean(o * o, axis=-1, keepdims=True) + EPS)
            mixn_ref[:, 3 * w + hd * HEAD_DIM:3 * w + (hd + 1) * HEAD_DIM] = (
                jax.nn.silu(zcol(10, hd)) * on).astype(BF16)
        inproj(2)
        conv_a_chunk(2 * CONV_A_ROWS)
        inproj(3)
        conv_a_chunk(3 * CONV_A_ROWS)
        inproj(4)
        h_ref[...] = (_rms_scale(xn_ref[...]) * g1_ref[...]).astype(BF16)
        v = zcol(3) * zcol(4)
        v_prev = jnp.where(first, 0.0, vb_ref[...])
        yb = (wb_ref[0:1, :] * _shift_rows(v, v_prev, 2, row)
              + wb_ref[1:2, :] * _shift_rows(v, v_prev, 1, row)
              + wb_ref[2:3, :] * v)
        vb_ref[...] = v[tl - SUBLANES:, :]
        pcb_ref[0] = keep(v[tl - (CONV_B_WIDTH - 1):, :], pcb_ref)
        mixn_ref[:, w:2 * w] = (zcol(2) * yb).astype(BF16)
        mix_ref[...] = mixn_ref[...]


def _mixer_layer(xp, n_seq, layer, g_mix, w_in, w_out, g_ffn, p):
    rows = xp.shape[0]
    seq = rows // n_seq
    w = W_GROUP
    n_cast = D_MODEL // W_CAST_ROWS
    n_tiles = rows // TL
    tiles_per_seq = seq // TL
    dmask, inner, sdec, cdec = _retention_tables(RET_CHUNK)
    inner = jnp.broadcast_to(inner[:, :, None], (RET_HEADS, RET_CHUNK, HEAD_DIM))
    sdec = jnp.broadcast_to(sdec[:, :, None], (RET_HEADS, RET_CHUNK, HEAD_DIM))
    cdec = jnp.broadcast_to(cdec[:, None, None], (RET_HEADS, 1, HEAD_DIM))
    cc, ss = _rope_tables(jnp.arange(seq, dtype=jnp.int32))
    wa8 = jnp.broadcast_to(p["w_conv_a"][layer][:, None, :], (CONV_A_WIDTH, SUBLANES, w))

    def tile(s):
        return jnp.clip(s - n_cast, 0, n_tiles - 1)

    def prev_tile(s):
        return jnp.clip(s - n_cast - 1, 0, n_tiles - 1)

    def next_tile(s):
        return jnp.clip(s - n_cast + 1, 0, n_tiles - 1)

    def cast_chunk(s):
        return jnp.minimum(s, n_cast - 1)

    def vec_spec(n):
        return pl.BlockSpec((None, 1, n), lambda s: (layer, 0, 0))

    def full_spec(shape):
        nd = len(shape)
        return pl.BlockSpec(shape, lambda s: (0,) * nd)

    def layer_spec(shape):
        nd = len(shape)
        return pl.BlockSpec((None,) + shape, lambda s: (layer,) + (0,) * nd)

    def seq_spec(shape):
        nd = len(shape)
        return pl.BlockSpec((1,) + shape, lambda s: (tile(s) // tiles_per_seq,) + (0,) * nd)

    rope_spec = pl.BlockSpec((TL, HEAD_DIM), lambda s: (tile(s) % tiles_per_seq, 0))
    in_specs = [
        pl.BlockSpec((TL, D_MODEL), lambda s: (prev_tile(s), 0)),
        pl.BlockSpec((TL, D_MODEL), lambda s: (next_tile(s), 0)),
        vec_spec(D_MODEL),
        pl.BlockSpec((None, W_CAST_ROWS, N_PROJ * w), lambda s: (layer, cast_chunk(s), 0)),
        pl.BlockSpec((None, W_CAST_ROWS, D_MODEL), lambda s: (layer, cast_chunk(s), 0)),
        vec_spec(D_MODEL),
        full_spec((CONV_A_WIDTH, SUBLANES, w)), vec_spec(w), vec_spec(w),
        layer_spec((CONV_B_WIDTH, w)), layer_spec((CONV_C_WIDTH, w)), vec_spec(w),
        layer_spec((LRU_HEADS, HEAD_DIM, HEAD_DIM)), vec_spec(w),
        layer_spec((LRU_HEADS, HEAD_DIM, HEAD_DIM)), vec_spec(w), vec_spec(w),
        rope_spec, rope_spec,
        full_spec((RET_HEADS, RET_CHUNK, RET_CHUNK)), full_spec((RET_HEADS, RET_CHUNK, HEAD_DIM)),
        full_spec((RET_HEADS, RET_CHUNK, HEAD_DIM)), full_spec((RET_HEADS, 1, HEAD_DIM)),
    ]
    state_shapes = [(CONV_A_WIDTH - 1, w), (CONV_B_WIDTH - 1, w), (CONV_C_WIDTH - 1, w), (1, w),
                    (RET_HEADS, HEAD_DIM, HEAD_DIM)]
    out_specs = [pl.BlockSpec((TL, D_MODEL), lambda s: (prev_tile(s), 0)),
                 pl.BlockSpec((TL, D_MODEL), lambda s: (prev_tile(s), 0))] + [seq_spec(sh) for sh in state_shapes]
    out_shape = [jax.ShapeDtypeStruct((rows, D_MODEL), F32),
                 jax.ShapeDtypeStruct((rows, D_MODEL), BF16)] + [
        jax.ShapeDtypeStruct((n_seq,) + sh, F32) for sh in state_shapes]
    scratch = [
        pltpu.VMEM((D_MODEL, N_PROJ * w), BF16),
        pltpu.VMEM((D_MODEL, D_MODEL), BF16),
        pltpu.VMEM((TL, D_MODEL), BF16),
        pltpu.VMEM((TL, N_PROJ * w), F32),
        pltpu.VMEM((TL, D_MODEL), BF16),
        pltpu.VMEM((TL, D_MODEL), BF16),
        pltpu.VMEM((SUBLANES, CONV_A_HALO + TL, w), F32),
        pltpu.VMEM((SUBLANES, w), F32),
        pltpu.VMEM((SUBLANES, w), F32),
        pltpu.VMEM((SUBLANES, w), F32),
        pltpu.VMEM((RET_HEADS, HEAD_DIM, HEAD_DIM), F32),
    ]
    return pl.pallas_call(
        functools.partial(_mixer_layer_kernel, n_cast, n_tiles, tiles_per_seq),
        grid=(n_cast + n_tiles + 1,),
        in_specs=in_specs,
        out_specs=out_specs,
        out_shape=out_shape,
        scratch_shapes=scratch,
        compiler_params=_cparams("arbitrary"),
        name="mixer_layer",
    )(xp, xp, g_mix, w_in, w_out, g_ffn, wa8, p["ln_a_g"], p["ln_a_b"], p["w_conv_b"], p["w_conv_c"], p["b_conv_c"],
      p["w_gate_a"], p["b_gate_a"], p["w_gate_x"], p["b_gate_x"], p["lru_lambda"],
      cc, ss, dmask, inner, sdec, cdec)


def _mix_sample_kernel(z_ref, sca_ref, scb_ref, scc_ref, slru_ref, sret_ref,
                       wa_ref, lng_ref, lnb_ref, wb_ref, wc_ref, bcc_ref,
                       wga_ref, bga_ref, wgx_ref, bgx_ref, lam_ref,
                       cc_ref, ss_ref, gdec_ref,
                       mix_ref, oca_ref, ocb_ref, occ_ref, olru_ref, oret_ref,
                       o_ref):
    w = W_GROUP
    bt = z_ref.shape[0]

    def zs(k):
        return z_ref[:, k * w:(k + 1) * w]

    u = zs(0) * jax.nn.sigmoid(zs(1))
    y = wa_ref[CONV_A_WIDTH - 1:CONV_A_WIDTH, :] * u
    for k in range(CONV_A_WIDTH - 1):
        y = y + wa_ref[k:k + 1, :] * sca_ref[:, k, :]
    mix_ref[:, 0:w] = _layernorm_silu(y, lng_ref[...], lnb_ref[...]).astype(BF16)
    oca_ref[:, 0:CONV_A_WIDTH - 2, :] = sca_ref[:, 1:CONV_A_WIDTH - 1, :]
    oca_ref[:, CONV_A_WIDTH - 2, :] = u

    v = zs(3) * zs(4)
    yb = wb_ref[0:1, :] * scb_ref[:, 0, :] + wb_ref[1:2, :] * scb_ref[:, 1, :] + wb_ref[2:3, :] * v
    mix_ref[:, w:2 * w] = (zs(2) * yb).astype(BF16)
    ocb_ref[:, 0, :] = scb_ref[:, 1, :]
    ocb_ref[:, 1, :] = v

    cx = zs(6)
    xr = (wc_ref[0:1, :] * scc_ref[:, 0, :] + wc_ref[1:2, :] * scc_ref[:, 1, :]
          + wc_ref[2:3, :] * scc_ref[:, 2, :] + wc_ref[3:4, :] * cx) + bcc_ref[...]
    occ_ref[:, 0, :] = scc_ref[:, 1, :]
    occ_ref[:, 1, :] = scc_ref[:, 2, :]
    occ_ref[:, 2, :] = cx
    a, b = _lru_gates(xr, wga_ref, bga_ref, wgx_ref, bgx_ref, lam_ref)
    h_new = a * slru_ref[...] + b
    olru_ref[...] = h_new
    mix_ref[:, 2 * w:3 * w] = (h_new * jax.nn.gelu(zs(5))).astype(BF16)

    scale = HEAD_DIM ** -0.5
    row0 = lax.broadcasted_iota(jnp.int32, (SUBLANES, HEAD_DIM), 0) == 0
    for h in range(RET_HEADS):
        cols = slice(h * HEAD_DIM, (h + 1) * HEAD_DIM)
        qr = _rope(z_ref[:, 7 * w + h * HEAD_DIM:7 * w + (h + 1) * HEAD_DIM], cc_ref[...], ss_ref[...])
        kr = _rope(z_ref[:, 8 * w + h * HEAD_DIM:8 * w + (h + 1) * HEAD_DIM], cc_ref[...], ss_ref[...]) * scale
        vv = z_ref[:, 9 * w + h * HEAD_DIM:9 * w + (h + 1) * HEAD_DIM]
        g = gdec_ref[h]
        for bi in range(bt):
            k8 = jnp.where(row0, jnp.broadcast_to(kr[bi:bi + 1, :], (SUBLANES, HEAD_DIM)), 0.0).astype(BF16)
            v8 = jnp.broadcast_to(vv[bi:bi + 1, :], (SUBLANES, HEAD_DIM)).astype(BF16)
            q8 = jnp.broadcast_to(qr[bi:bi + 1, :], (SUBLANES, HEAD_DIM)).astype(BF16)
            s_new = sret_ref[bi, h] * g + _dot_tn(k8, v8)
            oret_ref[bi, h] = s_new
            o8 = _dot(q8, s_new.astype(BF16))
            o_ref[bi:bi + 1, cols] = o8[0:1, :]
    for h in range(RET_HEADS):
        cols = slice(h * HEAD_DIM, (h + 1) * HEAD_DIM)
        o = o_ref[:, cols]
        on = o * lax.rsqrt(jnp.mean(o * o, axis=-1, keepdims=True) + EPS)
        mix_ref[:, 3 * w + h * HEAD_DIM:3 * w + (h + 1) * HEAD_DIM] = (
            jax.nn.silu(z_ref[:, 10 * w + h * HEAD_DIM:10 * w + (h + 1) * HEAD_DIM]) * on).astype(BF16)


def _mix_sample(z, layer, p, st, stacked):
    n_s = z.shape[0]
    depth = st[0].shape[0]
    w = W_GROUP
    _, _, _, gdec = _retention_tables(1)
    gdec = jnp.broadcast_to(gdec[:, None, None], (RET_HEADS, 1, HEAD_DIM))
    cc, ss = _rope_tables(PAST_LEN + jnp.arange(1, dtype=jnp.int32))

    def vec_spec():
        return pl.BlockSpec((None, 1, w), lambda i: (layer, 0, 0))

    def layer_spec(shape):
        nd = len(shape)
        return pl.BlockSpec((None,) + shape, lambda i: (layer,) + (0,) * nd)

    def state_spec(shape):
        nd = len(shape)
        return pl.BlockSpec((None, BT) + shape, lambda i: (layer, i) + (0,) * nd)

    state_shapes = [(CONV_A_WIDTH - 1, w), (CONV_B_WIDTH - 1, w), (CONV_C_WIDTH - 1, w), (w,),
                    (RET_HEADS, HEAD_DIM, HEAD_DIM)]
    in_specs = [pl.BlockSpec((BT, N_PROJ * w), lambda i: (i, 0))] + [state_spec(s) for s in state_shapes] + [
        layer_spec((CONV_A_WIDTH, w)), vec_spec(), vec_spec(),
        layer_spec((CONV_B_WIDTH, w)), layer_spec((CONV_C_WIDTH, w)), vec_spec(),
        layer_spec((LRU_HEADS, HEAD_DIM, HEAD_DIM)), vec_spec(),
        layer_spec((LRU_HEADS, HEAD_DIM, HEAD_DIM)), vec_spec(), vec_spec(),
        pl.BlockSpec((1, HEAD_DIM), lambda i: (0, 0)), pl.BlockSpec((1, HEAD_DIM), lambda i: (0, 0)),
        pl.BlockSpec((RET_HEADS, 1, HEAD_DIM), lambda i: (0, 0, 0)),
    ]
    out_specs = [pl.BlockSpec((BT, D_MODEL), lambda i: (i, 0))] + [state_spec(s) for s in state_shapes]
    out_shape = [jax.ShapeDtypeStruct((n_s, D_MODEL), BF16)] + [
        jax.ShapeDtypeStruct((depth, n_s) + s, F32) for s in state_shapes]
    n_in = len(in_specs)
    stacked = () if stacked is None else tuple(stacked)
    in_specs = in_specs + [pl.BlockSpec(memory_space=pl.ANY)] * len(stacked)
    return pl.pallas_call(
        _skip_refs(_mix_sample_kernel, n_in, len(stacked)),
        grid=(n_s // BT,),
        in_specs=in_specs,
        input_output_aliases={n_in + k: 1 + k for k in range(len(stacked))},
        out_specs=out_specs,
        out_shape=out_shape,
        scratch_shapes=[pltpu.VMEM((BT, w), F32)],
        compiler_params=_cparams("arbitrary"),
        name="mix_sample",
    )(z, *st, p["w_conv_a"], p["ln_a_g"], p["ln_a_b"], p["w_conv_b"], p["w_conv_c"], p["b_conv_c"],
      p["w_gate_a"], p["b_gate_a"], p["w_gate_x"], p["b_gate_x"], p["lru_lambda"], cc, ss, gdec, *stacked)


def kernel(x_prompt, x_sample, state_conv_a, state_conv_b, state_conv_c, state_lru_c, state_ret_d, state_conv_ffn,
           g_mix, w_in, w_conv_a, ln_a_g, ln_a_b, w_conv_b, w_conv_c, b_conv_c, w_gate_a, b_gate_a, w_gate_x,
           b_gate_x, lru_lambda, w_out, g_ffn, w_ffn_gate, w_ffn_up, w_conv_ffn, w_ffn_down, g_final):
    n_seq, seq, _ = x_prompt.shape
    n_s = x_sample.shape[0]
    depth = w_in.shape[0]
    assert x_sample.shape[1] == 1 and seq % TL == 0 and seq % TM == 0 and n_s % BT == 0

    def rows3(v):
        return v.reshape(depth, 1, v.shape[-1])

    p = dict(w_conv_a=w_conv_a, ln_a_g=rows3(ln_a_g), ln_a_b=rows3(ln_a_b), w_conv_b=w_conv_b,
             w_conv_c=w_conv_c, b_conv_c=rows3(b_conv_c), w_gate_a=w_gate_a, b_gate_a=rows3(b_gate_a),
             w_gate_x=w_gate_x, b_gate_x=rows3(b_gate_x), lru_lambda=rows3(lru_lambda))
    g_mix, g_ffn, g_final = rows3(g_mix), rows3(g_ffn), g_final.reshape(1, 1, D_MODEL)
    xp = x_prompt.reshape(n_seq * seq, D_MODEL)
    xs = x_sample.reshape(n_s, D_MODEL)
    sample_states = (state_conv_a, state_conv_b, state_conv_c, state_lru_c, state_ret_d)
    new_p = [[] for _ in range(6)]
    mix_states, scf = None, None
    for layer in range(depth):
        zs = _sample_inproj(xs, g_mix, w_in, layer)
        mixs, *mix_states = _mix_sample(zs, layer, p, sample_states, mix_states)
        xs = _sample_outproj(mixs, w_out, layer, xs)
        xp, h2p, pca, pcb, pcc, plru, pret = _mixer_layer(xp, n_seq, layer, g_mix, w_in, w_out, g_ffn, p)
        hidp, hids, pcf, scf = _ffn_up(h2p, xs, g_ffn, w_ffn_gate, w_ffn_up, w_conv_ffn, state_conv_ffn,
                                       layer, n_seq, scf)
        xp, xs = _resproj(hidp, hids, w_ffn_down, layer, xp, xs, TM_DOWN, "ffn_down")
        for lst, s in zip(new_p, (pca, pcb, pcc, plru.reshape(n_seq, W_GROUP), pret, pcf)):
            lst.append(s)
    y_prompt = _rmsnorm(xp, g_final, 0, F32, TM_NORM).reshape(n_seq, seq, D_MODEL)
    y_sample = _rmsnorm(xs, g_final, 0, F32, n_s).reshape(n_s, 1, D_MODEL)
    return (y_prompt, y_sample, *(jnp.stack(l) for l in new_p), *mix_states, scf)
```

```python
import functools

import jax
import jax.numpy as jnp
from jax import lax
from jax.experimental import pallas as pl
from jax.experimental.pallas import tpu as pltpu

F32 = jnp.float32
BF16 = jnp.bfloat16

D_MODEL = 2048
N_GROUPS = 4
W_GROUP = D_MODEL // N_GROUPS
N_PROJ = 11
CONV_A_WIDTH = 31
CONV_B_WIDTH = 3
CONV_C_WIDTH = 4
CONV_FFN_WIDTH = 3
LRU_HEADS = 4
LRU_C = 8.0
RET_HEADS = 4
HEAD_DIM = W_GROUP // RET_HEADS
RET_CHUNK = 128
ROPE_BASE = 10000.0
D_FF = 5632
EPS = 1e-6
PAST_LEN = 16384

SUBLANES = 8
VMEM_LIMIT_BYTES = 56 * 1024 * 1024

TM = 2048
TN = 512
TM_DOWN = 512
TM_OUT = 512
FFN_ROWS = 512
MM_ROWS = 512
TM_NORM = 512
TL = 256
W_CAST_ROWS = 128
CONV_A_ROWS = 32
CONV_A_HALO = 32
BT = 16


def _cparams(*semantics):
    return pltpu.CompilerParams(dimension_semantics=semantics, vmem_limit_bytes=VMEM_LIMIT_BYTES)


def _rms_scale(x):
    return x * lax.rsqrt(jnp.mean(x * x, axis=-1, keepdims=True) + EPS)


def _row_chunks(n_rows):
    chunk = min(MM_ROWS, n_rows // 2)
    return [slice(r, r + chunk) for r in range(0, n_rows, chunk)]


def _skip_refs(body, n_in, n_skip):
    def wrapped(*refs):
        return body(*refs[:n_in], *refs[n_in + n_skip:])
    return wrapped


def _dot(a, b):
    return jnp.dot(a, b, preferred_element_type=F32)


def _dot_nt(a, b):
    return lax.dot_general(a, b, (((1,), (1,)), ((), ())), preferred_element_type=F32)


def _dot_tn(a, b):
    return lax.dot_general(a, b, (((0,), (0,)), ((), ())), preferred_element_type=F32)


def _rmsnorm_kernel(x_ref, g_ref, o_ref):
    o_ref[...] = (_rms_scale(x_ref[...]) * g_ref[...]).astype(o_ref.dtype)


def _rmsnorm(x, g_all, layer, out_dtype, tm):
    rows = x.shape[0]
    return pl.pallas_call(
        _rmsnorm_kernel,
        grid=(rows // tm,),
        in_specs=[pl.BlockSpec((tm, D_MODEL), lambda i: (i, 0)),
                  pl.BlockSpec((None, 1, D_MODEL), lambda i: (layer, 0, 0))],
        out_specs=pl.BlockSpec((tm, D_MODEL), lambda i: (i, 0)),
        out_shape=jax.ShapeDtypeStruct((rows, D_MODEL), out_dtype),
        compiler_params=_cparams("arbitrary"),
        name="rmsnorm",
    )(x, g_all)


def _sample_inproj_kernel(xs_ref, g_ref, w_ref, zs_ref):
    hs = (_rms_scale(xs_ref[...]) * g_ref[...]).astype(BF16)
    zs_ref[...] = _dot(hs, w_ref[...].astype(BF16))


def _sample_inproj(xs, g_all, w_all, layer):
    n_s = xs.shape[0]
    n_out = w_all.shape[2]
    return pl.pallas_call(
        _sample_inproj_kernel,
        grid=(n_out // TN,),
        in_specs=[pl.BlockSpec((n_s, D_MODEL), lambda j: (0, 0)),
                  pl.BlockSpec((None, 1, D_MODEL), lambda j: (layer, 0, 0)),
                  pl.BlockSpec((None, D_MODEL, TN), lambda j: (layer, 0, j))],
        out_specs=pl.BlockSpec((n_s, TN), lambda j: (0, j)),
        out_shape=jax.ShapeDtypeStruct((n_s, n_out), F32),
        compiler_params=_cparams("arbitrary"),
        name="sample_inproj",
    )(xs, g_all, w_all)


def _resproj_kernel(ap_ref, as_ref, w_ref, xp_ref, xs_ref, op_ref, os_ref, wb_ref):
    @pl.when(pl.program_id(1) == 0)
    def _():
        wb_ref[...] = w_ref[...].astype(BF16)
        os_ref[...] = xs_ref[...] + _dot(as_ref[...], wb_ref[...])

    for rows in _row_chunks(ap_ref.shape[0]):
        op_ref[rows, :] = xp_ref[rows, :] + _dot(ap_ref[rows, :], wb_ref[...])


def _resproj(ap, a_s, w_all, layer, xp, xs, tm, name):
    rows, n_s = ap.shape[0], a_s.shape[0]
    k_dim = ap.shape[1]
    return pl.pallas_call(
        _resproj_kernel,
        grid=(D_MODEL // TN, rows // tm),
        in_specs=[pl.BlockSpec((tm, k_dim), lambda j, i: (i, 0)),
                  pl.BlockSpec((n_s, k_dim), lambda j, i: (0, 0)),
                  pl.BlockSpec((None, k_dim, TN), lambda j, i: (layer, 0, j)),
                  pl.BlockSpec((tm, TN), lambda j, i: (i, j)),
                  pl.BlockSpec((n_s, TN), lambda j, i: (0, j))],
        out_specs=[pl.BlockSpec((tm, TN), lambda j, i: (i, j)),
                   pl.BlockSpec((n_s, TN), lambda j, i: (0, j))],
        out_shape=[jax.ShapeDtypeStruct((rows, D_MODEL), F32),
                   jax.ShapeDtypeStruct((n_s, D_MODEL), F32)],
        scratch_shapes=[pltpu.VMEM((k_dim, TN), BF16)],
        compiler_params=_cparams("arbitrary", "arbitrary"),
        name=name,
    )(ap, a_s, w_all, xp, xs)


def _ffn_up_kernel(tiles_per_seq, hp_ref, xs_ref, g_ref, wg_ref, wu_ref, wc_ref, st_ref,
                   hidp_ref, hids_ref, pst_ref, sst_ref, wgb_ref, wub_ref, carry_ref):
    i = pl.program_id(1)
    w0 = wc_ref[0:1, :]
    w1 = wc_ref[1:2, :]
    w2 = wc_ref[2:3, :]

    @pl.when(i == 0)
    def _():
        wgb_ref[...] = wg_ref[...].astype(BF16)
        wub_ref[...] = wu_ref[...].astype(BF16)
        hs = (_rms_scale(xs_ref[...]) * g_ref[...]).astype(BF16)
        gate = _dot(hs, wgb_ref[...])
        up = _dot(hs, wub_ref[...])
        s0 = st_ref[:, 0, :]
        s1 = st_ref[:, 1, :]
        gpre = w0 * s0 + w1 * s1 + w2 * gate
        hids_ref[...] = (jax.nn.silu(gpre) * up).astype(BF16)
        sst_ref[:, 0, :] = s1
        sst_ref[:, 1, :] = gate

    tm = hp_ref.shape[0]
    first = (i % tiles_per_seq) == 0
    prev = jnp.where(first, 0.0, carry_ref[...])
    row = lax.broadcasted_iota(jnp.int32, prev.shape, 0)
    for c in range(tm // FFN_ROWS):
        rows = slice(c * FFN_ROWS, (c + 1) * FFN_ROWS)
        hp = hp_ref[rows, :]
        gate = _dot(hp, wgb_ref[...])
        up = _dot(hp, wub_ref[...])
        gpre = w0 * _shift_rows(gate, prev, 2, row) + w1 * _shift_rows(gate, prev, 1, row) + w2 * gate
        hidp_ref[rows, :] = (jax.nn.silu(gpre) * up).astype(BF16)
        prev = gate[FFN_ROWS - SUBLANES:, :]
    carry_ref[...] = prev
    pst_ref[0] = prev[SUBLANES - (CONV_FFN_WIDTH - 1):, :]


def _ffn_up(hp, xs, g_all, wg_all, wu_all, wc_all, st_all, layer, n_seq, stacked):
    rows, n_s = hp.shape[0], xs.shape[0]
    depth = st_all.shape[0]
    tiles_per_seq = rows // n_seq // TM
    w_spec = pl.BlockSpec((None, D_MODEL, TN), lambda j, i: (layer, 0, j))
    in_specs = [pl.BlockSpec((TM, D_MODEL), lambda j, i: (i, 0)),
                pl.BlockSpec((n_s, D_MODEL), lambda j, i: (0, 0)),
                pl.BlockSpec((None, 1, D_MODEL), lambda j, i: (layer, 0, 0)),
                w_spec, w_spec,
                pl.BlockSpec((None, CONV_FFN_WIDTH, TN), lambda j, i: (layer, 0, j)),
                pl.BlockSpec((None, n_s, CONV_FFN_WIDTH - 1, TN), lambda j, i: (layer, 0, 0, j))]
    n_in = len(in_specs)
    stacked = () if stacked is None else (stacked,)
    in_specs = in_specs + [pl.BlockSpec(memory_space=pl.ANY)] * len(stacked)
    return pl.pallas_call(
        _skip_refs(functools.partial(_ffn_up_kernel, tiles_per_seq), n_in, len(stacked)),
        grid=(D_FF // TN, rows // TM),
        in_specs=in_specs,
        out_specs=[pl.BlockSpec((TM, TN), lambda j, i: (i, j)),
                   pl.BlockSpec((n_s, TN), lambda j, i: (0, j)),
                   pl.BlockSpec((1, CONV_FFN_WIDTH - 1, TN), lambda j, i: (i // tiles_per_seq, 0, j)),
                   pl.BlockSpec((None, n_s, CONV_FFN_WIDTH - 1, TN), lambda j, i: (layer, 0, 0, j))],
        out_shape=[jax.ShapeDtypeStruct((rows, D_FF), BF16),
                   jax.ShapeDtypeStruct((n_s, D_FF), BF16),
                   jax.ShapeDtypeStruct((n_seq, CONV_FFN_WIDTH - 1, D_FF), F32),
                   jax.ShapeDtypeStruct((depth, n_s, CONV_FFN_WIDTH - 1, D_FF), F32)],
        input_output_aliases={n_in + k: 3 for k in range(len(stacked))},
        scratch_shapes=[pltpu.VMEM((D_MODEL, TN), BF16),
                        pltpu.VMEM((D_MODEL, TN), BF16),
                        pltpu.VMEM((SUBLANES, TN), F32)],
        compiler_params=_cparams("arbitrary", "arbitrary"),
        name="ffn_up",
    )(hp, xs, g_all, wg_all, wu_all, wc_all, st_all, *stacked)


def _shift_rows(x, prev, d, row):
    r = pltpu.roll(x, d, axis=0)
    top = jnp.where(row < d, pltpu.roll(prev, d, axis=0), r[0:SUBLANES])
    return jnp.concatenate([top, r[SUBLANES:]], axis=0)


def _layernorm_silu(y, g, b):
    mu = jnp.mean(y, axis=-1, keepdims=True)
    yc = y - mu
    yn = yc * lax.rsqrt(jnp.mean(yc * yc, axis=-1, keepdims=True) + EPS)
    return jax.nn.silu(yn * g + b)


def _lru_gate_dots(xr, wga_ref, wgx_ref):
    ra, rx = [], []
    for h in range(LRU_HEADS):
        xh = xr[:, h * HEAD_DIM:(h + 1) * HEAD_DIM].astype(BF16)
        ra.append(_dot(xh, wga_ref[h].astype(BF16)))
        rx.append(_dot(xh, wgx_ref[h].astype(BF16)))
    return jnp.concatenate(ra, axis=-1), jnp.concatenate(rx, axis=-1)


def _lru_decay(xr, ra, rx, bga_ref, bgx_ref, lam_ref):
    r = jax.nn.sigmoid(ra + bga_ref[...])
    ig = jax.nn.sigmoid(rx + bgx_ref[...])
    log_a = -LRU_C * r * jax.nn.softplus(-lam_ref[...])
    a = jnp.exp(log_a)
    one_minus_a2 = -jnp.tanh(log_a) * (a * a + 1.0)
    b = jnp.sqrt(jnp.maximum(one_minus_a2, 0.0)) * (ig * xr)
    return a, b


def _lru_gates(xr, wga_ref, bga_ref, wgx_ref, bgx_ref, lam_ref):
    ra, rx = _lru_gate_dots(xr, wga_ref, wgx_ref)
    return _lru_decay(xr, ra, rx, bga_ref, bgx_ref, lam_ref)


def _rope(t, cc, ss):
    return t * cc + pltpu.roll(t, HEAD_DIM // 2, axis=1) * ss


def _retention_tables(chunk):
    log_g = jnp.log(1.0 - 2.0 ** (-5.0 - jnp.arange(RET_HEADS, dtype=F32)))
    idx = jnp.arange(chunk, dtype=F32)
    rel = idx[:, None] - idx[None, :]
    dmask = jnp.where(rel >= 0, jnp.exp(log_g[:, None, None] * jnp.maximum(rel, 0.0)), 0.0)
    inner = jnp.exp(log_g[:, None] * (idx + 1.0))
    sdec = jnp.exp(log_g[:, None] * (chunk - 1.0 - idx))
    cdec = jnp.exp(log_g * chunk)
    return dmask, inner, sdec, cdec


def _rope_tables(pos):
    half = HEAD_DIM // 2
    inv = ROPE_BASE ** (-jnp.arange(half, dtype=F32) / half)
    ang = pos.astype(F32)[:, None] * inv[None, :]
    cos, sin = jnp.cos(ang), jnp.sin(ang)
    return jnp.concatenate([cos, cos], axis=-1), jnp.concatenate([-sin, sin], axis=-1)


def _conv_a_rows(ub_ref, wa_ref, base, row):
    n_sub = CONV_A_ROWS // SUBLANES
    n_tiles = n_sub + CONV_A_HALO // SUBLANES
    tap0 = CONV_A_HALO - (CONV_A_WIDTH - 1)
    tiles = [ub_ref[base + i * SUBLANES:base + (i + 1) * SUBLANES, :] for i in range(n_tiles)]
    acc = [None] * n_sub
    for s in range(SUBLANES):
        if s == 0:
            win = tiles
        else:
            rolled = [pltpu.roll(t, SUBLANES - s, axis=0) for t in tiles]
            win = [jnp.where(row < SUBLANES - s, rolled[i], rolled[i + 1]) for i in range(n_tiles - 1)]
        for k in range(CONV_A_WIDTH):
            blk, sk = divmod(tap0 + k, SUBLANES)
            if sk != s:
                continue
            wk = wa_ref[k]
            for j in range(n_sub):
                term = wk * win[blk + j]
                acc[j] = term if acc[j] is None else acc[j] + term
    return jnp.concatenate(acc, axis=0)


def _mixer_kernel(n_cast, tiles_per_seq,
                  xn_ref, g1_ref, win_ref,
                  wa_ref, lng_ref, lnb_ref, wb_ref, wc_ref, bcc_ref,
                  wga_ref, bga_ref, wgx_ref, bgx_ref, lam_ref,
                  cc_ref, ss_ref, dmask_ref, inner_ref, sdec_ref, cdec_ref,
                  mix_ref, pca_ref, pcb_ref, pcc_ref, plru_ref, pret_ref,
                  wbin_ref, h_ref, z_ref, ub_ref, vb_ref, xb_ref, hl_ref, s_ref):
    step = pl.program_id(0)
    w = W_GROUP
    tl = xn_ref.shape[0]

    @pl.when(step == 0)
    def _():
        ub_ref[...] = jnp.zeros(ub_ref.shape, F32)
        vb_ref[...] = jnp.zeros(vb_ref.shape, F32)
        xb_ref[...] = jnp.zeros(xb_ref.shape, F32)
        hl_ref[...] = jnp.zeros(hl_ref.shape, F32)
        s_ref[...] = jnp.zeros(s_ref.shape, F32)

    @pl.when(step < n_cast)
    def _():
        r0 = pl.multiple_of(step * W_CAST_ROWS, W_CAST_ROWS)
        wbin_ref[pl.ds(r0, W_CAST_ROWS), :] = win_ref[...].astype(BF16)

    @pl.when(step == n_cast - 1)
    def _():
        h_ref[...] = (_rms_scale(xn_ref[...]) * g1_ref[...]).astype(BF16)

    @pl.when(step >= n_cast)
    def _():
        first = ((step - n_cast) % tiles_per_seq) == 0
        row = lax.broadcasted_iota(jnp.int32, (SUBLANES, w), 0)
        heads = range(RET_HEADS)
        chunks = range(tl // RET_CHUNK)

        def zcol(k, hd=None, c=None):
            if hd is None:
                return z_ref[:, k * w:(k + 1) * w]
            return z_ref[c * RET_CHUNK:(c + 1) * RET_CHUNK, k * w + hd * HEAD_DIM:k * w + (hd + 1) * HEAD_DIM]

        def inproj(k):
            z_ref[:, k * w:(k + 1) * w] = _dot(h_ref[...], wbin_ref[:, k * w:(k + 1) * w])

        def conv_a_chunks(lo, hi):
            for ci in range(lo, hi):
                base = ci * CONV_A_ROWS
                ya = _layernorm_silu(_conv_a_rows(ub_ref, wa_ref, base, row), lng_ref[...], lnb_ref[...])
                mix_ref[base:base + CONV_A_ROWS, 0:w] = ya.astype(BF16)

        assert tl // CONV_A_ROWS == 8

        inproj(0)
        inproj(1)
        u = zcol(0) * jax.nn.sigmoid(zcol(1))
        ub_ref[0:CONV_A_HALO, :] = jnp.where(first, 0.0, ub_ref[tl:tl + CONV_A_HALO, :])
        ub_ref[CONV_A_HALO:CONV_A_HALO + tl, :] = u
        pca_ref[0] = u[tl - (CONV_A_WIDTH - 1):, :]
        inproj(6)
        conv_a_chunks(0, 1)
        inproj(7)
        cx = zcol(6)
        c_prev = jnp.where(first, 0.0, xb_ref[...])
        xr = (wc_ref[0:1, :] * _shift_rows(cx, c_prev, 3, row)
              + wc_ref[1:2, :] * _shift_rows(cx, c_prev, 2, row)
              + wc_ref[2:3, :] * _shift_rows(cx, c_prev, 1, row)
              + wc_ref[3:4, :] * cx) + bcc_ref[...]
        xb_ref[...] = cx[tl - SUBLANES:, :]
        pcc_ref[0] = cx[tl - (CONV_C_WIDTH - 1):, :]
        inproj(8)
        ra, rx = _lru_gate_dots(xr, wga_ref, wgx_ref)
        conv_a_chunks(1, 2)
        inproj(9)
        scale = HEAD_DIM ** -0.5
        qb, kb, ksd, vv = {}, {}, {}, {}
        for c in chunks:
            cc = cc_ref[c * RET_CHUNK:(c + 1) * RET_CHUNK, :]
            ss = ss_ref[c * RET_CHUNK:(c + 1) * RET_CHUNK, :]
            for hd in heads:
                qr = _rope(zcol(7, hd, c), cc, ss)
                kr = _rope(zcol(8, hd, c), cc, ss) * scale
                qb[c, hd] = qr.astype(BF16)
                kb[c, hd] = kr.astype(BF16)
                ksd[c, hd] = (kr * sdec_ref[hd]).astype(BF16)
                vv[c, hd] = zcol(9, hd, c).astype(BF16)
        a, b = _lru_decay(xr, ra, rx, bga_ref, bgx_ref, lam_ref)
        inproj(5)
        nblk = tl // SUBLANES
        a3 = a.reshape(nblk, SUBLANES, w)
        b3 = b.reshape(nblk, SUBLANES, w)
        row3 = lax.broadcasted_iota(jnp.int32, a3.shape, 1)
        for d in (1, 2, 4):
            keep_rows = row3 >= d
            a_sh = jnp.where(keep_rows, pltpu.roll(a3, d, axis=1), 1.0)
            b_sh = jnp.where(keep_rows, pltpu.roll(b3, d, axis=1), 0.0)
            b3 = a3 * b_sh + b3
            a3 = a3 * a_sh
        h_in = jnp.where(first, 0.0, hl_ref[...])
        hs = []
        for i in range(nblk):
            h_blk = a3[i] * h_in + b3[i]
            hs.append(h_blk)
            h_in = jnp.broadcast_to(h_blk[SUBLANES - 1:SUBLANES, :], (SUBLANES, w))
        hl_ref[...] = h_in
        plru_ref[0] = h_in[0:1, :]
        mix_ref[:, 2 * w:3 * w] = (jnp.concatenate(hs, axis=0) * jax.nn.gelu(zcol(5))).astype(BF16)
        conv_a_chunks(2, 3)
        s_cur = [jnp.where(first, 0.0, s_ref[hd]) for hd in heads]
        for c in chunks:
            scores = [_dot_nt(qb[c, hd], kb[c, hd]) for hd in heads]
            q_s = [_dot(qb[c, hd], s_cur[hd].astype(BF16)) for hd in heads]
            k_v = [_dot_tn(ksd[c, hd], vv[c, hd]) for hd in heads]
            if c == 0:
                inproj(10)
            conv_a_chunks(3 + 2 * c, 4 + 2 * c)
            masked = [(scores[hd] * dmask_ref[hd]).astype(BF16) for hd in heads]
            s_v = [_dot(masked[hd], vv[c, hd]) for hd in heads]
            inproj(2 + c)
            conv_a_chunks(4 + 2 * c, 5 + 2 * c)
            for hd in heads:
                o = s_v[hd] + q_s[hd] * inner_ref[hd]
                s_cur[hd] = s_cur[hd] * cdec_ref[hd] + k_v[hd]
                on = o * lax.rsqrt(jnp.mean(o * o, axis=-1, keepdims=True) + EPS)
                mix_ref[c * RET_CHUNK:(c + 1) * RET_CHUNK, 3 * w + hd * HEAD_DIM:3 * w + (hd + 1) * HEAD_DIM] = (
                    jax.nn.silu(zcol(10, hd, c)) * on).astype(BF16)
        for hd in heads:
            s_ref[hd] = s_cur[hd]
            pret_ref[0, hd] = s_cur[hd]
        inproj(4)
        conv_a_chunks(7, 8)
        h_ref[...] = (_rms_scale(xn_ref[...]) * g1_ref[...]).astype(BF16)
        v = zcol(3) * zcol(4)
        v_prev = jnp.where(first, 0.0, vb_ref[...])
        yb = (wb_ref[0:1, :] * _shift_rows(v, v_prev, 2, row)
              + wb_ref[1:2, :] * _shift_rows(v, v_prev, 1, row)
              + wb_ref[2:3, :] * v)
        vb_ref[...] = v[tl - SUBLANES:, :]
        pcb_ref[0] = v[tl - (CONV_B_WIDTH - 1):, :]
        mix_ref[:, w:2 * w] = (zcol(2) * yb).astype(BF16)


def _mixer(xp, n_seq, layer, g_mix, w_in, p):
    rows = xp.shape[0]
    seq = rows // n_seq
    w = W_GROUP
    n_cast = D_MODEL // W_CAST_ROWS
    n_tiles = rows // TL
    tiles_per_seq = seq // TL
    assert TL // RET_CHUNK == 2
    dmask, inner, sdec, cdec = _retention_tables(RET_CHUNK)
    inner = jnp.broadcast_to(inner[:, :, None], (RET_HEADS, RET_CHUNK, HEAD_DIM))
    sdec = jnp.broadcast_to(sdec[:, :, None], (RET_HEADS, RET_CHUNK, HEAD_DIM))
    cdec = jnp.broadcast_to(cdec[:, None, None], (RET_HEADS, 1, HEAD_DIM))
    cc, ss = _rope_tables(jnp.arange(seq, dtype=jnp.int32))
    wa8 = jnp.broadcast_to(p["w_conv_a"][layer][:, None, :], (CONV_A_WIDTH, SUBLANES, w))

    def tile(s):
        return jnp.clip(s - n_cast, 0, n_tiles - 1)

    def next_tile(s):
        return jnp.clip(s - n_cast + 1, 0, n_tiles - 1)

    def cast_chunk(s):
        return jnp.minimum(s, n_cast - 1)

    def vec_spec(n):
        return pl.BlockSpec((None, 1, n), lambda s: (layer, 0, 0))

    def full_spec(shape):
        nd = len(shape)
        return pl.BlockSpec(shape, lambda s: (0,) * nd)

    def layer_spec(shape):
        nd = len(shape)
        return pl.BlockSpec((None,) + shape, lambda s: (layer,) + (0,) * nd)

    def seq_spec(shape):
        nd = len(shape)
        return pl.BlockSpec((1,) + shape, lambda s: (tile(s) // tiles_per_seq,) + (0,) * nd)

    rope_spec = pl.BlockSpec((TL, HEAD_DIM), lambda s: (tile(s) % tiles_per_seq, 0))
    in_specs = [
        pl.BlockSpec((TL, D_MODEL), lambda s: (next_tile(s), 0)),
        vec_spec(D_MODEL),
        pl.BlockSpec((None, W_CAST_ROWS, N_PROJ * w), lambda s: (layer, cast_chunk(s), 0)),
        full_spec((CONV_A_WIDTH, SUBLANES, w)), vec_spec(w), vec_spec(w),
        layer_spec((CONV_B_WIDTH, w)), layer_spec((CONV_C_WIDTH, w)), vec_spec(w),
        layer_spec((LRU_HEADS, HEAD_DIM, HEAD_DIM)), vec_spec(w),
        layer_spec((LRU_HEADS, HEAD_DIM, HEAD_DIM)), vec_spec(w), vec_spec(w),
        rope_spec, rope_spec,
        full_spec((RET_HEADS, RET_CHUNK, RET_CHUNK)), full_spec((RET_HEADS, RET_CHUNK, HEAD_DIM)),
        full_spec((RET_HEADS, RET_CHUNK, HEAD_DIM)), full_spec((RET_HEADS, 1, HEAD_DIM)),
    ]
    state_shapes = [(CONV_A_WIDTH - 1, w), (CONV_B_WIDTH - 1, w), (CONV_C_WIDTH - 1, w), (1, w),
                    (RET_HEADS, HEAD_DIM, HEAD_DIM)]
    out_specs = [pl.BlockSpec((TL, D_MODEL), lambda s: (tile(s), 0))] + [seq_spec(sh) for sh in state_shapes]
    out_shape = [jax.ShapeDtypeStruct((rows, D_MODEL), BF16)] + [
        jax.ShapeDtypeStruct((n_seq,) + sh, F32) for sh in state_shapes]
    scratch = [
        pltpu.VMEM((D_MODEL, N_PROJ * w), BF16),
        pltpu.VMEM((TL, D_MODEL), BF16),
        pltpu.VMEM((TL, N_PROJ * w), F32),
        pltpu.VMEM((CONV_A_HALO + TL, w), F32),
        pltpu.VMEM((SUBLANES, w), F32),
        pltpu.VMEM((SUBLANES, w), F32),
        pltpu.VMEM((SUBLANES, w), F32),
        pltpu.VMEM((RET_HEADS, HEAD_DIM, HEAD_DIM), F32),
    ]
    return pl.pallas_call(
        functools.partial(_mixer_kernel, n_cast, tiles_per_seq),
        grid=(n_cast + n_tiles,),
        in_specs=in_specs,
        out_specs=out_specs,
        out_shape=out_shape,
        scratch_shapes=scratch,
        compiler_params=_cparams("arbitrary"),
        name="mixer",
    )(xp, g_mix, w_in, wa8, p["ln_a_g"], p["ln_a_b"], p["w_conv_b"], p["w_conv_c"], p["b_conv_c"],
      p["w_gate_a"], p["b_gate_a"], p["w_gate_x"], p["b_gate_x"], p["lru_lambda"],
      cc, ss, dmask, inner, sdec, cdec)


def _outproj_kernel(n_cast, mixp_ref, xp_ref, mixs_ref, xs_ref, w_ref, g_ref,
                    xo_ref, h2_ref, xso_ref, wb_ref):
    step = pl.program_id(0)

    @pl.when(step < n_cast)
    def _():
        r0 = pl.multiple_of(step * W_CAST_ROWS, W_CAST_ROWS)
        wb_ref[pl.ds(r0, W_CAST_ROWS), :] = w_ref[...].astype(BF16)

    @pl.when(step == n_cast)
    def _():
        xso_ref[...] = xs_ref[...] + _dot(mixs_ref[...], wb_ref[...])

    @pl.when(step >= n_cast)
    def _():
        for rows in _row_chunks(mixp_ref.shape[0]):
            x_new = xp_ref[rows, :] + _dot(mixp_ref[rows, :], wb_ref[...])
            xo_ref[rows, :] = x_new
            h2_ref[rows, :] = (_rms_scale(x_new) * g_ref[...]).astype(BF16)


def _outproj(mixp, xp, mixs, xs, w_all, g_all, layer):
    rows, n_s = xp.shape[0], xs.shape[0]
    n_cast = D_MODEL // W_CAST_ROWS
    n_tiles = rows // TM_OUT

    def tile(s):
        return jnp.maximum(s - n_cast, 0)

    row_spec = pl.BlockSpec((TM_OUT, D_MODEL), lambda s: (tile(s), 0))
    s_spec = pl.BlockSpec((n_s, D_MODEL), lambda s: (0, 0))
    return pl.pallas_call(
        functools.partial(_outproj_kernel, n_cast),
        grid=(n_cast + n_tiles,),
        in_specs=[row_spec, row_spec, s_spec, s_spec,
                  pl.BlockSpec((None, W_CAST_ROWS, D_MODEL), lambda s: (layer, jnp.minimum(s, n_cast - 1), 0)),
                  pl.BlockSpec((None, 1, D_MODEL), lambda s: (layer, 0, 0))],
        out_specs=[row_spec, row_spec, s_spec],
        out_shape=[jax.ShapeDtypeStruct((rows, D_MODEL), F32),
                   jax.ShapeDtypeStruct((rows, D_MODEL), BF16),
                   jax.ShapeDtypeStruct((n_s, D_MODEL), F32)],
        scratch_shapes=[pltpu.VMEM((D_MODEL, D_MODEL), BF16)],
        compiler_params=_cparams("arbitrary"),
        name="outproj",
    )(mixp, xp, mixs, xs, w_all, g_all)


def _mix_sample_kernel(z_ref, sca_ref, scb_ref, scc_ref, slru_ref, sret_ref,
                       wa_ref, lng_ref, lnb_ref, wb_ref, wc_ref, bcc_ref,
                       wga_ref, bga_ref, wgx_ref, bgx_ref, lam_ref,
                       cc_ref, ss_ref, gdec_ref,
                       mix_ref, oca_ref, ocb_ref, occ_ref, olru_ref, oret_ref,
                       o_ref):
    w = W_GROUP
    bt = z_ref.shape[0]

    def zs(k):
        return z_ref[:, k * w:(k + 1) * w]

    u = zs(0) * jax.nn.sigmoid(zs(1))
    y = wa_ref[CONV_A_WIDTH - 1:CONV_A_WIDTH, :] * u
    for k in range(CONV_A_WIDTH - 1):
        y = y + wa_ref[k:k + 1, :] * sca_ref[:, k, :]
    mix_ref[:, 0:w] = _layernorm_silu(y, lng_ref[...], lnb_ref[...]).astype(BF16)
    oca_ref[:, 0:CONV_A_WIDTH - 2, :] = sca_ref[:, 1:CONV_A_WIDTH - 1, :]
    oca_ref[:, CONV_A_WIDTH - 2, :] = u

    v = zs(3) * zs(4)
    yb = wb_ref[0:1, :] * scb_ref[:, 0, :] + wb_ref[1:2, :] * scb_ref[:, 1, :] + wb_ref[2:3, :] * v
    mix_ref[:, w:2 * w] = (zs(2) * yb).astype(BF16)
    ocb_ref[:, 0, :] = scb_ref[:, 1, :]
    ocb_ref[:, 1, :] = v

    cx = zs(6)
    xr = (wc_ref[0:1, :] * scc_ref[:, 0, :] + wc_ref[1:2, :] * scc_ref[:, 1, :]
          + wc_ref[2:3, :] * scc_ref[:, 2, :] + wc_ref[3:4, :] * cx) + bcc_ref[...]
    occ_ref[:, 0, :] = scc_ref[:, 1, :]
    occ_ref[:, 1, :] = scc_ref[:, 2, :]
    occ_ref[:, 2, :] = cx
    a, b = _lru_gates(xr, wga_ref, bga_ref, wgx_ref, bgx_ref, lam_ref)
    h_new = a * slru_ref[...] + b
    olru_ref[...] = h_new
    mix_ref[:, 2 * w:3 * w] = (h_new * jax.nn.gelu(zs(5))).astype(BF16)

    scale = HEAD_DIM ** -0.5
    row0 = lax.broadcasted_iota(jnp.int32, (SUBLANES, HEAD_DIM), 0) == 0
    for h in range(RET_HEADS):
        cols = slice(h * HEAD_DIM, (h + 1) * HEAD_DIM)
        qr = _rope(z_ref[:, 7 * w + h * HEAD_DIM:7 * w + (h + 1) * HEAD_DIM], cc_ref[...], ss_ref[...])
        kr = _rope(z_ref[:, 8 * w + h * HEAD_DIM:8 * w + (h + 1) * HEAD_DIM], cc_ref[...], ss_ref[...]) * scale
        vv = z_ref[:, 9 * w + h * HEAD_DIM:9 * w + (h + 1) * HEAD_DIM]
        g = gdec_ref[h]
        for bi in range(bt):
            k8 = jnp.where(row0, jnp.broadcast_to(kr[bi:bi + 1, :], (SUBLANES, HEAD_DIM)), 0.0).astype(BF16)
            v8 = jnp.broadcast_to(vv[bi:bi + 1, :], (SUBLANES, HEAD_DIM)).astype(BF16)
            q8 = jnp.broadcast_to(qr[bi:bi + 1, :], (SUBLANES, HEAD_DIM)).astype(BF16)
            s_new = sret_ref[bi, h] * g + _dot_tn(k8, v8)
            oret_ref[bi, h] = s_new
            o8 = _dot(q8, s_new.astype(BF16))
            o_ref[bi:bi + 1, cols] = o8[0:1, :]
    for h in range(RET_HEADS):
        cols = slice(h * HEAD_DIM, (h + 1) * HEAD_DIM)
        o = o_ref[:, cols]
        on = o * lax.rsqrt(jnp.mean(o * o, axis=-1, keepdims=True) + EPS)
        mix_ref[:, 3 * w + h * HEAD_DIM:3 * w + (h + 1) * HEAD_DIM] = (
            jax.nn.silu(z_ref[:, 10 * w + h * HEAD_DIM:10 * w + (h + 1) * HEAD_DIM]) * on).astype(BF16)


def _mix_sample(z, layer, p, st, stacked):
    n_s = z.shape[0]
    depth = st[0].shape[0]
    w = W_GROUP
    _, _, _, gdec = _retention_tables(1)
    gdec = jnp.broadcast_to(gdec[:, None, None], (RET_HEADS, 1, HEAD_DIM))
    cc, ss = _rope_tables(PAST_LEN + jnp.arange(1, dtype=jnp.int32))

    def vec_spec():
        return pl.BlockSpec((None, 1, w), lambda i: (layer, 0, 0))

    def layer_spec(shape):
        nd = len(shape)
        return pl.BlockSpec((None,) + shape, lambda i: (layer,) + (0,) * nd)

    def state_spec(shape):
        nd = len(shape)
        return pl.BlockSpec((None, BT) + shape, lambda i: (layer, i) + (0,) * nd)

    state_shapes = [(CONV_A_WIDTH - 1, w), (CONV_B_WIDTH - 1, w), (CONV_C_WIDTH - 1, w), (w,),
                    (RET_HEADS, HEAD_DIM, HEAD_DIM)]
    in_specs = [pl.BlockSpec((BT, N_PROJ * w), lambda i: (i, 0))] + [state_spec(s) for s in state_shapes] + [
        layer_spec((CONV_A_WIDTH, w)), vec_spec(), vec_spec(),
        layer_spec((CONV_B_WIDTH, w)), layer_spec((CONV_C_WIDTH, w)), vec_spec(),
        layer_spec((LRU_HEADS, HEAD_DIM, HEAD_DIM)), vec_spec(),
        layer_spec((LRU_HEADS, HEAD_DIM, HEAD_DIM)), vec_spec(), vec_spec(),
        pl.BlockSpec((1, HEAD_DIM), lambda i: (0, 0)), pl.BlockSpec((1, HEAD_DIM), lambda i: (0, 0)),
        pl.BlockSpec((RET_HEADS, 1, HEAD_DIM), lambda i: (0, 0, 0)),
    ]
    out_specs = [pl.BlockSpec((BT, D_MODEL), lambda i: (i, 0))] + [state_spec(s) for s in state_shapes]
    out_shape = [jax.ShapeDtypeStruct((n_s, D_MODEL), BF16)] + [
        jax.ShapeDtypeStruct((depth, n_s) + s, F32) for s in state_shapes]
    n_in = len(in_specs)
    stacked = () if stacked is None else tuple(stacked)
    in_specs = in_specs + [pl.BlockSpec(memory_space=pl.ANY)] * len(stacked)
    return pl.pallas_call(
        _skip_refs(_mix_sample_kernel, n_in, len(stacked)),
        grid=(n_s // BT,),
        in_specs=in_specs,
        input_output_aliases={n_in + k: 1 + k for k in range(len(stacked))},
        out_specs=out_specs,
        out_shape=out_shape,
        scratch_shapes=[pltpu.VMEM((BT, w), F32)],
        compiler_params=_cparams("arbitrary"),
        name="mix_sample",
    )(z, *st, p["w_conv_a"], p["ln_a_g"], p["ln_a_b"], p["w_conv_b"], p["w_conv_c"], p["b_conv_c"],
      p["w_gate_a"], p["b_gate_a"], p["w_gate_x"], p["b_gate_x"], p["lru_lambda"], cc, ss, gdec, *stacked)


def kernel(x_prompt, x_sample, state_conv_a, state_conv_b, state_conv_c, state_lru_c, state_ret_d, state_conv_ffn,
           g_mix, w_in, w_conv_a, ln_a_g, ln_a_b, w_conv_b, w_conv_c, b_conv_c, w_gate_a, b_gate_a, w_gate_x,
           b_gate_x, lru_lambda, w_out, g_ffn, w_ffn_gate, w_ffn_up, w_conv_ffn, w_ffn_down, g_final):
    n_seq, seq, _ = x_prompt.shape
    n_s = x_sample.shape[0]
    depth = w_in.shape[0]
    assert x_sample.shape[1] == 1 and seq % TL == 0 and seq % TM == 0 and n_s % BT == 0

    def rows3(v):
        return v.reshape(depth, 1, v.shape[-1])

    p = dict(w_conv_a=w_conv_a, ln_a_g=rows3(ln_a_g), ln_a_b=rows3(ln_a_b), w_conv_b=w_conv_b,
             w_conv_c=w_conv_c, b_conv_c=rows3(b_conv_c), w_gate_a=w_gate_a, b_gate_a=rows3(b_gate_a),
             w_gate_x=w_gate_x, b_gate_x=rows3(b_gate_x), lru_lambda=rows3(lru_lambda))
    g_mix, g_ffn, g_final = rows3(g_mix), rows3(g_ffn), g_final.reshape(1, 1, D_MODEL)
    xp = x_prompt.reshape(n_seq * seq, D_MODEL)
    xs = x_sample.reshape(n_s, D_MODEL)
    sample_states = (state_conv_a, state_conv_b, state_conv_c, state_lru_c, state_ret_d)
    new_p = [[] for _ in range(6)]
    mix_states, scf = None, None
    for layer in range(depth):
        zs = _sample_inproj(xs, g_mix, w_in, layer)
        mixs, *mix_states = _mix_sample(zs, layer, p, sample_states, mix_states)
        mixp, pca, pcb, pcc, plru, pret = _mixer(xp, n_seq, layer, g_mix, w_in, p)
        xp, h2p, xs = _outproj(mixp, xp, mixs, xs, w_out, g_ffn, layer)
        hidp, hids, pcf, scf = _ffn_up(h2p, xs, g_ffn, w_ffn_gate, w_ffn_up, w_conv_ffn, state_conv_ffn,
                                       layer, n_seq, scf)
        xp, xs = _resproj(hidp, hids, w_ffn_down, layer, xp, xs, TM_DOWN, "ffn_down")
        for lst, s in zip(new_p, (pca, pcb, pcc, plru.reshape(n_seq, W_GROUP), pret, pcf)):
            lst.append(s)
    y_prompt = _rmsnorm(xp, g_final, 0, F32, TM_NORM).reshape(n_seq, seq, D_MODEL)
    y_sample = _rmsnorm(xs, g_final, 0, F32, n_s).reshape(n_s, 1, D_MODEL)
    return (y_prompt, y_sample, *(jnp.stack(l) for l in new_p), *mix_states, scf)
```

```python
import functools

import jax
import jax.numpy as jnp
from jax import lax
from jax.experimental import pallas as pl
from jax.experimental.pallas import tpu as pltpu

F32 = jnp.float32
BF16 = jnp.bfloat16

D_MODEL = 2048
N_GROUPS = 4
W_GROUP = D_MODEL // N_GROUPS
N_PROJ = 11
CONV_A_WIDTH = 31
CONV_B_WIDTH = 3
CONV_C_WIDTH = 4
CONV_FFN_WIDTH = 3
LRU_HEADS = 4
LRU_C = 8.0
RET_HEADS = 4
HEAD_DIM = W_GROUP // RET_HEADS
RET_CHUNK = 128
ROPE_BASE = 10000.0
D_FF = 5632
EPS = 1e-6
PAST_LEN = 16384

SUBLANES = 8
VMEM_LIMIT_BYTES = 56 * 1024 * 1024

TM = 2048
TN = 512
TM_DOWN = 512
TM_OUT = 512
FFN_ROWS = 512
MM_ROWS = 512
TM_NORM = 512
TL = 256
W_CAST_ROWS = 128
CONV_A_ROWS = 32
CONV_A_HALO = 32
BT = 16


def _cparams(*semantics):
    return pltpu.CompilerParams(dimension_semantics=semantics, vmem_limit_bytes=VMEM_LIMIT_BYTES)


def _rms_scale(x):
    return x * lax.rsqrt(jnp.mean(x * x, axis=-1, keepdims=True) + EPS)


def _row_chunks(n_rows):
    chunk = min(MM_ROWS, n_rows // 2)
    return [slice(r, r + chunk) for r in range(0, n_rows, chunk)]


def _skip_refs(body, n_in, n_skip):
    def wrapped(*refs):
        return body(*refs[:n_in], *refs[n_in + n_skip:])
    return wrapped


def _dot(a, b):
    return jnp.dot(a, b, preferred_element_type=F32)


def _dot_nt(a, b):
    return lax.dot_general(a, b, (((1,), (1,)), ((), ())), preferred_element_type=F32)


def _dot_tn(a, b):
    return lax.dot_general(a, b, (((0,), (0,)), ((), ())), preferred_element_type=F32)


def _rmsnorm_kernel(x_ref, g_ref, o_ref):
    o_ref[...] = (_rms_scale(x_ref[...]) * g_ref[...]).astype(o_ref.dtype)


def _rmsnorm(x, g_all, layer, out_dtype, tm):
    rows = x.shape[0]
    return pl.pallas_call(
        _rmsnorm_kernel,
        grid=(rows // tm,),
        in_specs=[pl.BlockSpec((tm, D_MODEL), lambda i: (i, 0)),
                  pl.BlockSpec((None, 1, D_MODEL), lambda i: (layer, 0, 0))],
        out_specs=pl.BlockSpec((tm, D_MODEL), lambda i: (i, 0)),
        out_shape=jax.ShapeDtypeStruct((rows, D_MODEL), out_dtype),
        compiler_params=_cparams("arbitrary"),
        name="rmsnorm",
    )(x, g_all)


def _resproj_kernel(ap_ref, as_ref, w_ref, xp_ref, xs_ref, op_ref, os_ref, wb_ref):
    @pl.when(pl.program_id(1) == 0)
    def _():
        wb_ref[...] = w_ref[...].astype(BF16)
        os_ref[...] = xs_ref[...] + _dot(as_ref[...], wb_ref[...])

    for rows in _row_chunks(ap_ref.shape[0]):
        op_ref[rows, :] = xp_ref[rows, :] + _dot(ap_ref[rows, :], wb_ref[...])


def _resproj(ap, a_s, w_all, layer, xp, xs, tm, name):
    rows, n_s = ap.shape[0], a_s.shape[0]
    k_dim = ap.shape[1]
    return pl.pallas_call(
        _resproj_kernel,
        grid=(D_MODEL // TN, rows // tm),
        in_specs=[pl.BlockSpec((tm, k_dim), lambda j, i: (i, 0)),
                  pl.BlockSpec((n_s, k_dim), lambda j, i: (0, 0)),
                  pl.BlockSpec((None, k_dim, TN), lambda j, i: (layer, 0, j)),
                  pl.BlockSpec((tm, TN), lambda j, i: (i, j)),
                  pl.BlockSpec((n_s, TN), lambda j, i: (0, j))],
        out_specs=[pl.BlockSpec((tm, TN), lambda j, i: (i, j)),
                   pl.BlockSpec((n_s, TN), lambda j, i: (0, j))],
        out_shape=[jax.ShapeDtypeStruct((rows, D_MODEL), F32),
                   jax.ShapeDtypeStruct((n_s, D_MODEL), F32)],
        scratch_shapes=[pltpu.VMEM((k_dim, TN), BF16)],
        compiler_params=_cparams("arbitrary", "arbitrary"),
        name=name,
    )(ap, a_s, w_all, xp, xs)


def _ffn_up_kernel(tiles_per_seq, hp_ref, xs_ref, g_ref, wg_ref, wu_ref, wc_ref, st_ref,
                   hidp_ref, hids_ref, pst_ref, sst_ref, wgb_ref, wub_ref, carry_ref):
    i = pl.program_id(1)
    w0 = wc_ref[0:1, :]
    w1 = wc_ref[1:2, :]
    w2 = wc_ref[2:3, :]

    @pl.when(i == 0)
    def _():
        wgb_ref[...] = wg_ref[...].astype(BF16)
        wub_ref[...] = wu_ref[...].astype(BF16)
        hs = (_rms_scale(xs_ref[...]) * g_ref[...]).astype(BF16)
        gate = _dot(hs, wgb_ref[...])
        up = _dot(hs, wub_ref[...])
        s0 = st_ref[:, 0, :]
        s1 = st_ref[:, 1, :]
        gpre = w0 * s0 + w1 * s1 + w2 * gate
        hids_ref[...] = (jax.nn.silu(gpre) * up).astype(BF16)
        sst_ref[:, 0, :] = s1
        sst_ref[:, 1, :] = gate

    tm = hp_ref.shape[0]
    first = (i % tiles_per_seq) == 0
    prev = jnp.where(first, 0.0, carry_ref[...])
    row = lax.broadcasted_iota(jnp.int32, prev.shape, 0)
    for c in range(tm // FFN_ROWS):
        rows = slice(c * FFN_ROWS, (c + 1) * FFN_ROWS)
        hp = hp_ref[rows, :]
        gate = _dot(hp, wgb_ref[...])
        up = _dot(hp, wub_ref[...])
        gpre = w0 * _shift_rows(gate, prev, 2, row) + w1 * _shift_rows(gate, prev, 1, row) + w2 * gate
        hidp_ref[rows, :] = (jax.nn.silu(gpre) * up).astype(BF16)
        prev = gate[FFN_ROWS - SUBLANES:, :]
    carry_ref[...] = prev
    pst_ref[0] = prev[SUBLANES - (CONV_FFN_WIDTH - 1):, :]


def _ffn_up(hp, xs, g_all, wg_all, wu_all, wc_all, st_all, layer, n_seq, stacked):
    rows, n_s = hp.shape[0], xs.shape[0]
    depth = st_all.shape[0]
    tiles_per_seq = rows // n_seq // TM
    w_spec = pl.BlockSpec((None, D_MODEL, TN), lambda j, i: (layer, 0, j))
    in_specs = [pl.BlockSpec((TM, D_MODEL), lambda j, i: (i, 0)),
                pl.BlockSpec((n_s, D_MODEL), lambda j, i: (0, 0)),
                pl.BlockSpec((None, 1, D_MODEL), lambda j, i: (layer, 0, 0)),
                w_spec, w_spec,
                pl.BlockSpec((None, CONV_FFN_WIDTH, TN), lambda j, i: (layer, 0, j)),
                pl.BlockSpec((None, n_s, CONV_FFN_WIDTH - 1, TN), lambda j, i: (layer, 0, 0, j))]
    n_in = len(in_specs)
    stacked = () if stacked is None else (stacked,)
    in_specs = in_specs + [pl.BlockSpec(memory_space=pl.ANY)] * len(stacked)
    return pl.pallas_call(
        _skip_refs(functools.partial(_ffn_up_kernel, tiles_per_seq), n_in, len(stacked)),
        grid=(D_FF // TN, rows // TM),
        in_specs=in_specs,
        out_specs=[pl.BlockSpec((TM, TN), lambda j, i: (i, j)),
                   pl.BlockSpec((n_s, TN), lambda j, i: (0, j)),
                   pl.BlockSpec((1, CONV_FFN_WIDTH - 1, TN), lambda j, i: (i // tiles_per_seq, 0, j)),
                   pl.BlockSpec((None, n_s, CONV_FFN_WIDTH - 1, TN), lambda j, i: (layer, 0, 0, j))],
        out_shape=[jax.ShapeDtypeStruct((rows, D_FF), BF16),
                   jax.ShapeDtypeStruct((n_s, D_FF), BF16),
                   jax.ShapeDtypeStruct((n_seq, CONV_FFN_WIDTH - 1, D_FF), F32),
                   jax.ShapeDtypeStruct((depth, n_s, CONV_FFN_WIDTH - 1, D_FF), F32)],
        input_output_aliases={n_in + k: 3 for k in range(len(stacked))},
        scratch_shapes=[pltpu.VMEM((D_MODEL, TN), BF16),
                        pltpu.VMEM((D_MODEL, TN), BF16),
                        pltpu.VMEM((SUBLANES, TN), F32)],
        compiler_params=_cparams("arbitrary", "arbitrary"),
        name="ffn_up",
    )(hp, xs, g_all, wg_all, wu_all, wc_all, st_all, *stacked)


def _shift_rows(x, prev, d, row):
    r = pltpu.roll(x, d, axis=0)
    top = jnp.where(row < d, pltpu.roll(prev, d, axis=0), r[0:SUBLANES])
    return jnp.concatenate([top, r[SUBLANES:]], axis=0)


def _layernorm_silu(y, g, b):
    mu = jnp.mean(y, axis=-1, keepdims=True)
    yc = y - mu
    yn = yc * lax.rsqrt(jnp.mean(yc * yc, axis=-1, keepdims=True) + EPS)
    return jax.nn.silu(yn * g + b)


def _lru_gate_dots(xr, wga_ref, wgx_ref):
    ra, rx = [], []
    for h in range(LRU_HEADS):
        xh = xr[:, h * HEAD_DIM:(h + 1) * HEAD_DIM].astype(BF16)
        ra.append(_dot(xh, wga_ref[h].astype(BF16)))
        rx.append(_dot(xh, wgx_ref[h].astype(BF16)))
    return jnp.concatenate(ra, axis=-1), jnp.concatenate(rx, axis=-1)


def _lru_decay(xr, ra, rx, bga_ref, bgx_ref, lam_ref):
    r = jax.nn.sigmoid(ra + bga_ref[...])
    ig = jax.nn.sigmoid(rx + bgx_ref[...])
    log_a = -LRU_C * r * jax.nn.softplus(-lam_ref[...])
    a = jnp.exp(log_a)
    one_minus_a2 = -jnp.tanh(log_a) * (a * a + 1.0)
    b = jnp.sqrt(jnp.maximum(one_minus_a2, 0.0)) * (ig * xr)
    return a, b


def _lru_gates(xr, wga_ref, bga_ref, wgx_ref, bgx_ref, lam_ref):
    ra, rx = _lru_gate_dots(xr, wga_ref, wgx_ref)
    return _lru_decay(xr, ra, rx, bga_ref, bgx_ref, lam_ref)


def _rope(t, cc, ss):
    return t * cc + pltpu.roll(t, HEAD_DIM // 2, axis=1) * ss


def _retention_tables(chunk):
    log_g = jnp.log(1.0 - 2.0 ** (-5.0 - jnp.arange(RET_HEADS, dtype=F32)))
    idx = jnp.arange(chunk, dtype=F32)
    rel = idx[:, None] - idx[None, :]
    dmask = jnp.where(rel >= 0, jnp.exp(log_g[:, None, None] * jnp.maximum(rel, 0.0)), 0.0)
    inner = jnp.exp(log_g[:, None] * (idx + 1.0))
    sdec = jnp.exp(log_g[:, None] * (chunk - 1.0 - idx))
    cdec = jnp.exp(log_g * chunk)
    return dmask, inner, sdec, cdec


def _rope_tables(pos):
    half = HEAD_DIM // 2
    inv = ROPE_BASE ** (-jnp.arange(half, dtype=F32) / half)
    ang = pos.astype(F32)[:, None] * inv[None, :]
    cos, sin = jnp.cos(ang), jnp.sin(ang)
    return jnp.concatenate([cos, cos], axis=-1), jnp.concatenate([-sin, sin], axis=-1)


def _conv_a_rows(ub_ref, wa_ref, base, row):
    n_sub = CONV_A_ROWS // SUBLANES
    n_tiles = n_sub + CONV_A_HALO // SUBLANES
    tap0 = CONV_A_HALO - (CONV_A_WIDTH - 1)
    tiles = [ub_ref[base + i * SUBLANES:base + (i + 1) * SUBLANES, :] for i in range(n_tiles)]
    acc = [None] * n_sub
    for s in range(SUBLANES):
        if s == 0:
            win = tiles
        else:
            rolled = [pltpu.roll(t, SUBLANES - s, axis=0) for t in tiles]
            win = [jnp.where(row < SUBLANES - s, rolled[i], rolled[i + 1]) for i in range(n_tiles - 1)]
        for k in range(CONV_A_WIDTH):
            blk, sk = divmod(tap0 + k, SUBLANES)
            if sk != s:
                continue
            wk = wa_ref[k]
            for j in range(n_sub):
                term = wk * win[blk + j]
                acc[j] = term if acc[j] is None else acc[j] + term
    return jnp.concatenate(acc, axis=0)


def _mixer_kernel(n_cast, tiles_per_seq,
                  xn_ref, xs_ref, g1_ref, win_ref,
                  wa_ref, lng_ref, lnb_ref, wb_ref, wc_ref, bcc_ref,
                  wga_ref, bga_ref, wgx_ref, bgx_ref, lam_ref,
                  cc_ref, ss_ref, dmask_ref, inner_ref, sdec_ref, cdec_ref,
                  mix_ref, zs_ref, pca_ref, pcb_ref, pcc_ref, plru_ref, pret_ref,
                  wbin_ref, h_ref, z_ref, ub_ref, vb_ref, xb_ref, hl_ref, s_ref):
    step = pl.program_id(0)
    w = W_GROUP
    tl = xn_ref.shape[0]

    @pl.when(step == 0)
    def _():
        ub_ref[...] = jnp.zeros(ub_ref.shape, F32)
        vb_ref[...] = jnp.zeros(vb_ref.shape, F32)
        xb_ref[...] = jnp.zeros(xb_ref.shape, F32)
        hl_ref[...] = jnp.zeros(hl_ref.shape, F32)
        s_ref[...] = jnp.zeros(s_ref.shape, F32)

    @pl.when(step < n_cast)
    def _():
        r0 = pl.multiple_of(step * W_CAST_ROWS, W_CAST_ROWS)
        wbin_ref[pl.ds(r0, W_CAST_ROWS), :] = win_ref[...].astype(BF16)

    @pl.when(step == n_cast - 1)
    def _():
        h_ref[...] = (_rms_scale(xn_ref[...]) * g1_ref[...]).astype(BF16)
        hs = (_rms_scale(xs_ref[...]) * g1_ref[...]).astype(BF16)
        for k in range(N_PROJ):
            zs_ref[:, k * w:(k + 1) * w] = _dot(hs, wbin_ref[:, k * w:(k + 1) * w])

    @pl.when(step >= n_cast)
    def _():
        first = ((step - n_cast) % tiles_per_seq) == 0
        row = lax.broadcasted_iota(jnp.int32, (SUBLANES, w), 0)
        heads = range(RET_HEADS)
        chunks = range(tl // RET_CHUNK)

        def zcol(k, hd=None, c=None):
            if hd is None:
                return z_ref[:, k * w:(k + 1) * w]
            return z_ref[c * RET_CHUNK:(c + 1) * RET_CHUNK, k * w + hd * HEAD_DIM:k * w + (hd + 1) * HEAD_DIM]

        def inproj(k):
            z_ref[:, k * w:(k + 1) * w] = _dot(h_ref[...], wbin_ref[:, k * w:(k + 1) * w])

        def conv_a_chunks(lo, hi):
            for ci in range(lo, hi):
                base = ci * CONV_A_ROWS
                ya = _layernorm_silu(_conv_a_rows(ub_ref, wa_ref, base, row), lng_ref[...], lnb_ref[...])
                mix_ref[base:base + CONV_A_ROWS, 0:w] = ya.astype(BF16)

        assert tl // CONV_A_ROWS == 8

        inproj(0)
        inproj(1)
        u = zcol(0) * jax.nn.sigmoid(zcol(1))
        ub_ref[0:CONV_A_HALO, :] = jnp.where(first, 0.0, ub_ref[tl:tl + CONV_A_HALO, :])
        ub_ref[CONV_A_HALO:CONV_A_HALO + tl, :] = u
        pca_ref[0] = u[tl - (CONV_A_WIDTH - 1):, :]
        inproj(7)
        conv_a_chunks(0, 1)
        inproj(8)
        conv_a_chunks(1, 2)
        inproj(6)
        scale = HEAD_DIM ** -0.5
        qb, kb, ksd = {}, {}, {}
        for c in chunks:
            cc = cc_ref[c * RET_CHUNK:(c + 1) * RET_CHUNK, :]
            ss = ss_ref[c * RET_CHUNK:(c + 1) * RET_CHUNK, :]
            for hd in heads:
                qr = _rope(zcol(7, hd, c), cc, ss)
                kr = _rope(zcol(8, hd, c), cc, ss) * scale
                qb[c, hd] = qr.astype(BF16)
                kb[c, hd] = kr.astype(BF16)
                ksd[c, hd] = (kr * sdec_ref[hd]).astype(BF16)
        inproj(9)
        cx = zcol(6)
        c_prev = jnp.where(first, 0.0, xb_ref[...])
        xr = (wc_ref[0:1, :] * _shift_rows(cx, c_prev, 3, row)
              + wc_ref[1:2, :] * _shift_rows(cx, c_prev, 2, row)
              + wc_ref[2:3, :] * _shift_rows(cx, c_prev, 1, row)
              + wc_ref[3:4, :] * cx) + bcc_ref[...]
        xb_ref[...] = cx[tl - SUBLANES:, :]
        pcc_ref[0] = cx[tl - (CONV_C_WIDTH - 1):, :]
        conv_a_chunks(2, 3)
        inproj(5)
        vv = {(c, hd): zcol(9, hd, c).astype(BF16) for c in chunks for hd in heads}
        conv_a_chunks(3, 4)
        inproj(10)
        ra, rx = _lru_gate_dots(xr, wga_ref, wgx_ref)
        conv_a_chunks(4, 5)
        inproj(2)
        a, b = _lru_decay(xr, ra, rx, bga_ref, bgx_ref, lam_ref)
        nblk = tl // SUBLANES
        a3 = a.reshape(nblk, SUBLANES, w)
        b3 = b.reshape(nblk, SUBLANES, w)
        row3 = lax.broadcasted_iota(jnp.int32, a3.shape, 1)
        for d in (1, 2, 4):
            keep_rows = row3 >= d
            a_sh = jnp.where(keep_rows, pltpu.roll(a3, d, axis=1), 1.0)
            b_sh = jnp.where(keep_rows, pltpu.roll(b3, d, axis=1), 0.0)
            b3 = a3 * b_sh + b3
            a3 = a3 * a_sh
        h_in = jnp.where(first, 0.0, hl_ref[...])
        hs = []
        for i in range(nblk):
            h_blk = a3[i] * h_in + b3[i]
            hs.append(h_blk)
            h_in = jnp.broadcast_to(h_blk[SUBLANES - 1:SUBLANES, :], (SUBLANES, w))
        hl_ref[...] = h_in
        plru_ref[0] = h_in[0:1, :]
        mix_ref[:, 2 * w:3 * w] = (jnp.concatenate(hs, axis=0) * jax.nn.gelu(zcol(5))).astype(BF16)
        s_cur = [jnp.where(first, 0.0, s_ref[hd]) for hd in heads]
        for c in chunks:
            scores = [_dot_nt(qb[c, hd], kb[c, hd]) for hd in heads]
            q_s = [_dot(qb[c, hd], s_cur[hd].astype(BF16)) for hd in heads]
            k_v = [_dot_tn(ksd[c, hd], vv[c, hd]) for hd in heads]
            inproj(3 + c)
            conv_a_chunks(5 + c, 6 + c)
            masked = [(scores[hd] * dmask_ref[hd]).astype(BF16) for hd in heads]
            s_v = [_dot(masked[hd], vv[c, hd]) for hd in heads]
            for hd in heads:
                o = s_v[hd] + q_s[hd] * inner_ref[hd]
                s_cur[hd] = s_cur[hd] * cdec_ref[hd] + k_v[hd]
                on = o * lax.rsqrt(jnp.mean(o * o, axis=-1, keepdims=True) + EPS)
                mix_ref[c * RET_CHUNK:(c + 1) * RET_CHUNK, 3 * w + hd * HEAD_DIM:3 * w + (hd + 1) * HEAD_DIM] = (
                    jax.nn.silu(zcol(10, hd, c)) * on).astype(BF16)
        for hd in heads:
            s_ref[hd] = s_cur[hd]
            pret_ref[0, hd] = s_cur[hd]
        conv_a_chunks(7, 8)
        h_ref[...] = (_rms_scale(xn_ref[...]) * g1_ref[...]).astype(BF16)
        v = zcol(3) * zcol(4)
        v_prev = jnp.where(first, 0.0, vb_ref[...])
        yb = (wb_ref[0:1, :] * _shift_rows(v, v_prev, 2, row)
              + wb_ref[1:2, :] * _shift_rows(v, v_prev, 1, row)
              + wb_ref[2:3, :] * v)
        vb_ref[...] = v[tl - SUBLANES:, :]
        pcb_ref[0] = v[tl - (CONV_B_WIDTH - 1):, :]
        mix_ref[:, w:2 * w] = (zcol(2) * yb).astype(BF16)


def _mixer(xp, xs, n_seq, layer, g_mix, w_in, p):
    rows, n_s = xp.shape[0], xs.shape[0]
    seq = rows // n_seq
    w = W_GROUP
    n_cast = D_MODEL // W_CAST_ROWS
    n_tiles = rows // TL
    tiles_per_seq = seq // TL
    assert TL // RET_CHUNK == 2
    dmask, inner, sdec, cdec = _retention_tables(RET_CHUNK)
    inner = jnp.broadcast_to(inner[:, :, None], (RET_HEADS, RET_CHUNK, HEAD_DIM))
    sdec = jnp.broadcast_to(sdec[:, :, None], (RET_HEADS, RET_CHUNK, HEAD_DIM))
    cdec = jnp.broadcast_to(cdec[:, None, None], (RET_HEADS, 1, HEAD_DIM))
    cc, ss = _rope_tables(jnp.arange(seq, dtype=jnp.int32))
    wa8 = jnp.broadcast_to(p["w_conv_a"][layer][:, None, :], (CONV_A_WIDTH, SUBLANES, w))

    def tile(s):
        return jnp.clip(s - n_cast, 0, n_tiles - 1)

    def next_tile(s):
        return jnp.clip(s - n_cast + 1, 0, n_tiles - 1)

    def cast_chunk(s):
        return jnp.minimum(s, n_cast - 1)

    def vec_spec(n):
        return pl.BlockSpec((None, 1, n), lambda s: (layer, 0, 0))

    def full_spec(shape):
        nd = len(shape)
        return pl.BlockSpec(shape, lambda s: (0,) * nd)

    def layer_spec(shape):
        nd = len(shape)
        return pl.BlockSpec((None,) + shape, lambda s: (layer,) + (0,) * nd)

    def seq_spec(shape):
        nd = len(shape)
        return pl.BlockSpec((1,) + shape, lambda s: (tile(s) // tiles_per_seq,) + (0,) * nd)

    rope_spec = pl.BlockSpec((TL, HEAD_DIM), lambda s: (tile(s) % tiles_per_seq, 0))
    in_specs = [
        pl.BlockSpec((TL, D_MODEL), lambda s: (next_tile(s), 0)),
        pl.BlockSpec((n_s, D_MODEL), lambda s: (0, 0)),
        vec_spec(D_MODEL),
        pl.BlockSpec((None, W_CAST_ROWS, N_PROJ * w), lambda s: (layer, cast_chunk(s), 0)),
        full_spec((CONV_A_WIDTH, SUBLANES, w)), vec_spec(w), vec_spec(w),
        layer_spec((CONV_B_WIDTH, w)), layer_spec((CONV_C_WIDTH, w)), vec_spec(w),
        layer_spec((LRU_HEADS, HEAD_DIM, HEAD_DIM)), vec_spec(w),
        layer_spec((LRU_HEADS, HEAD_DIM, HEAD_DIM)), vec_spec(w), vec_spec(w),
        rope_spec, rope_spec,
        full_spec((RET_HEADS, RET_CHUNK, RET_CHUNK)), full_spec((RET_HEADS, RET_CHUNK, HEAD_DIM)),
        full_spec((RET_HEADS, RET_CHUNK, HEAD_DIM)), full_spec((RET_HEADS, 1, HEAD_DIM)),
    ]
    state_shapes = [(CONV_A_WIDTH - 1, w), (CONV_B_WIDTH - 1, w), (CONV_C_WIDTH - 1, w), (1, w),
                    (RET_HEADS, HEAD_DIM, HEAD_DIM)]
    out_specs = [pl.BlockSpec((TL, D_MODEL), lambda s: (tile(s), 0)),
                 pl.BlockSpec((n_s, N_PROJ * w), lambda s: (0, 0))] + [seq_spec(sh) for sh in state_shapes]
    out_shape = [jax.ShapeDtypeStruct((rows, D_MODEL), BF16),
                 jax.ShapeDtypeStruct((n_s, N_PROJ * w), F32)] + [
        jax.ShapeDtypeStruct((n_seq,) + sh, F32) for sh in state_shapes]
    scratch = [
        pltpu.VMEM((D_MODEL, N_PROJ * w), BF16),
        pltpu.VMEM((TL, D_MODEL), BF16),
        pltpu.VMEM((TL, N_PROJ * w), F32),
        pltpu.VMEM((CONV_A_HALO + TL, w), F32),
        pltpu.VMEM((SUBLANES, w), F32),
        pltpu.VMEM((SUBLANES, w), F32),
        pltpu.VMEM((SUBLANES, w), F32),
        pltpu.VMEM((RET_HEADS, HEAD_DIM, HEAD_DIM), F32),
    ]
    return pl.pallas_call(
        functools.partial(_mixer_kernel, n_cast, tiles_per_seq),
        grid=(n_cast + n_tiles,),
        in_specs=in_specs,
        out_specs=out_specs,
        out_shape=out_shape,
        scratch_shapes=scratch,
        compiler_params=_cparams("arbitrary"),
        name="mixer",
    )(xp, xs, g_mix, w_in, wa8, p["ln_a_g"], p["ln_a_b"], p["w_conv_b"], p["w_conv_c"], p["b_conv_c"],
      p["w_gate_a"], p["b_gate_a"], p["w_gate_x"], p["b_gate_x"], p["lru_lambda"],
      cc, ss, dmask, inner, sdec, cdec)


def _outproj_kernel(n_cast, mixp_ref, xp_ref, mixs_ref, xs_ref, w_ref, g_ref,
                    xo_ref, h2_ref, xso_ref, wb_ref):
    step = pl.program_id(0)

    @pl.when(step < n_cast)
    def _():
        r0 = pl.multiple_of(step * W_CAST_ROWS, W_CAST_ROWS)
        wb_ref[pl.ds(r0, W_CAST_ROWS), :] = w_ref[...].astype(BF16)

    @pl.when(step == n_cast)
    def _():
        xso_ref[...] = xs_ref[...] + _dot(mixs_ref[...], wb_ref[...])

    @pl.when(step >= n_cast)
    def _():
        for rows in _row_chunks(mixp_ref.shape[0]):
            x_new = xp_ref[rows, :] + _dot(mixp_ref[rows, :], wb_ref[...])
            xo_ref[rows, :] = x_new
            h2_ref[rows, :] = (_rms_scale(x_new) * g_ref[...]).astype(BF16)


def _outproj(mixp, xp, mixs, xs, w_all, g_all, layer):
    rows, n_s = xp.shape[0], xs.shape[0]
    n_cast = D_MODEL // W_CAST_ROWS
    n_tiles = rows // TM_OUT

    def tile(s):
        return jnp.maximum(s - n_cast, 0)

    row_spec = pl.BlockSpec((TM_OUT, D_MODEL), lambda s: (tile(s), 0))
    s_spec = pl.BlockSpec((n_s, D_MODEL), lambda s: (0, 0))
    return pl.pallas_call(
        functools.partial(_outproj_kernel, n_cast),
        grid=(n_cast + n_tiles,),
        in_specs=[row_spec, row_spec, s_spec, s_spec,
                  pl.BlockSpec((None, W_CAST_ROWS, D_MODEL), lambda s: (layer, jnp.minimum(s, n_cast - 1), 0)),
                  pl.BlockSpec((None, 1, D_MODEL), lambda s: (layer, 0, 0))],
        out_specs=[row_spec, row_spec, s_spec],
        out_shape=[jax.ShapeDtypeStruct((rows, D_MODEL), F32),
                   jax.ShapeDtypeStruct((rows, D_MODEL), BF16),
                   jax.ShapeDtypeStruct((n_s, D_MODEL), F32)],
        scratch_shapes=[pltpu.VMEM((D_MODEL, D_MODEL), BF16)],
        compiler_params=_cparams("arbitrary"),
        name="outproj",
    )(mixp, xp, mixs, xs, w_all, g_all)


def _mix_sample_kernel(z_ref, sca_ref, scb_ref, scc_ref, slru_ref, sret_ref,
                       wa_ref, lng_ref, lnb_ref, wb_ref, wc_ref, bcc_ref,
                       wga_ref, bga_ref, wgx_ref, bgx_ref, lam_ref,
                       cc_ref, ss_ref, gdec_ref,
                       mix_ref, oca_ref, ocb_ref, occ_ref, olru_ref, oret_ref,
                       o_ref):
    w = W_GROUP
    bt = z_ref.shape[0]

    def zs(k):
        return z_ref[:, k * w:(k + 1) * w]

    u = zs(0) * jax.nn.sigmoid(zs(1))
    y = wa_ref[CONV_A_WIDTH - 1:CONV_A_WIDTH, :] * u
    for k in range(CONV_A_WIDTH - 1):
        y = y + wa_ref[k:k + 1, :] * sca_ref[:, k, :]
    mix_ref[:, 0:w] = _layernorm_silu(y, lng_ref[...], lnb_ref[...]).astype(BF16)
    oca_ref[:, 0:CONV_A_WIDTH - 2, :] = sca_ref[:, 1:CONV_A_WIDTH - 1, :]
    oca_ref[:, CONV_A_WIDTH - 2, :] = u

    v = zs(3) * zs(4)
    yb = wb_ref[0:1, :] * scb_ref[:, 0, :] + wb_ref[1:2, :] * scb_ref[:, 1, :] + wb_ref[2:3, :] * v
    mix_ref[:, w:2 * w] = (zs(2) * yb).astype(BF16)
    ocb_ref[:, 0, :] = scb_ref[:, 1, :]
    ocb_ref[:, 1, :] = v

    cx = zs(6)
    xr = (wc_ref[0:1, :] * scc_ref[:, 0, :] + wc_ref[1:2, :] * scc_ref[:, 1, :]
          + wc_ref[2:3, :] * scc_ref[:, 2, :] + wc_ref[3:4, :] * cx) + bcc_ref[...]
    occ_ref[:, 0, :] = scc_ref[:, 1, :]
    occ_ref[:, 1, :] = scc_ref[:, 2, :]
    occ_ref[:, 2, :] = cx
    a, b = _lru_gates(xr, wga_ref, bga_ref, wgx_ref, bgx_ref, lam_ref)
    h_new = a * slru_ref[...] + b
    olru_ref[...] = h_new
    mix_ref[:, 2 * w:3 * w] = (h_new * jax.nn.gelu(zs(5))).astype(BF16)

    scale = HEAD_DIM ** -0.5
    row0 = lax.broadcasted_iota(jnp.int32, (SUBLANES, HEAD_DIM), 0) == 0
    for h in range(RET_HEADS):
        cols = slice(h * HEAD_DIM, (h + 1) * HEAD_DIM)
        qr = _rope(z_ref[:, 7 * w + h * HEAD_DIM:7 * w + (h + 1) * HEAD_DIM], cc_ref[...], ss_ref[...])
        kr = _rope(z_ref[:, 8 * w + h * HEAD_DIM:8 * w + (h + 1) * HEAD_DIM], cc_ref[...], ss_ref[...]) * scale
        vv = z_ref[:, 9 * w + h * HEAD_DIM:9 * w + (h + 1) * HEAD_DIM]
        g = gdec_ref[h]
        for bi in range(bt):
            k8 = jnp.where(row0, jnp.broadcast_to(kr[bi:bi + 1, :], (SUBLANES, HEAD_DIM)), 0.0).astype(BF16)
            v8 = jnp.broadcast_to(vv[bi:bi + 1, :], (SUBLANES, HEAD_DIM)).astype(BF16)
            q8 = jnp.broadcast_to(qr[bi:bi + 1, :], (SUBLANES, HEAD_DIM)).astype(BF16)
            s_new = sret_ref[bi, h] * g + _dot_tn(k8, v8)
            oret_ref[bi, h] = s_new
            o8 = _dot(q8, s_new.astype(BF16))
            o_ref[bi:bi + 1, cols] = o8[0:1, :]
    for h in range(RET_HEADS):
        cols = slice(h * HEAD_DIM, (h + 1) * HEAD_DIM)
        o = o_ref[:, cols]
        on = o * lax.rsqrt(jnp.mean(o * o, axis=-1, keepdims=True) + EPS)
        mix_ref[:, 3 * w + h * HEAD_DIM:3 * w + (h + 1) * HEAD_DIM] = (
            jax.nn.silu(z_ref[:, 10 * w + h * HEAD_DIM:10 * w + (h + 1) * HEAD_DIM]) * on).astype(BF16)


def _mix_sample(z, layer, p, st, stacked):
    n_s = z.shape[0]
    depth = st[0].shape[0]
    w = W_GROUP
    _, _, _, gdec = _retention_tables(1)
    gdec = jnp.broadcast_to(gdec[:, None, None], (RET_HEADS, 1, HEAD_DIM))
    cc, ss = _rope_tables(PAST_LEN + jnp.arange(1, dtype=jnp.int32))

    def vec_spec():
        return pl.BlockSpec((None, 1, w), lambda i: (layer, 0, 0))

    def layer_spec(shape):
        nd = len(shape)
        return pl.BlockSpec((None,) + shape, lambda i: (layer,) + (0,) * nd)

    def state_spec(shape):
        nd = len(shape)
        return pl.BlockSpec((None, BT) + shape, lambda i: (layer, i) + (0,) * nd)

    state_shapes = [(CONV_A_WIDTH - 1, w), (CONV_B_WIDTH - 1, w), (CONV_C_WIDTH - 1, w), (w,),
                    (RET_HEADS, HEAD_DIM, HEAD_DIM)]
    in_specs = [pl.BlockSpec((BT, N_PROJ * w), lambda i: (i, 0))] + [state_spec(s) for s in state_shapes] + [
        layer_spec((CONV_A_WIDTH, w)), vec_spec(), vec_spec(),
        layer_spec((CONV_B_WIDTH, w)), layer_spec((CONV_C_WIDTH, w)), vec_spec(),
        layer_spec((LRU_HEADS, HEAD_DIM, HEAD_DIM)), vec_spec(),
        layer_spec((LRU_HEADS, HEAD_DIM, HEAD_DIM)), vec_spec(), vec_spec(),
        pl.BlockSpec((1, HEAD_DIM), lambda i: (0, 0)), pl.BlockSpec((1, HEAD_DIM), lambda i: (0, 0)),
        pl.BlockSpec((RET_HEADS, 1, HEAD_DIM), lambda i: (0, 0, 0)),
    ]
    out_specs = [pl.BlockSpec((BT, D_MODEL), lambda i: (i, 0))] + [state_spec(s) for s in state_shapes]
    out_shape = [jax.ShapeDtypeStruct((n_s, D_MODEL), BF16)] + [
        jax.ShapeDtypeStruct((depth, n_s) + s, F32) for s in state_shapes]
    n_in = len(in_specs)
    stacked = () if stacked is None else tuple(stacked)
    in_specs = in_specs + [pl.BlockSpec(memory_space=pl.ANY)] * len(stacked)
    return pl.pallas_call(
        _skip_refs(_mix_sample_kernel, n_in, len(stacked)),
        grid=(n_s // BT,),
        in_specs=in_specs,
        input_output_aliases={n_in + k: 1 + k for k in range(len(stacked))},
        out_specs=out_specs,
        out_shape=out_shape,
        scratch_shapes=[pltpu.VMEM((BT, w), F32)],
        compiler_params=_cparams("arbitrary"),
        name="mix_sample",
    )(z, *st, p["w_conv_a"], p["ln_a_g"], p["ln_a_b"], p["w_conv_b"], p["w_conv_c"], p["b_conv_c"],
      p["w_gate_a"], p["b_gate_a"], p["w_gate_x"], p["b_gate_x"], p["lru_lambda"], cc, ss, gdec, *stacked)


def kernel(x_prompt, x_sample, state_conv_a, state_conv_b, state_conv_c, state_lru_c, state_ret_d, state_conv_ffn,
           g_mix, w_in, w_conv_a, ln_a_g, ln_a_b, w_conv_b, w_conv_c, b_conv_c, w_gate_a, b_gate_a, w_gate_x,
           b_gate_x, lru_lambda, w_out, g_ffn, w_ffn_gate, w_ffn_up, w_conv_ffn, w_ffn_down, g_final):
    n_seq, seq, _ = x_prompt.shape
    n_s = x_sample.shape[0]
    depth = w_in.shape[0]
    assert x_sample.shape[1] == 1 and seq % TL == 0 and seq % TM == 0 and n_s % BT == 0

    def rows3(v):
        return v.reshape(depth, 1, v.shape[-1])

    p = dict(w_conv_a=w_conv_a, ln_a_g=rows3(ln_a_g), ln_a_b=rows3(ln_a_b), w_conv_b=w_conv_b,
             w_conv_c=w_conv_c, b_conv_c=rows3(b_conv_c), w_gate_a=w_gate_a, b_gate_a=rows3(b_gate_a),
             w_gate_x=w_gate_x, b_gate_x=rows3(b_gate_x), lru_lambda=rows3(lru_lambda))
    g_mix, g_ffn, g_final = rows3(g_mix), rows3(g_ffn), g_final.reshape(1, 1, D_MODEL)
    xp = x_prompt.reshape(n_seq * seq, D_MODEL)
    xs = x_sample.reshape(n_s, D_MODEL)
    sample_states = (state_conv_a, state_conv_b, state_conv_c, state_lru_c, state_ret_d)
    new_p = [[] for _ in range(6)]
    mix_states, scf = None, None
    for layer in range(depth):
        mixp, zs, pca, pcb, pcc, plru, pret = _mixer(xp, xs, n_seq, layer, g_mix, w_in, p)
        mixs, *mix_states = _mix_sample(zs, layer, p, sample_states, mix_states)
        xp, h2p, xs = _outproj(mixp, xp, mixs, xs, w_out, g_ffn, layer)
        hidp, hids, pcf, scf = _ffn_up(h2p, xs, g_ffn, w_ffn_gate, w_ffn_up, w_conv_ffn, state_conv_ffn,
                                       layer, n_seq, scf)
        xp, xs = _resproj(hidp, hids, w_ffn_down, layer, xp, xs, TM_DOWN, "ffn_down")
        for lst, s in zip(new_p, (pca, pcb, pcc, plru.reshape(n_seq, W_GROUP), pret, pcf)):
            lst.append(s)
    y_prompt = _rmsnorm(xp, g_final, 0, F32, TM_NORM).reshape(n_seq, seq, D_MODEL)
    y_sample = _rmsnorm(xs, g_final, 0, F32, n_s).reshape(n_s, 1, D_MODEL)
    return (y_prompt, y_sample, *(jnp.stack(l) for l in new_p), *mix_states, scf)
```

```python
import functools

import jax
import jax.numpy as jnp
from jax import lax
from jax.experimental import pallas as pl
from jax.experimental.pallas import tpu as pltpu

F32 = jnp.float32
BF16 = jnp.bfloat16

D_MODEL = 2048
N_GROUPS = 4
W_GROUP = D_MODEL // N_GROUPS
N_PROJ = 11
CONV_A_WIDTH = 31
CONV_B_WIDTH = 3
CONV_C_WIDTH = 4
CONV_FFN_WIDTH = 3
LRU_HEADS = 4
LRU_C = 8.0
RET_HEADS = 4
HEAD_DIM = W_GROUP // RET_HEADS
RET_CHUNK = 128
ROPE_BASE = 10000.0
D_FF = 5632
EPS = 1e-6
PAST_LEN = 16384

SUBLANES = 8
VMEM_LIMIT_BYTES = 56 * 1024 * 1024

TM = 2048
TN = 512
TM_DOWN = 512
TM_OUT = 512
FFN_ROWS = 512
MM_ROWS = 512
TM_NORM = 1024
TL = 256
W_CAST_ROWS = 256
CONV_A_ROWS = 32
CONV_A_HALO = 32
BT = 16


def _cparams(*semantics):
    return pltpu.CompilerParams(dimension_semantics=semantics, vmem_limit_bytes=VMEM_LIMIT_BYTES)


def _rms_scale(x):
    return x * lax.rsqrt(jnp.mean(x * x, axis=-1, keepdims=True) + EPS)


def _row_chunks(n_rows):
    chunk = min(MM_ROWS, n_rows // 2)
    return [slice(r, r + chunk) for r in range(0, n_rows, chunk)]


def _skip_refs(body, n_in, n_skip):
    def wrapped(*refs):
        return body(*refs[:n_in], *refs[n_in + n_skip:])
    return wrapped


def _dot(a, b):
    return jnp.dot(a, b, preferred_element_type=F32)


def _dot_nt(a, b):
    return lax.dot_general(a, b, (((1,), (1,)), ((), ())), preferred_element_type=F32)


def _dot_tn(a, b):
    return lax.dot_general(a, b, (((0,), (0,)), ((), ())), preferred_element_type=F32)


def _rmsnorm_kernel(x_ref, g_ref, o_ref):
    o_ref[...] = (_rms_scale(x_ref[...]) * g_ref[...]).astype(o_ref.dtype)


def _rmsnorm(x, g_all, layer, out_dtype, tm):
    rows = x.shape[0]
    return pl.pallas_call(
        _rmsnorm_kernel,
        grid=(rows // tm,),
        in_specs=[pl.BlockSpec((tm, D_MODEL), lambda i: (i, 0)),
                  pl.BlockSpec((None, 1, D_MODEL), lambda i: (layer, 0, 0))],
        out_specs=pl.BlockSpec((tm, D_MODEL), lambda i: (i, 0)),
        out_shape=jax.ShapeDtypeStruct((rows, D_MODEL), out_dtype),
        compiler_params=_cparams("arbitrary"),
        name="rmsnorm",
    )(x, g_all)


def _resproj_kernel(ap_ref, as_ref, w_ref, xp_ref, xs_ref, op_ref, os_ref, wb_ref):
    @pl.when(pl.program_id(1) == 0)
    def _():
        wb_ref[...] = w_ref[...].astype(BF16)
        os_ref[...] = xs_ref[...] + _dot(as_ref[...], wb_ref[...])

    for rows in _row_chunks(ap_ref.shape[0]):
        op_ref[rows, :] = xp_ref[rows, :] + _dot(ap_ref[rows, :], wb_ref[...])


def _resproj(ap, a_s, w_all, layer, xp, xs, tm, name):
    rows, n_s = ap.shape[0], a_s.shape[0]
    k_dim = ap.shape[1]
    return pl.pallas_call(
        _resproj_kernel,
        grid=(D_MODEL // TN, rows // tm),
        in_specs=[pl.BlockSpec((tm, k_dim), lambda j, i: (i, 0)),
                  pl.BlockSpec((n_s, k_dim), lambda j, i: (0, 0)),
                  pl.BlockSpec((None, k_dim, TN), lambda j, i: (layer, 0, j)),
                  pl.BlockSpec((tm, TN), lambda j, i: (i, j)),
                  pl.BlockSpec((n_s, TN), lambda j, i: (0, j))],
        out_specs=[pl.BlockSpec((tm, TN), lambda j, i: (i, j)),
                   pl.BlockSpec((n_s, TN), lambda j, i: (0, j))],
        out_shape=[jax.ShapeDtypeStruct((rows, D_MODEL), F32),
                   jax.ShapeDtypeStruct((n_s, D_MODEL), F32)],
        scratch_shapes=[pltpu.VMEM((k_dim, TN), BF16)],
        compiler_params=_cparams("arbitrary", "arbitrary"),
        name=name,
    )(ap, a_s, w_all, xp, xs)


def _ffn_up_kernel(tiles_per_seq, hp_ref, xs_ref, g_ref, wg_ref, wu_ref, wc_ref, st_ref,
                   hidp_ref, hids_ref, pst_ref, sst_ref, wgb_ref, wub_ref, carry_ref):
    i = pl.program_id(1)
    w0 = wc_ref[0:1, :]
    w1 = wc_ref[1:2, :]
    w2 = wc_ref[2:3, :]

    @pl.when(i == 0)
    def _():
        wgb_ref[...] = wg_ref[...].astype(BF16)
        wub_ref[...] = wu_ref[...].astype(BF16)
        hs = (_rms_scale(xs_ref[...]) * g_ref[...]).astype(BF16)
        gate = _dot(hs, wgb_ref[...])
        up = _dot(hs, wub_ref[...])
        s0 = st_ref[:, 0, :]
        s1 = st_ref[:, 1, :]
        gpre = w0 * s0 + w1 * s1 + w2 * gate
        hids_ref[...] = (jax.nn.silu(gpre) * up).astype(BF16)
        sst_ref[:, 0, :] = s1
        sst_ref[:, 1, :] = gate

    tm = hp_ref.shape[0]
    first = (i % tiles_per_seq) == 0
    prev = jnp.where(first, 0.0, carry_ref[...])
    row = lax.broadcasted_iota(jnp.int32, prev.shape, 0)
    for c in range(tm // FFN_ROWS):
        rows = slice(c * FFN_ROWS, (c + 1) * FFN_ROWS)
        hp = hp_ref[rows, :]
        gate = _dot(hp, wgb_ref[...])
        up = _dot(hp, wub_ref[...])
        gpre = w0 * _shift_rows(gate, prev, 2, row) + w1 * _shift_rows(gate, prev, 1, row) + w2 * gate
        hidp_ref[rows, :] = (jax.nn.silu(gpre) * up).astype(BF16)
        prev = gate[FFN_ROWS - SUBLANES:, :]
    carry_ref[...] = prev
    pst_ref[0] = prev[SUBLANES - (CONV_FFN_WIDTH - 1):, :]


def _ffn_up(hp, xs, g_all, wg_all, wu_all, wc_all, st_all, layer, n_seq, stacked):
    rows, n_s = hp.shape[0], xs.shape[0]
    depth = st_all.shape[0]
    tiles_per_seq = rows // n_seq // TM
    w_spec = pl.BlockSpec((None, D_MODEL, TN), lambda j, i: (layer, 0, j))
    in_specs = [pl.BlockSpec((TM, D_MODEL), lambda j, i: (i, 0)),
                pl.BlockSpec((n_s, D_MODEL), lambda j, i: (0, 0)),
                pl.BlockSpec((None, 1, D_MODEL), lambda j, i: (layer, 0, 0)),
                w_spec, w_spec,
                pl.BlockSpec((None, CONV_FFN_WIDTH, TN), lambda j, i: (layer, 0, j)),
                pl.BlockSpec((None, n_s, CONV_FFN_WIDTH - 1, TN), lambda j, i: (layer, 0, 0, j))]
    n_in = len(in_specs)
    stacked = () if stacked is None else (stacked,)
    in_specs = in_specs + [pl.BlockSpec(memory_space=pl.ANY)] * len(stacked)
    return pl.pallas_call(
        _skip_refs(functools.partial(_ffn_up_kernel, tiles_per_seq), n_in, len(stacked)),
        grid=(D_FF // TN, rows // TM),
        in_specs=in_specs,
        out_specs=[pl.BlockSpec((TM, TN), lambda j, i: (i, j)),
                   pl.BlockSpec((n_s, TN), lambda j, i: (0, j)),
                   pl.BlockSpec((1, CONV_FFN_WIDTH - 1, TN), lambda j, i: (i // tiles_per_seq, 0, j)),
                   pl.BlockSpec((None, n_s, CONV_FFN_WIDTH - 1, TN), lambda j, i: (layer, 0, 0, j))],
        out_shape=[jax.ShapeDtypeStruct((rows, D_FF), BF16),
                   jax.ShapeDtypeStruct((n_s, D_FF), BF16),
                   jax.ShapeDtypeStruct((n_seq, CONV_FFN_WIDTH - 1, D_FF), F32),
                   jax.ShapeDtypeStruct((depth, n_s, CONV_FFN_WIDTH - 1, D_FF), F32)],
        input_output_aliases={n_in + k: 3 for k in range(len(stacked))},
        scratch_shapes=[pltpu.VMEM((D_MODEL, TN), BF16),
                        pltpu.VMEM((D_MODEL, TN), BF16),
                        pltpu.VMEM((SUBLANES, TN), F32)],
        compiler_params=_cparams("arbitrary", "arbitrary"),
        name="ffn_up",
    )(hp, xs, g_all, wg_all, wu_all, wc_all, st_all, *stacked)


def _shift_rows(x, prev, d, row):
    r = pltpu.roll(x, d, axis=0)
    top = jnp.where(row < d, pltpu.roll(prev, d, axis=0), r[0:SUBLANES])
    return jnp.concatenate([top, r[SUBLANES:]], axis=0)


def _layernorm_silu(y, g, b):
    mu = jnp.mean(y, axis=-1, keepdims=True)
    yc = y - mu
    yn = yc * lax.rsqrt(jnp.mean(yc * yc, axis=-1, keepdims=True) + EPS)
    return jax.nn.silu(yn * g + b)


def _lru_gate_dots(xr, wga_ref, wgx_ref):
    ra, rx = [], []
    for h in range(LRU_HEADS):
        xh = xr[:, h * HEAD_DIM:(h + 1) * HEAD_DIM].astype(BF16)
        ra.append(_dot(xh, wga_ref[h].astype(BF16)))
        rx.append(_dot(xh, wgx_ref[h].astype(BF16)))
    return jnp.concatenate(ra, axis=-1), jnp.concatenate(rx, axis=-1)


def _lru_decay(xr, ra, rx, bga_ref, bgx_ref, lam_ref):
    r = jax.nn.sigmoid(ra + bga_ref[...])
    ig = jax.nn.sigmoid(rx + bgx_ref[...])
    log_a = -LRU_C * r * jax.nn.softplus(-lam_ref[...])
    a = jnp.exp(log_a)
    one_minus_a2 = -jnp.tanh(log_a) * (a * a + 1.0)
    b = jnp.sqrt(jnp.maximum(one_minus_a2, 0.0)) * (ig * xr)
    return a, b


def _lru_gates(xr, wga_ref, bga_ref, wgx_ref, bgx_ref, lam_ref):
    ra, rx = _lru_gate_dots(xr, wga_ref, wgx_ref)
    return _lru_decay(xr, ra, rx, bga_ref, bgx_ref, lam_ref)


def _rope(t, cc, ss):
    return t * cc + pltpu.roll(t, HEAD_DIM // 2, axis=1) * ss


def _retention_tables(chunk):
    log_g = jnp.log(1.0 - 2.0 ** (-5.0 - jnp.arange(RET_HEADS, dtype=F32)))
    idx = jnp.arange(chunk, dtype=F32)
    rel = idx[:, None] - idx[None, :]
    dmask = jnp.where(rel >= 0, jnp.exp(log_g[:, None, None] * jnp.maximum(rel, 0.0)), 0.0)
    inner = jnp.exp(log_g[:, None] * (idx + 1.0))
    sdec = jnp.exp(log_g[:, None] * (chunk - 1.0 - idx))
    cdec = jnp.exp(log_g * chunk)
    return dmask, inner, sdec, cdec


def _rope_tables(pos):
    half = HEAD_DIM // 2
    inv = ROPE_BASE ** (-jnp.arange(half, dtype=F32) / half)
    ang = pos.astype(F32)[:, None] * inv[None, :]
    cos, sin = jnp.cos(ang), jnp.sin(ang)
    return jnp.concatenate([cos, cos], axis=-1), jnp.concatenate([-sin, sin], axis=-1)


def _conv_a_rows(ub_ref, wa_ref, base, row):
    n_sub = CONV_A_ROWS // SUBLANES
    n_tiles = n_sub + CONV_A_HALO // SUBLANES
    tap0 = CONV_A_HALO - (CONV_A_WIDTH - 1)
    tiles = [ub_ref[base + i * SUBLANES:base + (i + 1) * SUBLANES, :] for i in range(n_tiles)]
    acc = [None] * n_sub
    for s in range(SUBLANES):
        if s == 0:
            win = tiles
        else:
            rolled = [pltpu.roll(t, SUBLANES - s, axis=0) for t in tiles]
            win = [jnp.where(row < SUBLANES - s, rolled[i], rolled[i + 1]) for i in range(n_tiles - 1)]
        for k in range(CONV_A_WIDTH):
            blk, sk = divmod(tap0 + k, SUBLANES)
            if sk != s:
                continue
            wk = wa_ref[k]
            for j in range(n_sub):
                term = wk * win[blk + j]
                acc[j] = term if acc[j] is None else acc[j] + term
    return jnp.concatenate(acc, axis=0)


def _mixer_kernel(n_cast, tiles_per_seq,
                  xn_ref, xs_ref, g1_ref, win_ref,
                  wa_ref, lng_ref, lnb_ref, wb_ref, wc_ref, bcc_ref,
                  wga_ref, bga_ref, wgx_ref, bgx_ref, lam_ref,
                  cc_ref, ss_ref, dmask_ref, inner_ref, sdec_ref, cdec_ref,
                  mix_ref, zs_ref, pca_ref, pcb_ref, pcc_ref, plru_ref, pret_ref,
                  wbin_ref, h_ref, z_ref, ub_ref, vb_ref, xb_ref, hl_ref, s_ref):
    step = pl.program_id(0)
    w = W_GROUP
    tl = xn_ref.shape[0]

    @pl.when(step == 0)
    def _():
        ub_ref[...] = jnp.zeros(ub_ref.shape, F32)
        vb_ref[...] = jnp.zeros(vb_ref.shape, F32)
        xb_ref[...] = jnp.zeros(xb_ref.shape, F32)
        hl_ref[...] = jnp.zeros(hl_ref.shape, F32)
        s_ref[...] = jnp.zeros(s_ref.shape, F32)

    @pl.when(step < n_cast)
    def _():
        r0 = pl.multiple_of(step * W_CAST_ROWS, W_CAST_ROWS)
        wbin_ref[pl.ds(r0, W_CAST_ROWS), :] = win_ref[...].astype(BF16)

    @pl.when(step == n_cast - 1)
    def _():
        h_ref[...] = (_rms_scale(xn_ref[...]) * g1_ref[...]).astype(BF16)
        hs = (_rms_scale(xs_ref[...]) * g1_ref[...]).astype(BF16)
        for k in range(N_PROJ):
            zs_ref[:, k * w:(k + 1) * w] = _dot(hs, wbin_ref[:, k * w:(k + 1) * w])

    @pl.when(step >= n_cast)
    def _():
        first = ((step - n_cast) % tiles_per_seq) == 0
        row = lax.broadcasted_iota(jnp.int32, (SUBLANES, w), 0)
        heads = range(RET_HEADS)
        chunks = range(tl // RET_CHUNK)

        def zcol(k, hd=None, c=None):
            if hd is None:
                return z_ref[:, k * w:(k + 1) * w]
            return z_ref[c * RET_CHUNK:(c + 1) * RET_CHUNK, k * w + hd * HEAD_DIM:k * w + (hd + 1) * HEAD_DIM]

        def inproj(k):
            z_ref[:, k * w:(k + 1) * w] = _dot(h_ref[...], wbin_ref[:, k * w:(k + 1) * w])

        def conv_a_chunks(lo, hi):
            for ci in range(lo, hi):
                base = ci * CONV_A_ROWS
                ya = _layernorm_silu(_conv_a_rows(ub_ref, wa_ref, base, row), lng_ref[...], lnb_ref[...])
                mix_ref[base:base + CONV_A_ROWS, 0:w] = ya.astype(BF16)

        assert tl // CONV_A_ROWS == 8

        inproj(0)
        inproj(1)
        u = zcol(0) * jax.nn.sigmoid(zcol(1))
        ub_ref[0:CONV_A_HALO, :] = jnp.where(first, 0.0, ub_ref[tl:tl + CONV_A_HALO, :])
        ub_ref[CONV_A_HALO:CONV_A_HALO + tl, :] = u
        pca_ref[0] = u[tl - (CONV_A_WIDTH - 1):, :]
        inproj(7)
        conv_a_chunks(0, 1)
        inproj(8)
        conv_a_chunks(1, 2)
        inproj(6)
        scale = HEAD_DIM ** -0.5
        qb, kb, ksd = {}, {}, {}
        for c in chunks:
            cc = cc_ref[c * RET_CHUNK:(c + 1) * RET_CHUNK, :]
            ss = ss_ref[c * RET_CHUNK:(c + 1) * RET_CHUNK, :]
            for hd in heads:
                qr = _rope(zcol(7, hd, c), cc, ss)
                kr = _rope(zcol(8, hd, c), cc, ss) * scale
                qb[c, hd] = qr.astype(BF16)
                kb[c, hd] = kr.astype(BF16)
                ksd[c, hd] = (kr * sdec_ref[hd]).astype(BF16)
        inproj(9)
        cx = zcol(6)
        c_prev = jnp.where(first, 0.0, xb_ref[...])
        xr = (wc_ref[0:1, :] * _shift_rows(cx, c_prev, 3, row)
              + wc_ref[1:2, :] * _shift_rows(cx, c_prev, 2, row)
              + wc_ref[2:3, :] * _shift_rows(cx, c_prev, 1, row)
              + wc_ref[3:4, :] * cx) + bcc_ref[...]
        xb_ref[...] = cx[tl - SUBLANES:, :]
        pcc_ref[0] = cx[tl - (CONV_C_WIDTH - 1):, :]
        ra, rx = _lru_gate_dots(xr, wga_ref, wgx_ref)
        vv = {(c, hd): zcol(9, hd, c).astype(BF16) for c in chunks for hd in heads}
        conv_a_chunks(2, 3)

        s_cur = [jnp.where(first, 0.0, s_ref[hd]) for hd in heads]

        def retention_dots(c):
            scores = [_dot_nt(qb[c, hd], kb[c, hd]) for hd in heads]
            q_s = [_dot(qb[c, hd], s_cur[hd].astype(BF16)) for hd in heads]
            k_v = [_dot_tn(ksd[c, hd], vv[c, hd]) for hd in heads]
            return scores, q_s, k_v

        def retention_sv(c, scores):
            masked = [(scores[hd] * dmask_ref[hd]).astype(BF16) for hd in heads]
            return [_dot(masked[hd], vv[c, hd]) for hd in heads]

        def retention_finish(c, s_v, q_s, k_v):
            for hd in heads:
                o = s_v[hd] + q_s[hd] * inner_ref[hd]
                s_cur[hd] = s_cur[hd] * cdec_ref[hd] + k_v[hd]
                on = o * lax.rsqrt(jnp.mean(o * o, axis=-1, keepdims=True) + EPS)
                mix_ref[c * RET_CHUNK:(c + 1) * RET_CHUNK, 3 * w + hd * HEAD_DIM:3 * w + (hd + 1) * HEAD_DIM] = (
                    jax.nn.silu(zcol(10, hd, c)) * on).astype(BF16)

        scores, q_s, k_v = retention_dots(0)
        inproj(5)
        a, b = _lru_decay(xr, ra, rx, bga_ref, bgx_ref, lam_ref)
        s_v = retention_sv(0, scores)
        inproj(10)
        nblk = tl // SUBLANES
        a3 = a.reshape(nblk, SUBLANES, w)
        b3 = b.reshape(nblk, SUBLANES, w)
        row3 = lax.broadcasted_iota(jnp.int32, a3.shape, 1)
        for d in (1, 2, 4):
            keep_rows = row3 >= d
            a_sh = jnp.where(keep_rows, pltpu.roll(a3, d, axis=1), 1.0)
            b_sh = jnp.where(keep_rows, pltpu.roll(b3, d, axis=1), 0.0)
            b3 = a3 * b_sh + b3
            a3 = a3 * a_sh
        h_in = jnp.where(first, 0.0, hl_ref[...])
        hs = []
        for i in range(nblk):
            h_blk = a3[i] * h_in + b3[i]
            hs.append(h_blk)
            h_in = jnp.broadcast_to(h_blk[SUBLANES - 1:SUBLANES, :], (SUBLANES, w))
        hl_ref[...] = h_in
        plru_ref[0] = h_in[0:1, :]
        mix_ref[:, 2 * w:3 * w] = (jnp.concatenate(hs, axis=0) * jax.nn.gelu(zcol(5))).astype(BF16)
        conv_a_chunks(3, 4)
        retention_finish(0, s_v, q_s, k_v)
        scores, q_s, k_v = retention_dots(1)
        inproj(2)
        conv_a_chunks(4, 5)
        s_v = retention_sv(1, scores)
        inproj(3)
        conv_a_chunks(5, 6)
        retention_finish(1, s_v, q_s, k_v)
        for hd in heads:
            s_ref[hd] = s_cur[hd]
            pret_ref[0, hd] = s_cur[hd]
        inproj(4)
        conv_a_chunks(6, 8)
        h_ref[...] = (_rms_scale(xn_ref[...]) * g1_ref[...]).astype(BF16)
        v = zcol(3) * zcol(4)
        v_prev = jnp.where(first, 0.0, vb_ref[...])
        yb = (wb_ref[0:1, :] * _shift_rows(v, v_prev, 2, row)
              + wb_ref[1:2, :] * _shift_rows(v, v_prev, 1, row)
              + wb_ref[2:3, :] * v)
        vb_ref[...] = v[tl - SUBLANES:, :]
        pcb_ref[0] = v[tl - (CONV_B_WIDTH - 1):, :]
        mix_ref[:, w:2 * w] = (zcol(2) * yb).astype(BF16)


def _mixer(xp, xs, n_seq, layer, g_mix, w_in, p):
    rows, n_s = xp.shape[0], xs.shape[0]
    seq = rows // n_seq
    w = W_GROUP
    n_cast = D_MODEL // W_CAST_ROWS
    n_tiles = rows // TL
    tiles_per_seq = seq // TL
    assert TL // RET_CHUNK == 2
    dmask, inner, sdec, cdec = _retention_tables(RET_CHUNK)
    inner = jnp.broadcast_to(inner[:, :, None], (RET_HEADS, RET_CHUNK, HEAD_DIM))
    sdec = jnp.broadcast_to(sdec[:, :, None], (RET_HEADS, RET_CHUNK, HEAD_DIM))
    cdec = jnp.broadcast_to(cdec[:, None, None], (RET_HEADS, 1, HEAD_DIM))
    cc, ss = _rope_tables(jnp.arange(seq, dtype=jnp.int32))
    wa8 = jnp.broadcast_to(p["w_conv_a"][layer][:, None, :], (CONV_A_WIDTH, SUBLANES, w))

    def tile(s):
        return jnp.clip(s - n_cast, 0, n_tiles - 1)

    def next_tile(s):
        return jnp.clip(s - n_cast + 1, 0, n_tiles - 1)

    def cast_chunk(s):
        return jnp.minimum(s, n_cast - 1)

    def vec_spec(n):
        return pl.BlockSpec((None, 1, n), lambda s: (layer, 0, 0))

    def full_spec(shape):
        nd = len(shape)
        return pl.BlockSpec(shape, lambda s: (0,) * nd)

    def layer_spec(shape):
        nd = len(shape)
        return pl.BlockSpec((None,) + shape, lambda s: (layer,) + (0,) * nd)

    def seq_spec(shape):
        nd = len(shape)
        return pl.BlockSpec((1,) + shape, lambda s: (tile(s) // tiles_per_seq,) + (0,) * nd)

    rope_spec = pl.BlockSpec((TL, HEAD_DIM), lambda s: (tile(s) % tiles_per_seq, 0))
    in_specs = [
        pl.BlockSpec((TL, D_MODEL), lambda s: (next_tile(s), 0)),
        pl.BlockSpec((n_s, D_MODEL), lambda s: (0, 0)),
        vec_spec(D_MODEL),
        pl.BlockSpec((None, W_CAST_ROWS, N_PROJ * w), lambda s: (layer, cast_chunk(s), 0)),
        full_spec((CONV_A_WIDTH, SUBLANES, w)), vec_spec(w), vec_spec(w),
        layer_spec((CONV_B_WIDTH, w)), layer_spec((CONV_C_WIDTH, w)), vec_spec(w),
        layer_spec((LRU_HEADS, HEAD_DIM, HEAD_DIM)), vec_spec(w),
        layer_spec((LRU_HEADS, HEAD_DIM, HEAD_DIM)), vec_spec(w), vec_spec(w),
        rope_spec, rope_spec,
        full_spec((RET_HEADS, RET_CHUNK, RET_CHUNK)), full_spec((RET_HEADS, RET_CHUNK, HEAD_DIM)),
        full_spec((RET_HEADS, RET_CHUNK, HEAD_DIM)), full_spec((RET_HEADS, 1, HEAD_DIM)),
    ]
    state_shapes = [(CONV_A_WIDTH - 1, w), (CONV_B_WIDTH - 1, w), (CONV_C_WIDTH - 1, w), (1, w),
                    (RET_HEADS, HEAD_DIM, HEAD_DIM)]
    out_specs = [pl.BlockSpec((TL, D_MODEL), lambda s: (tile(s), 0)),
                 pl.BlockSpec((n_s, N_PROJ * w), lambda s: (0, 0))] + [seq_spec(sh) for sh in state_shapes]
    out_shape = [jax.ShapeDtypeStruct((rows, D_MODEL), BF16),
                 jax.ShapeDtypeStruct((n_s, N_PROJ * w), F32)] + [
        jax.ShapeDtypeStruct((n_seq,) + sh, F32) for sh in state_shapes]
    scratch = [
        pltpu.VMEM((D_MODEL, N_PROJ * w), BF16),
        pltpu.VMEM((TL, D_MODEL), BF16),
        pltpu.VMEM((TL, N_PROJ * w), F32),
        pltpu.VMEM((CONV_A_HALO + TL, w), F32),
        pltpu.VMEM((SUBLANES, w), F32),
        pltpu.VMEM((SUBLANES, w), F32),
        pltpu.VMEM((SUBLANES, w), F32),
        pltpu.VMEM((RET_HEADS, HEAD_DIM, HEAD_DIM), F32),
    ]
    return pl.pallas_call(
        functools.partial(_mixer_kernel, n_cast, tiles_per_seq),
        grid=(n_cast + n_tiles,),
        in_specs=in_specs,
        out_specs=out_specs,
        out_shape=out_shape,
        scratch_shapes=scratch,
        compiler_params=_cparams("arbitrary"),
        name="mixer",
    )(xp, xs, g_mix, w_in, wa8, p["ln_a_g"], p["ln_a_b"], p["w_conv_b"], p["w_conv_c"], p["b_conv_c"],
      p["w_gate_a"], p["b_gate_a"], p["w_gate_x"], p["b_gate_x"], p["lru_lambda"],
      cc, ss, dmask, inner, sdec, cdec)


def _outproj_kernel(n_cast, mixp_ref, xp_ref, mixs_ref, xs_ref, w_ref, g_ref,
                    xo_ref, h2_ref, xso_ref, wb_ref):
    step = pl.program_id(0)

    @pl.when(step < n_cast)
    def _():
        r0 = pl.multiple_of(step * W_CAST_ROWS, W_CAST_ROWS)
        wb_ref[pl.ds(r0, W_CAST_ROWS), :] = w_ref[...].astype(BF16)

    @pl.when(step == n_cast)
    def _():
        xso_ref[...] = xs_ref[...] + _dot(mixs_ref[...], wb_ref[...])

    @pl.when(step >= n_cast)
    def _():
        for rows in _row_chunks(mixp_ref.shape[0]):
            x_new = xp_ref[rows, :] + _dot(mixp_ref[rows, :], wb_ref[...])
            xo_ref[rows, :] = x_new
            h2_ref[rows, :] = (_rms_scale(x_new) * g_ref[...]).astype(BF16)


def _outproj(mixp, xp, mixs, xs, w_all, g_all, layer):
    rows, n_s = xp.shape[0], xs.shape[0]
    n_cast = D_MODEL // W_CAST_ROWS
    n_tiles = rows // TM_OUT

    def tile(s):
        return jnp.maximum(s - n_cast, 0)

    row_spec = pl.BlockSpec((TM_OUT, D_MODEL), lambda s: (tile(s), 0))
    s_spec = pl.BlockSpec((n_s, D_MODEL), lambda s: (0, 0))
    return pl.pallas_call(
        functools.partial(_outproj_kernel, n_cast),
        grid=(n_cast + n_tiles,),
        in_specs=[row_spec, row_spec, s_spec, s_spec,
                  pl.BlockSpec((None, W_CAST_ROWS, D_MODEL), lambda s: (layer, jnp.minimum(s, n_cast - 1), 0)),
                  pl.BlockSpec((None, 1, D_MODEL), lambda s: (layer, 0, 0))],
        out_specs=[row_spec, row_spec, s_spec],
        out_shape=[jax.ShapeDtypeStruct((rows, D_MODEL), F32),
                   jax.ShapeDtypeStruct((rows, D_MODEL), BF16),
                   jax.ShapeDtypeStruct((n_s, D_MODEL), F32)],
        scratch_shapes=[pltpu.VMEM((D_MODEL, D_MODEL), BF16)],
        compiler_params=_cparams("arbitrary"),
        name="outproj",
    )(mixp, xp, mixs, xs, w_all, g_all)


def _mix_sample_kernel(z_ref, sca_ref, scb_ref, scc_ref, slru_ref, sret_ref,
                       wa_ref, lng_ref, lnb_ref, wb_ref, wc_ref, bcc_ref,
                       wga_ref, bga_ref, wgx_ref, bgx_ref, lam_ref,
                       cc_ref, ss_ref, gdec_ref,
                       mix_ref, oca_ref, ocb_ref, occ_ref, olru_ref, oret_ref,
                       o_ref):
    w = W_GROUP
    bt = z_ref.shape[0]

    def zs(k):
        return z_ref[:, k * w:(k + 1) * w]

    u = zs(0) * jax.nn.sigmoid(zs(1))
    y = wa_ref[CONV_A_WIDTH - 1:CONV_A_WIDTH, :] * u
    for k in range(CONV_A_WIDTH - 1):
        y = y + wa_ref[k:k + 1, :] * sca_ref[k]
    mix_ref[:, 0:w] = _layernorm_silu(y, lng_ref[...], lnb_ref[...]).astype(BF16)
    for k in range(CONV_A_WIDTH - 2):
        oca_ref[:, k, :] = sca_ref[k + 1]
    oca_ref[:, CONV_A_WIDTH - 2, :] = u

    v = zs(3) * zs(4)
    yb = wb_ref[0:1, :] * scb_ref[:, 0, :] + wb_ref[1:2, :] * scb_ref[:, 1, :] + wb_ref[2:3, :] * v
    mix_ref[:, w:2 * w] = (zs(2) * yb).astype(BF16)
    ocb_ref[:, 0, :] = scb_ref[:, 1, :]
    ocb_ref[:, 1, :] = v

    cx = zs(6)
    xr = (wc_ref[0:1, :] * scc_ref[:, 0, :] + wc_ref[1:2, :] * scc_ref[:, 1, :]
          + wc_ref[2:3, :] * scc_ref[:, 2, :] + wc_ref[3:4, :] * cx) + bcc_ref[...]
    occ_ref[:, 0, :] = scc_ref[:, 1, :]
    occ_ref[:, 1, :] = scc_ref[:, 2, :]
    occ_ref[:, 2, :] = cx
    a, b = _lru_gates(xr, wga_ref, bga_ref, wgx_ref, bgx_ref, lam_ref)
    h_new = a * slru_ref[...] + b
    olru_ref[...] = h_new
    mix_ref[:, 2 * w:3 * w] = (h_new * jax.nn.gelu(zs(5))).astype(BF16)

    scale = HEAD_DIM ** -0.5
    row0 = lax.broadcasted_iota(jnp.int32, (SUBLANES, HEAD_DIM), 0) == 0
    for h in range(RET_HEADS):
        cols = slice(h * HEAD_DIM, (h + 1) * HEAD_DIM)
        qr = _rope(z_ref[:, 7 * w + h * HEAD_DIM:7 * w + (h + 1) * HEAD_DIM], cc_ref[...], ss_ref[...])
        kr = _rope(z_ref[:, 8 * w + h * HEAD_DIM:8 * w + (h + 1) * HEAD_DIM], cc_ref[...], ss_ref[...]) * scale
        vv = z_ref[:, 9 * w + h * HEAD_DIM:9 * w + (h + 1) * HEAD_DIM]
        g = gdec_ref[h]
        for bi in range(bt):
            k8 = jnp.where(row0, jnp.broadcast_to(kr[bi:bi + 1, :], (SUBLANES, HEAD_DIM)), 0.0).astype(BF16)
            v8 = jnp.broadcast_to(vv[bi:bi + 1, :], (SUBLANES, HEAD_DIM)).astype(BF16)
            q8 = jnp.broadcast_to(qr[bi:bi + 1, :], (SUBLANES, HEAD_DIM)).astype(BF16)
            s_new = sret_ref[bi, h] * g + _dot_tn(k8, v8)
            oret_ref[bi, h] = s_new
            o8 = _dot(q8, s_new.astype(BF16))
            o_ref[bi:bi + 1, cols] = o8[0:1, :]
    for h in range(RET_HEADS):
        cols = slice(h * HEAD_DIM, (h + 1) * HEAD_DIM)
        o = o_ref[:, cols]
        on = o * lax.rsqrt(jnp.mean(o * o, axis=-1, keepdims=True) + EPS)
        mix_ref[:, 3 * w + h * HEAD_DIM:3 * w + (h + 1) * HEAD_DIM] = (
            jax.nn.silu(z_ref[:, 10 * w + h * HEAD_DIM:10 * w + (h + 1) * HEAD_DIM]) * on).astype(BF16)


def _mix_sample(z, layer, p, st, stacked):
    n_s = z.shape[0]
    depth = st[0].shape[0]
    w = W_GROUP
    _, _, _, gdec = _retention_tables(1)
    gdec = jnp.broadcast_to(gdec[:, None, None], (RET_HEADS, 1, HEAD_DIM))
    cc, ss = _rope_tables(PAST_LEN + jnp.arange(1, dtype=jnp.int32))

    def vec_spec():
        return pl.BlockSpec((None, 1, w), lambda i: (layer, 0, 0))

    def layer_spec(shape):
        nd = len(shape)
        return pl.BlockSpec((None,) + shape, lambda i: (layer,) + (0,) * nd)

    def state_spec(shape):
        nd = len(shape)
        return pl.BlockSpec((None, BT) + shape, lambda i: (layer, i) + (0,) * nd)

    state_shapes = [(CONV_A_WIDTH - 1, w), (CONV_B_WIDTH - 1, w), (CONV_C_WIDTH - 1, w), (w,),
                    (RET_HEADS, HEAD_DIM, HEAD_DIM)]
    conv_a_spec = pl.BlockSpec((None, CONV_A_WIDTH - 1, BT, w), lambda i: (layer, 0, i, 0))
    in_specs = [pl.BlockSpec((BT, N_PROJ * w), lambda i: (i, 0)), conv_a_spec] + [
        state_spec(s) for s in state_shapes[1:]] + [
        layer_spec((CONV_A_WIDTH, w)), vec_spec(), vec_spec(),
        layer_spec((CONV_B_WIDTH, w)), layer_spec((CONV_C_WIDTH, w)), vec_spec(),
        layer_spec((LRU_HEADS, HEAD_DIM, HEAD_DIM)), vec_spec(),
        layer_spec((LRU_HEADS, HEAD_DIM, HEAD_DIM)), vec_spec(), vec_spec(),
        pl.BlockSpec((1, HEAD_DIM), lambda i: (0, 0)), pl.BlockSpec((1, HEAD_DIM), lambda i: (0, 0)),
        pl.BlockSpec((RET_HEADS, 1, HEAD_DIM), lambda i: (0, 0, 0)),
    ]
    out_specs = [pl.BlockSpec((BT, D_MODEL), lambda i: (i, 0))] + [state_spec(s) for s in state_shapes]
    out_shape = [jax.ShapeDtypeStruct((n_s, D_MODEL), BF16)] + [
        jax.ShapeDtypeStruct((depth, n_s) + s, F32) for s in state_shapes]
    n_in = len(in_specs)
    stacked = () if stacked is None else tuple(stacked)
    in_specs = in_specs + [pl.BlockSpec(memory_space=pl.ANY)] * len(stacked)
    return pl.pallas_call(
        _skip_refs(_mix_sample_kernel, n_in, len(stacked)),
        grid=(n_s // BT,),
        in_specs=in_specs,
        input_output_aliases={n_in + k: 1 + k for k in range(len(stacked))},
        out_specs=out_specs,
        out_shape=out_shape,
        scratch_shapes=[pltpu.VMEM((BT, w), F32)],
        compiler_params=_cparams("arbitrary"),
        name="mix_sample",
    )(z, *st, p["w_conv_a"], p["ln_a_g"], p["ln_a_b"], p["w_conv_b"], p["w_conv_c"], p["b_conv_c"],
      p["w_gate_a"], p["b_gate_a"], p["w_gate_x"], p["b_gate_x"], p["lru_lambda"], cc, ss, gdec, *stacked)


def kernel(x_prompt, x_sample, state_conv_a, state_conv_b, state_conv_c, state_lru_c, state_ret_d, state_conv_ffn,
           g_mix, w_in, w_conv_a, ln_a_g, ln_a_b, w_conv_b, w_conv_c, b_conv_c, w_gate_a, b_gate_a, w_gate_x,
           b_gate_x, lru_lambda, w_out, g_ffn, w_ffn_gate, w_ffn_up, w_conv_ffn, w_ffn_down, g_final):
    n_seq, seq, _ = x_prompt.shape
    n_s = x_sample.shape[0]
    depth = w_in.shape[0]
    assert x_sample.shape[1] == 1 and seq % TL == 0 and seq % TM == 0 and n_s % BT == 0

    def rows3(v):
        return v.reshape(depth, 1, v.shape[-1])

    p = dict(w_conv_a=w_conv_a, ln_a_g=rows3(ln_a_g), ln_a_b=rows3(ln_a_b), w_conv_b=w_conv_b,
             w_conv_c=w_conv_c, b_conv_c=rows3(b_conv_c), w_gate_a=w_gate_a, b_gate_a=rows3(b_gate_a),
             w_gate_x=w_gate_x, b_gate_x=rows3(b_gate_x), lru_lambda=rows3(lru_lambda))
    g_mix, g_ffn, g_final = rows3(g_mix), rows3(g_ffn), g_final.reshape(1, 1, D_MODEL)
    xp = x_prompt.reshape(n_seq * seq, D_MODEL)
    xs = x_sample.reshape(n_s, D_MODEL)
    sample_states = (jnp.transpose(state_conv_a, (0, 2, 1, 3)), state_conv_b, state_conv_c, state_lru_c, state_ret_d)
    new_p = [[] for _ in range(6)]
    mix_states, scf = None, None
    for layer in range(depth):
        mixp, zs, pca, pcb, pcc, plru, pret = _mixer(xp, xs, n_seq, layer, g_mix, w_in, p)
        mixs, *mix_states = _mix_sample(zs, layer, p, sample_states, mix_states)
        xp, h2p, xs = _outproj(mixp, xp, mixs, xs, w_out, g_ffn, layer)
        hidp, hids, pcf, scf = _ffn_up(h2p, xs, g_ffn, w_ffn_gate, w_ffn_up, w_conv_ffn, state_conv_ffn,
                                       layer, n_seq, scf)
        xp, xs = _resproj(hidp, hids, w_ffn_down, layer, xp, xs, TM_DOWN, "ffn_down")
        for lst, s in zip(new_p, (pca, pcb, pcc, plru.reshape(n_seq, W_GROUP), pret, pcf)):
            lst.append(s)
    y_prompt = _rmsnorm(xp, g_final, 0, F32, TM_NORM).reshape(n_seq, seq, D_MODEL)
    y_sample = _rmsnorm(xs, g_final, 0, F32, n_s).reshape(n_s, 1, D_MODEL)
    return (y_prompt, y_sample, *(jnp.stack(l) for l in new_p), *mix_states, scf)
```

```python
import functools

import jax
import jax.numpy as jnp
from jax import lax
from jax.experimental import pallas as pl
from jax.experimental.pallas import tpu as pltpu

F32 = jnp.float32
BF16 = jnp.bfloat16

D_MODEL = 2048
N_GROUPS = 4
W_GROUP = D_MODEL // N_GROUPS
N_PROJ = 11
CONV_A_WIDTH = 31
CONV_B_WIDTH = 3
CONV_C_WIDTH = 4
CONV_FFN_WIDTH = 3
LRU_HEADS = 4
LRU_C = 8.0
RET_HEADS = 4
HEAD_DIM = W_GROUP // RET_HEADS
RET_CHUNK = 128
ROPE_BASE = 10000.0
D_FF = 5632
EPS = 1e-6
PAST_LEN = 16384

SUBLANES = 8
VMEM_LIMIT_BYTES = 56 * 1024 * 1024

TM = 2048
TN = 512
TM_DOWN = 1024
TM_OUT = 512
FFN_ROWS = 512
MM_ROWS = 512
TM_NORM = 1024
TL = 256
W_CAST_ROWS = 256
CONV_A_ROWS = 32
CONV_A_HALO = 32
BT = 16


def _cparams(*semantics):
    return pltpu.CompilerParams(dimension_semantics=semantics, vmem_limit_bytes=VMEM_LIMIT_BYTES)


def _rms_scale(x):
    return x * lax.rsqrt(jnp.mean(x * x, axis=-1, keepdims=True) + EPS)


def _row_chunks(n_rows):
    chunk = min(MM_ROWS, n_rows // 2)
    return [slice(r, r + chunk) for r in range(0, n_rows, chunk)]


def _skip_refs(body, n_in, n_skip):
    def wrapped(*refs):
        return body(*refs[:n_in], *refs[n_in + n_skip:])
    return wrapped


def _dot(a, b):
    return jnp.dot(a, b, preferred_element_type=F32)


def _dot_nt(a, b):
    return lax.dot_general(a, b, (((1,), (1,)), ((), ())), preferred_element_type=F32)


def _dot_tn(a, b):
    return lax.dot_general(a, b, (((0,), (0,)), ((), ())), preferred_element_type=F32)


def _rmsnorm_kernel(x_ref, g_ref, o_ref):
    o_ref[...] = (_rms_scale(x_ref[...]) * g_ref[...]).astype(o_ref.dtype)


def _rmsnorm(x, g_all, layer, out_dtype, tm):
    rows = x.shape[0]
    return pl.pallas_call(
        _rmsnorm_kernel,
        grid=(rows // tm,),
        in_specs=[pl.BlockSpec((tm, D_MODEL), lambda i: (i, 0)),
                  pl.BlockSpec((None, 1, D_MODEL), lambda i: (layer, 0, 0))],
        out_specs=pl.BlockSpec((tm, D_MODEL), lambda i: (i, 0)),
        out_shape=jax.ShapeDtypeStruct((rows, D_MODEL), out_dtype),
        compiler_params=_cparams("arbitrary"),
        name="rmsnorm",
    )(x, g_all)


def _resproj_kernel(ap_ref, as_ref, w_ref, xp_ref, xs_ref, op_ref, os_ref, wb_ref):
    @pl.when(pl.program_id(1) == 0)
    def _():
        wb_ref[...] = w_ref[...].astype(BF16)
        os_ref[...] = xs_ref[...] + _dot(as_ref[...], wb_ref[...])

    for rows in _row_chunks(ap_ref.shape[0]):
        op_ref[rows, :] = xp_ref[rows, :] + _dot(ap_ref[rows, :], wb_ref[...])


def _resproj(ap, a_s, w_all, layer, xp, xs, tm, name):
    rows, n_s = ap.shape[0], a_s.shape[0]
    k_dim = ap.shape[1]
    return pl.pallas_call(
        _resproj_kernel,
        grid=(D_MODEL // TN, rows // tm),
        in_specs=[pl.BlockSpec((tm, k_dim), lambda j, i: (i, 0)),
                  pl.BlockSpec((n_s, k_dim), lambda j, i: (0, 0)),
                  pl.BlockSpec((None, k_dim, TN), lambda j, i: (layer, 0, j), pipeline_mode=pl.Buffered(1)),
                  pl.BlockSpec((tm, TN), lambda j, i: (i, j)),
                  pl.BlockSpec((n_s, TN), lambda j, i: (0, j))],
        out_specs=[pl.BlockSpec((tm, TN), lambda j, i: (i, j)),
                   pl.BlockSpec((n_s, TN), lambda j, i: (0, j))],
        out_shape=[jax.ShapeDtypeStruct((rows, D_MODEL), F32),
                   jax.ShapeDtypeStruct((n_s, D_MODEL), F32)],
        scratch_shapes=[pltpu.VMEM((k_dim, TN), BF16)],
        compiler_params=_cparams("arbitrary", "arbitrary"),
        name=name,
    )(ap, a_s, w_all, xp, xs)


def _ffn_up_kernel(tiles_per_seq, hp_ref, xs_ref, g_ref, wg_ref, wu_ref, wc_ref, st_ref,
                   hidp_ref, hids_ref, pst_ref, sst_ref, wgb_ref, wub_ref, carry_ref):
    i = pl.program_id(1)
    w0 = wc_ref[0:1, :]
    w1 = wc_ref[1:2, :]
    w2 = wc_ref[2:3, :]

    @pl.when(i == 0)
    def _():
        wgb_ref[...] = wg_ref[...].astype(BF16)
        wub_ref[...] = wu_ref[...].astype(BF16)
        hs = (_rms_scale(xs_ref[...]) * g_ref[...]).astype(BF16)
        gate = _dot(hs, wgb_ref[...])
        up = _dot(hs, wub_ref[...])
        s0 = st_ref[:, 0, :]
        s1 = st_ref[:, 1, :]
        gpre = w0 * s0 + w1 * s1 + w2 * gate
        hids_ref[...] = (jax.nn.silu(gpre) * up).astype(BF16)
        sst_ref[:, 0, :] = s1
        sst_ref[:, 1, :] = gate

    tm = hp_ref.shape[0]
    first = (i % tiles_per_seq) == 0
    prev = jnp.where(first, 0.0, carry_ref[...])
    row = lax.broadcasted_iota(jnp.int32, prev.shape, 0)
    for c in range(tm // FFN_ROWS):
        rows = slice(c * FFN_ROWS, (c + 1) * FFN_ROWS)
        hp = hp_ref[rows, :]
        gate = _dot(hp, wgb_ref[...])
        up = _dot(hp, wub_ref[...])
        gpre = w0 * _shift_rows(gate, prev, 2, row) + w1 * _shift_rows(gate, prev, 1, row) + w2 * gate
        hidp_ref[rows, :] = (jax.nn.silu(gpre) * up).astype(BF16)
        prev = gate[FFN_ROWS - SUBLANES:, :]
    carry_ref[...] = prev
    pst_ref[0] = prev[SUBLANES - (CONV_FFN_WIDTH - 1):, :]


def _ffn_up(hp, xs, g_all, wg_all, wu_all, wc_all, st_all, layer, n_seq, stacked):
    rows, n_s = hp.shape[0], xs.shape[0]
    depth = st_all.shape[0]
    tiles_per_seq = rows // n_seq // TM
    w_spec = pl.BlockSpec((None, D_MODEL, TN), lambda j, i: (layer, 0, j))
    in_specs = [pl.BlockSpec((TM, D_MODEL), lambda j, i: (i, 0)),
                pl.BlockSpec((n_s, D_MODEL), lambda j, i: (0, 0)),
                pl.BlockSpec((None, 1, D_MODEL), lambda j, i: (layer, 0, 0)),
                w_spec, w_spec,
                pl.BlockSpec((None, CONV_FFN_WIDTH, TN), lambda j, i: (layer, 0, j)),
                pl.BlockSpec((None, n_s, CONV_FFN_WIDTH - 1, TN), lambda j, i: (layer, 0, 0, j))]
    n_in = len(in_specs)
    stacked = () if stacked is None else (stacked,)
    in_specs = in_specs + [pl.BlockSpec(memory_space=pl.ANY)] * len(stacked)
    return pl.pallas_call(
        _skip_refs(functools.partial(_ffn_up_kernel, tiles_per_seq), n_in, len(stacked)),
        grid=(D_FF // TN, rows // TM),
        in_specs=in_specs,
        out_specs=[pl.BlockSpec((TM, TN), lambda j, i: (i, j)),
                   pl.BlockSpec((n_s, TN), lambda j, i: (0, j)),
                   pl.BlockSpec((1, CONV_FFN_WIDTH - 1, TN), lambda j, i: (i // tiles_per_seq, 0, j)),
                   pl.BlockSpec((None, n_s, CONV_FFN_WIDTH - 1, TN), lambda j, i: (layer, 0, 0, j))],
        out_shape=[jax.ShapeDtypeStruct((rows, D_FF), BF16),
                   jax.ShapeDtypeStruct((n_s, D_FF), BF16),
                   jax.ShapeDtypeStruct((n_seq, CONV_FFN_WIDTH - 1, D_FF), F32),
                   jax.ShapeDtypeStruct((depth, n_s, CONV_FFN_WIDTH - 1, D_FF), F32)],
        input_output_aliases={n_in + k: 3 for k in range(len(stacked))},
        scratch_shapes=[pltpu.VMEM((D_MODEL, TN), BF16),
                        pltpu.VMEM((D_MODEL, TN), BF16),
                        pltpu.VMEM((SUBLANES, TN), F32)],
        compiler_params=_cparams("arbitrary", "arbitrary"),
        name="ffn_up",
    )(hp, xs, g_all, wg_all, wu_all, wc_all, st_all, *stacked)


def _shift_rows(x, prev, d, row):
    r = pltpu.roll(x, d, axis=0)
    top = jnp.where(row < d, pltpu.roll(prev, d, axis=0), r[0:SUBLANES])
    return jnp.concatenate([top, r[SUBLANES:]], axis=0)


def _layernorm_silu(y, g, b):
    mu = jnp.mean(y, axis=-1, keepdims=True)
    yc = y - mu
    yn = yc * lax.rsqrt(jnp.mean(yc * yc, axis=-1, keepdims=True) + EPS)
    return jax.nn.silu(yn * g + b)


def _lru_gate_dots(xr, wga_ref, wgx_ref):
    ra, rx = [], []
    for h in range(LRU_HEADS):
        xh = xr[:, h * HEAD_DIM:(h + 1) * HEAD_DIM].astype(BF16)
        ra.append(_dot(xh, wga_ref[h].astype(BF16)))
        rx.append(_dot(xh, wgx_ref[h].astype(BF16)))
    return jnp.concatenate(ra, axis=-1), jnp.concatenate(rx, axis=-1)


def _lru_decay(xr, ra, rx, bga_ref, bgx_ref, lam_ref):
    r = jax.nn.sigmoid(ra + bga_ref[...])
    ig = jax.nn.sigmoid(rx + bgx_ref[...])
    log_a = -LRU_C * r * jax.nn.softplus(-lam_ref[...])
    a = jnp.exp(log_a)
    one_minus_a2 = -jnp.tanh(log_a) * (a * a + 1.0)
    b = jnp.sqrt(jnp.maximum(one_minus_a2, 0.0)) * (ig * xr)
    return a, b


def _lru_gates(xr, wga_ref, bga_ref, wgx_ref, bgx_ref, lam_ref):
    ra, rx = _lru_gate_dots(xr, wga_ref, wgx_ref)
    return _lru_decay(xr, ra, rx, bga_ref, bgx_ref, lam_ref)


def _rope(t, cc, ss):
    return t * cc + pltpu.roll(t, HEAD_DIM // 2, axis=1) * ss


def _retention_tables(chunk):
    log_g = jnp.log(1.0 - 2.0 ** (-5.0 - jnp.arange(RET_HEADS, dtype=F32)))
    idx = jnp.arange(chunk, dtype=F32)
    rel = idx[:, None] - idx[None, :]
    dmask = jnp.where(rel >= 0, jnp.exp(log_g[:, None, None] * jnp.maximum(rel, 0.0)), 0.0)
    inner = jnp.exp(log_g[:, None] * (idx + 1.0))
    sdec = jnp.exp(log_g[:, None] * (chunk - 1.0 - idx))
    cdec = jnp.exp(log_g * chunk)
    return dmask, inner, sdec, cdec


def _rope_tables(pos):
    half = HEAD_DIM // 2
    inv = ROPE_BASE ** (-jnp.arange(half, dtype=F32) / half)
    ang = pos.astype(F32)[:, None] * inv[None, :]
    cos, sin = jnp.cos(ang), jnp.sin(ang)
    return jnp.concatenate([cos, cos], axis=-1), jnp.concatenate([-sin, sin], axis=-1)


def _conv_a_rows(ub_ref, wa_ref, base, row):
    n_sub = CONV_A_ROWS // SUBLANES
    n_tiles = n_sub + CONV_A_HALO // SUBLANES
    tap0 = CONV_A_HALO - (CONV_A_WIDTH - 1)
    tiles = [ub_ref[base + i * SUBLANES:base + (i + 1) * SUBLANES, :] for i in range(n_tiles)]
    acc = [None] * n_sub
    for s in range(SUBLANES):
        if s == 0:
            win = tiles
        else:
            rolled = [pltpu.roll(t, SUBLANES - s, axis=0) for t in tiles]
            win = [jnp.where(row < SUBLANES - s, rolled[i], rolled[i + 1]) for i in range(n_tiles - 1)]
        for k in range(CONV_A_WIDTH):
            blk, sk = divmod(tap0 + k, SUBLANES)
            if sk != s:
                continue
            wk = wa_ref[k]
            for j in range(n_sub):
                term = wk * win[blk + j]
                acc[j] = term if acc[j] is None else acc[j] + term
    return jnp.concatenate(acc, axis=0)


def _mixer_kernel(n_cast, tiles_per_seq,
                  xn_ref, xs_ref, g1_ref, win_ref,
                  wa_ref, lng_ref, lnb_ref, wb_ref, wc_ref, bcc_ref,
                  wga_ref, bga_ref, wgx_ref, bgx_ref, lam_ref,
                  cc_ref, ss_ref, dmask_ref, inner_ref, sdec_ref, cdec_ref,
                  mix_ref, zs_ref, pca_ref, pcb_ref, pcc_ref, plru_ref, pret_ref,
                  wbin_ref, h_ref, z_ref, ub_ref, vb_ref, xb_ref, hl_ref, s_ref):
    step = pl.program_id(0)
    w = W_GROUP
    tl = xn_ref.shape[0]

    @pl.when(step == 0)
    def _():
        ub_ref[...] = jnp.zeros(ub_ref.shape, F32)
        vb_ref[...] = jnp.zeros(vb_ref.shape, F32)
        xb_ref[...] = jnp.zeros(xb_ref.shape, F32)
        hl_ref[...] = jnp.zeros(hl_ref.shape, F32)
        s_ref[...] = jnp.zeros(s_ref.shape, F32)

    @pl.when(step < n_cast)
    def _():
        r0 = pl.multiple_of(step * W_CAST_ROWS, W_CAST_ROWS)
        wbin_ref[pl.ds(r0, W_CAST_ROWS), :] = win_ref[...].astype(BF16)

    @pl.when(step == n_cast - 1)
    def _():
        h_ref[...] = (_rms_scale(xn_ref[...]) * g1_ref[...]).astype(BF16)
        hs = (_rms_scale(xs_ref[...]) * g1_ref[...]).astype(BF16)
        for k in range(N_PROJ):
            zs_ref[:, k * w:(k + 1) * w] = _dot(hs, wbin_ref[:, k * w:(k + 1) * w])

    @pl.when(step >= n_cast)
    def _():
        first = ((step - n_cast) % tiles_per_seq) == 0
        row = lax.broadcasted_iota(jnp.int32, (SUBLANES, w), 0)
        heads = range(RET_HEADS)
        chunks = range(tl // RET_CHUNK)

        def zcol(k, hd=None, c=None):
            if hd is None:
                return z_ref[:, k * w:(k + 1) * w]
            return z_ref[c * RET_CHUNK:(c + 1) * RET_CHUNK, k * w + hd * HEAD_DIM:k * w + (hd + 1) * HEAD_DIM]

        def inproj(k):
            z_ref[:, k * w:(k + 1) * w] = _dot(h_ref[...], wbin_ref[:, k * w:(k + 1) * w])

        def conv_a_chunks(lo, hi):
            for ci in range(lo, hi):
                base = ci * CONV_A_ROWS
                ya = _layernorm_silu(_conv_a_rows(ub_ref, wa_ref, base, row), lng_ref[...], lnb_ref[...])
                mix_ref[base:base + CONV_A_ROWS, 0:w] = ya.astype(BF16)

        assert tl // CONV_A_ROWS == 8

        inproj(0)
        inproj(1)
        u = zcol(0) * jax.nn.sigmoid(zcol(1))
        ub_ref[0:CONV_A_HALO, :] = jnp.where(first, 0.0, ub_ref[tl:tl + CONV_A_HALO, :])
        ub_ref[CONV_A_HALO:CONV_A_HALO + tl, :] = u
        pca_ref[0] = u[tl - (CONV_A_WIDTH - 1):, :]
        inproj(7)
        conv_a_chunks(0, 1)
        inproj(8)
        conv_a_chunks(1, 2)
        inproj(6)
        scale = HEAD_DIM ** -0.5
        qb, kb, ksd = {}, {}, {}
        for c in chunks:
            cc = cc_ref[c * RET_CHUNK:(c + 1) * RET_CHUNK, :]
            ss = ss_ref[c * RET_CHUNK:(c + 1) * RET_CHUNK, :]
            for hd in heads:
                qr = _rope(zcol(7, hd, c), cc, ss)
                kr = _rope(zcol(8, hd, c), cc, ss) * scale
                qb[c, hd] = qr.astype(BF16)
                kb[c, hd] = kr.astype(BF16)
                ksd[c, hd] = (kr * sdec_ref[hd]).astype(BF16)
        inproj(9)
        cx = zcol(6)
        c_prev = jnp.where(first, 0.0, xb_ref[...])
        xr = (wc_ref[0:1, :] * _shift_rows(cx, c_prev, 3, row)
              + wc_ref[1:2, :] * _shift_rows(cx, c_prev, 2, row)
              + wc_ref[2:3, :] * _shift_rows(cx, c_prev, 1, row)
              + wc_ref[3:4, :] * cx) + bcc_ref[...]
        xb_ref[...] = cx[tl - SUBLANES:, :]
        pcc_ref[0] = cx[tl - (CONV_C_WIDTH - 1):, :]
        ra, rx = _lru_gate_dots(xr, wga_ref, wgx_ref)
        vv = {(c, hd): zcol(9, hd, c).astype(BF16) for c in chunks for hd in heads}
        conv_a_chunks(2, 3)

        s_cur = [jnp.where(first, 0.0, s_ref[hd]) for hd in heads]

        def retention_dots(c):
            scores = [_dot_nt(qb[c, hd], kb[c, hd]) for hd in heads]
            q_s = [_dot(qb[c, hd], s_cur[hd].astype(BF16)) for hd in heads]
            k_v = [_dot_tn(ksd[c, hd], vv[c, hd]) for hd in heads]
            return scores, q_s, k_v

        def retention_sv(c, scores):
            masked = [(scores[hd] * dmask_ref[hd]).astype(BF16) for hd in heads]
            return [_dot(masked[hd], vv[c, hd]) for hd in heads]

        def retention_finish(c, s_v, q_s, k_v):
            for hd in heads:
                o = s_v[hd] + q_s[hd] * inner_ref[hd]
                s_cur[hd] = s_cur[hd] * cdec_ref[hd] + k_v[hd]
                on = o * lax.rsqrt(jnp.mean(o * o, axis=-1, keepdims=True) + EPS)
                mix_ref[c * RET_CHUNK:(c + 1) * RET_CHUNK, 3 * w + hd * HEAD_DIM:3 * w + (hd + 1) * HEAD_DIM] = (
                    jax.nn.silu(zcol(10, hd, c)) * on).astype(BF16)

        scores, q_s, k_v = retention_dots(0)
        inproj(5)
        a, b = _lru_decay(xr, ra, rx, bga_ref, bgx_ref, lam_ref)
        s_v = retention_sv(0, scores)
        inproj(10)
        nblk = tl // SUBLANES
        a3 = a.reshape(nblk, SUBLANES, w)
        b3 = b.reshape(nblk, SUBLANES, w)
        row3 = lax.broadcasted_iota(jnp.int32, a3.shape, 1)
        for d in (1, 2, 4):
            keep_rows = row3 >= d
            a_sh = jnp.where(keep_rows, pltpu.roll(a3, d, axis=1), 1.0)
            b_sh = jnp.where(keep_rows, pltpu.roll(b3, d, axis=1), 0.0)
            b3 = a3 * b_sh + b3
            a3 = a3 * a_sh
        h_in = jnp.where(first, 0.0, hl_ref[...])
        hs = []
        for i in range(nblk):
            h_blk = a3[i] * h_in + b3[i]
            hs.append(h_blk)
            h_in = jnp.broadcast_to(h_blk[SUBLANES - 1:SUBLANES, :], (SUBLANES, w))
        hl_ref[...] = h_in
        plru_ref[0] = h_in[0:1, :]
        mix_ref[:, 2 * w:3 * w] = (jnp.concatenate(hs, axis=0) * jax.nn.gelu(zcol(5))).astype(BF16)
        conv_a_chunks(3, 4)
        retention_finish(0, s_v, q_s, k_v)
        scores, q_s, k_v = retention_dots(1)
        inproj(2)
        conv_a_chunks(4, 5)
        s_v = retention_sv(1, scores)
        inproj(3)
        conv_a_chunks(5, 6)
        retention_finish(1, s_v, q_s, k_v)
        for hd in heads:
            s_ref[hd] = s_cur[hd]
            pret_ref[0, hd] = s_cur[hd]
        inproj(4)
        conv_a_chunks(6, 8)
        h_ref[...] = (_rms_scale(xn_ref[...]) * g1_ref[...]).astype(BF16)
        v = zcol(3) * zcol(4)
        v_prev = jnp.where(first, 0.0, vb_ref[...])
        yb = (wb_ref[0:1, :] * _shift_rows(v, v_prev, 2, row)
              + wb_ref[1:2, :] * _shift_rows(v, v_prev, 1, row)
              + wb_ref[2:3, :] * v)
        vb_ref[...] = v[tl - SUBLANES:, :]
        pcb_ref[0] = v[tl - (CONV_B_WIDTH - 1):, :]
        mix_ref[:, w:2 * w] = (zcol(2) * yb).astype(BF16)


def _mixer(xp, xs, n_seq, layer, g_mix, w_in, p):
    rows, n_s = xp.shape[0], xs.shape[0]
    seq = rows // n_seq
    w = W_GROUP
    n_cast = D_MODEL // W_CAST_ROWS
    n_tiles = rows // TL
    tiles_per_seq = seq // TL
    assert TL // RET_CHUNK == 2
    dmask, inner, sdec, cdec = _retention_tables(RET_CHUNK)
    inner = jnp.broadcast_to(inner[:, :, None], (RET_HEADS, RET_CHUNK, HEAD_DIM))
    sdec = jnp.broadcast_to(sdec[:, :, None], (RET_HEADS, RET_CHUNK, HEAD_DIM))
    cdec = jnp.broadcast_to(cdec[:, None, None], (RET_HEADS, 1, HEAD_DIM))
    cc, ss = _rope_tables(jnp.arange(seq, dtype=jnp.int32))
    wa8 = jnp.broadcast_to(p["w_conv_a"][layer][:, None, :], (CONV_A_WIDTH, SUBLANES, w))

    def tile(s):
        return jnp.clip(s - n_cast, 0, n_tiles - 1)

    def next_tile(s):
        return jnp.clip(s - n_cast + 1, 0, n_tiles - 1)

    def cast_chunk(s):
        return jnp.minimum(s, n_cast - 1)

    def vec_spec(n):
        return pl.BlockSpec((None, 1, n), lambda s: (layer, 0, 0))

    def full_spec(shape):
        nd = len(shape)
        return pl.BlockSpec(shape, lambda s: (0,) * nd)

    def layer_spec(shape):
        nd = len(shape)
        return pl.BlockSpec((None,) + shape, lambda s: (layer,) + (0,) * nd)

    def seq_spec(shape):
        nd = len(shape)
        return pl.BlockSpec((1,) + shape, lambda s: (tile(s) // tiles_per_seq,) + (0,) * nd)

    rope_spec = pl.BlockSpec((TL, HEAD_DIM), lambda s: (tile(s) % tiles_per_seq, 0))
    in_specs = [
        pl.BlockSpec((TL, D_MODEL), lambda s: (next_tile(s), 0)),
        pl.BlockSpec((n_s, D_MODEL), lambda s: (0, 0)),
        vec_spec(D_MODEL),
        pl.BlockSpec((None, W_CAST_ROWS, N_PROJ * w), lambda s: (layer, cast_chunk(s), 0)),
        full_spec((CONV_A_WIDTH, SUBLANES, w)), vec_spec(w), vec_spec(w),
        layer_spec((CONV_B_WIDTH, w)), layer_spec((CONV_C_WIDTH, w)), vec_spec(w),
        layer_spec((LRU_HEADS, HEAD_DIM, HEAD_DIM)), vec_spec(w),
        layer_spec((LRU_HEADS, HEAD_DIM, HEAD_DIM)), vec_spec(w), vec_spec(w),
        rope_spec, rope_spec,
        full_spec((RET_HEADS, RET_CHUNK, RET_CHUNK)), full_spec((RET_HEADS, RET_CHUNK, HEAD_DIM)),
        full_spec((RET_HEADS, RET_CHUNK, HEAD_DIM)), full_spec((RET_HEADS, 1, HEAD_DIM)),
    ]
    state_shapes = [(CONV_A_WIDTH - 1, w), (CONV_B_WIDTH - 1, w), (CONV_C_WIDTH - 1, w), (1, w),
                    (RET_HEADS, HEAD_DIM, HEAD_DIM)]
    out_specs = [pl.BlockSpec((TL, D_MODEL), lambda s: (tile(s), 0)),
                 pl.BlockSpec((n_s, N_PROJ * w), lambda s: (0, 0))] + [seq_spec(sh) for sh in state_shapes]
    out_shape = [jax.ShapeDtypeStruct((rows, D_MODEL), BF16),
                 jax.ShapeDtypeStruct((n_s, N_PROJ * w), F32)] + [
        jax.ShapeDtypeStruct((n_seq,) + sh, F32) for sh in state_shapes]
    scratch = [
        pltpu.VMEM((D_MODEL, N_PROJ * w), BF16),
        pltpu.VMEM((TL, D_MODEL), BF16),
        pltpu.VMEM((TL, N_PROJ * w), F32),
        pltpu.VMEM((CONV_A_HALO + TL, w), F32),
        pltpu.VMEM((SUBLANES, w), F32),
        pltpu.VMEM((SUBLANES, w), F32),
        pltpu.VMEM((SUBLANES, w), F32),
        pltpu.VMEM((RET_HEADS, HEAD_DIM, HEAD_DIM), F32),
    ]
    return pl.pallas_call(
        functools.partial(_mixer_kernel, n_cast, tiles_per_seq),
        grid=(n_cast + n_tiles,),
        in_specs=in_specs,
        out_specs=out_specs,
        out_shape=out_shape,
        scratch_shapes=scratch,
        compiler_params=_cparams("arbitrary"),
        name="mixer",
    )(xp, xs, g_mix, w_in, wa8, p["ln_a_g"], p["ln_a_b"], p["w_conv_b"], p["w_conv_c"], p["b_conv_c"],
      p["w_gate_a"], p["b_gate_a"], p["w_gate_x"], p["b_gate_x"], p["lru_lambda"],
      cc, ss, dmask, inner, sdec, cdec)


def _outproj_kernel(n_cast, mixp_ref, xp_ref, mixs_ref, xs_ref, w_ref, g_ref,
                    xo_ref, h2_ref, xso_ref, wb_ref):
    step = pl.program_id(0)

    @pl.when(step < n_cast)
    def _():
        r0 = pl.multiple_of(step * W_CAST_ROWS, W_CAST_ROWS)
        wb_ref[pl.ds(r0, W_CAST_ROWS), :] = w_ref[...].astype(BF16)

    @pl.when(step == n_cast)
    def _():
        xso_ref[...] = xs_ref[...] + _dot(mixs_ref[...], wb_ref[...])

    @pl.when(step >= n_cast)
    def _():
        for rows in _row_chunks(mixp_ref.shape[0]):
            x_new = xp_ref[rows, :] + _dot(mixp_ref[rows, :], wb_ref[...])
            xo_ref[rows, :] = x_new
            h2_ref[rows, :] = (_rms_scale(x_new) * g_ref[...]).astype(BF16)


def _outproj(mixp, xp, mixs, xs, w_all, g_all, layer):
    rows, n_s = xp.shape[0], xs.shape[0]
    n_cast = D_MODEL // W_CAST_ROWS
    n_tiles = rows // TM_OUT

    def tile(s):
        return jnp.maximum(s - n_cast, 0)

    row_spec = pl.BlockSpec((TM_OUT, D_MODEL), lambda s: (tile(s), 0))
    s_spec = pl.BlockSpec((n_s, D_MODEL), lambda s: (0, 0))
    return pl.pallas_call(
        functools.partial(_outproj_kernel, n_cast),
        grid=(n_cast + n_tiles,),
        in_specs=[row_spec, row_spec, s_spec, s_spec,
                  pl.BlockSpec((None, W_CAST_ROWS, D_MODEL), lambda s: (layer, jnp.minimum(s, n_cast - 1), 0)),
                  pl.BlockSpec((None, 1, D_MODEL), lambda s: (layer, 0, 0))],
        out_specs=[row_spec, row_spec, s_spec],
        out_shape=[jax.ShapeDtypeStruct((rows, D_MODEL), F32),
                   jax.ShapeDtypeStruct((rows, D_MODEL), BF16),
                   jax.ShapeDtypeStruct((n_s, D_MODEL), F32)],
        scratch_shapes=[pltpu.VMEM((D_MODEL, D_MODEL), BF16)],
        compiler_params=_cparams("arbitrary"),
        name="outproj",
    )(mixp, xp, mixs, xs, w_all, g_all)


def _mix_sample_kernel(z_ref, sca_ref, scb_ref, scc_ref, slru_ref, sret_ref,
                       wa_ref, lng_ref, lnb_ref, wb_ref, wc_ref, bcc_ref,
                       wga_ref, bga_ref, wgx_ref, bgx_ref, lam_ref,
                       cc_ref, ss_ref, gdec_ref,
                       mix_ref, oca_ref, ocb_ref, occ_ref, olru_ref, oret_ref,
                       o_ref):
    w = W_GROUP
    bt = z_ref.shape[0]

    def zs(k):
        return z_ref[:, k * w:(k + 1) * w]

    u = zs(0) * jax.nn.sigmoid(zs(1))
    y = wa_ref[CONV_A_WIDTH - 1:CONV_A_WIDTH, :] * u
    for k in range(CONV_A_WIDTH - 1):
        y = y + wa_ref[k:k + 1, :] * sca_ref[k]
    mix_ref[:, 0:w] = _layernorm_silu(y, lng_ref[...], lnb_ref[...]).astype(BF16)
    for k in range(CONV_A_WIDTH - 2):
        oca_ref[:, k, :] = sca_ref[k + 1]
    oca_ref[:, CONV_A_WIDTH - 2, :] = u

    v = zs(3) * zs(4)
    yb = wb_ref[0:1, :] * scb_ref[:, 0, :] + wb_ref[1:2, :] * scb_ref[:, 1, :] + wb_ref[2:3, :] * v
    mix_ref[:, w:2 * w] = (zs(2) * yb).astype(BF16)
    ocb_ref[:, 0, :] = scb_ref[:, 1, :]
    ocb_ref[:, 1, :] = v

    cx = zs(6)
    xr = (wc_ref[0:1, :] * scc_ref[:, 0, :] + wc_ref[1:2, :] * scc_ref[:, 1, :]
          + wc_ref[2:3, :] * scc_ref[:, 2, :] + wc_ref[3:4, :] * cx) + bcc_ref[...]
    occ_ref[:, 0, :] = scc_ref[:, 1, :]
    occ_ref[:, 1, :] = scc_ref[:, 2, :]
    occ_ref[:, 2, :] = cx
    a, b = _lru_gates(xr, wga_ref, bga_ref, wgx_ref, bgx_ref, lam_ref)
    h_new = a * slru_ref[...] + b
    olru_ref[...] = h_new
    mix_ref[:, 2 * w:3 * w] = (h_new * jax.nn.gelu(zs(5))).astype(BF16)

    scale = HEAD_DIM ** -0.5
    row0 = lax.broadcasted_iota(jnp.int32, (SUBLANES, HEAD_DIM), 0) == 0
    for h in range(RET_HEADS):
        cols = slice(h * HEAD_DIM, (h + 1) * HEAD_DIM)
        qr = _rope(z_ref[:, 7 * w + h * HEAD_DIM:7 * w + (h + 1) * HEAD_DIM], cc_ref[...], ss_ref[...])
        kr = _rope(z_ref[:, 8 * w + h * HEAD_DIM:8 * w + (h + 1) * HEAD_DIM], cc_ref[...], ss_ref[...]) * scale
        vv = z_ref[:, 9 * w + h * HEAD_DIM:9 * w + (h + 1) * HEAD_DIM]
        g = gdec_ref[h]
        for bi in range(bt):
            k8 = jnp.where(row0, jnp.broadcast_to(kr[bi:bi + 1, :], (SUBLANES, HEAD_DIM)), 0.0).astype(BF16)
            v8 = jnp.broadcast_to(vv[bi:bi + 1, :], (SUBLANES, HEAD_DIM)).astype(BF16)
            q8 = jnp.broadcast_to(qr[bi:bi + 1, :], (SUBLANES, HEAD_DIM)).astype(BF16)
            s_new = sret_ref[bi, h] * g + _dot_tn(k8, v8)
            oret_ref[bi, h] = s_new
            o8 = _dot(q8, s_new.astype(BF16))
            o_ref[bi:bi + 1, cols] = o8[0:1, :]
    for h in range(RET_HEADS):
        cols = slice(h * HEAD_DIM, (h + 1) * HEAD_DIM)
        o = o_ref[:, cols]
        on = o * lax.rsqrt(jnp.mean(o * o, axis=-1, keepdims=True) + EPS)
        mix_ref[:, 3 * w + h * HEAD_DIM:3 * w + (h + 1) * HEAD_DIM] = (
            jax.nn.silu(z_ref[:, 10 * w + h * HEAD_DIM:10 * w + (h + 1) * HEAD_DIM]) * on).astype(BF16)


def _mix_sample(z, layer, p, st, stacked):
    n_s = z.shape[0]
    depth = st[0].shape[0]
    w = W_GROUP
    _, _, _, gdec = _retention_tables(1)
    gdec = jnp.broadcast_to(gdec[:, None, None], (RET_HEADS, 1, HEAD_DIM))
    cc, ss = _rope_tables(PAST_LEN + jnp.arange(1, dtype=jnp.int32))

    def vec_spec():
        return pl.BlockSpec((None, 1, w), lambda i: (layer, 0, 0))

    def layer_spec(shape):
        nd = len(shape)
        return pl.BlockSpec((None,) + shape, lambda i: (layer,) + (0,) * nd)

    def state_spec(shape):
        nd = len(shape)
        return pl.BlockSpec((None, BT) + shape, lambda i: (layer, i) + (0,) * nd)

    state_shapes = [(CONV_A_WIDTH - 1, w), (CONV_B_WIDTH - 1, w), (CONV_C_WIDTH - 1, w), (w,),
                    (RET_HEADS, HEAD_DIM, HEAD_DIM)]
    conv_a_spec = pl.BlockSpec((None, CONV_A_WIDTH - 1, BT, w), lambda i: (layer, 0, i, 0))
    in_specs = [pl.BlockSpec((BT, N_PROJ * w), lambda i: (i, 0)), conv_a_spec] + [
        state_spec(s) for s in state_shapes[1:]] + [
        layer_spec((CONV_A_WIDTH, w)), vec_spec(), vec_spec(),
        layer_spec((CONV_B_WIDTH, w)), layer_spec((CONV_C_WIDTH, w)), vec_spec(),
        layer_spec((LRU_HEADS, HEAD_DIM, HEAD_DIM)), vec_spec(),
        layer_spec((LRU_HEADS, HEAD_DIM, HEAD_DIM)), vec_spec(), vec_spec(),
        pl.BlockSpec((1, HEAD_DIM), lambda i: (0, 0)), pl.BlockSpec((1, HEAD_DIM), lambda i: (0, 0)),
        pl.BlockSpec((RET_HEADS, 1, HEAD_DIM), lambda i: (0, 0, 0)),
    ]
    out_specs = [pl.BlockSpec((BT, D_MODEL), lambda i: (i, 0))] + [state_spec(s) for s in state_shapes]
    out_shape = [jax.ShapeDtypeStruct((n_s, D_MODEL), BF16)] + [
        jax.ShapeDtypeStruct((depth, n_s) + s, F32) for s in state_shapes]
    n_in = len(in_specs)
    stacked = () if stacked is None else tuple(stacked)
    in_specs = in_specs + [pl.BlockSpec(memory_space=pl.ANY)] * len(stacked)
    return pl.pallas_call(
        _skip_refs(_mix_sample_kernel, n_in, len(stacked)),
        grid=(n_s // BT,),
        in_specs=in_specs,
        input_output_aliases={n_in + k: 1 + k for k in range(len(stacked))},
        out_specs=out_specs,
        out_shape=out_shape,
        scratch_shapes=[pltpu.VMEM((BT, w), F32)],
        compiler_params=_cparams("arbitrary"),
        name="mix_sample",
    )(z, *st, p["w_conv_a"], p["ln_a_g"], p["ln_a_b"], p["w_conv_b"], p["w_conv_c"], p["b_conv_c"],
      p["w_gate_a"], p["b_gate_a"], p["w_gate_x"], p["b_gate_x"], p["lru_lambda"], cc, ss, gdec, *stacked)


def kernel(x_prompt, x_sample, state_conv_a, state_conv_b, state_conv_c, state_lru_c, state_ret_d, state_conv_ffn,
           g_mix, w_in, w_conv_a, ln_a_g, ln_a_b, w_conv_b, w_conv_c, b_conv_c, w_gate_a, b_gate_a, w_gate_x,
           b_gate_x, lru_lambda, w_out, g_ffn, w_ffn_gate, w_ffn_up, w_conv_ffn, w_ffn_down, g_final):
    n_seq, seq, _ = x_prompt.shape
    n_s = x_sample.shape[0]
    depth = w_in.shape[0]
    assert x_sample.shape[1] == 1 and seq % TL == 0 and seq % TM == 0 and n_s % BT == 0

    def rows3(v):
        return v.reshape(depth, 1, v.shape[-1])

    p = dict(w_conv_a=w_conv_a, ln_a_g=rows3(ln_a_g), ln_a_b=rows3(ln_a_b), w_conv_b=w_conv_b,
             w_conv_c=w_conv_c, b_conv_c=rows3(b_conv_c), w_gate_a=w_gate_a, b_gate_a=rows3(b_gate_a),
             w_gate_x=w_gate_x, b_gate_x=rows3(b_gate_x), lru_lambda=rows3(lru_lambda))
    g_mix, g_ffn, g_final = rows3(g_mix), rows3(g_ffn), g_final.reshape(1, 1, D_MODEL)
    xp = x_prompt.reshape(n_seq * seq, D_MODEL)
    xs = x_sample.reshape(n_s, D_MODEL)
    sample_states = (jnp.transpose(state_conv_a, (0, 2, 1, 3)), state_conv_b, state_conv_c, state_lru_c, state_ret_d)
    new_p = [[] for _ in range(6)]
    mix_states, scf = None, None
    for layer in range(depth):
        mixp, zs, pca, pcb, pcc, plru, pret = _mixer(xp, xs, n_seq, layer, g_mix, w_in, p)
        mixs, *mix_states = _mix_sample(zs, layer, p, sample_states, mix_states)
        xp, h2p, xs = _outproj(mixp, xp, mixs, xs, w_out, g_ffn, layer)
        hidp, hids, pcf, scf = _ffn_up(h2p, xs, g_ffn, w_ffn_gate, w_ffn_up, w_conv_ffn, state_conv_ffn,
                                       layer, n_seq, scf)
        xp, xs = _resproj(hidp, hids, w_ffn_down, layer, xp, xs, TM_DOWN, "ffn_down")
        for lst, s in zip(new_p, (pca, pcb, pcc, plru.reshape(n_seq, W_GROUP), pret, pcf)):
            lst.append(s)
    y_prompt = _rmsnorm(xp, g_final, 0, F32, TM_NORM).reshape(n_seq, seq, D_MODEL)
    y_sample = _rmsnorm(xs, g_final, 0, F32, n_s).reshape(n_s, 1, D_MODEL)
    return (y_prompt, y_sample, *(jnp.stack(l) for l in new_p), *mix_states, scf)
```

```python
import functools

import jax
import jax.numpy as jnp
from jax import lax
from jax.experimental import pallas as pl
from jax.experimental.pallas import tpu as pltpu

F32 = jnp.float32
BF16 = jnp.bfloat16

D_MODEL = 2048
N_GROUPS = 4
W_GROUP = D_MODEL // N_GROUPS
N_PROJ = 11
CONV_A_WIDTH = 31
CONV_B_WIDTH = 3
CONV_C_WIDTH = 4
CONV_FFN_WIDTH = 3
LRU_HEADS = 4
LRU_C = 8.0
RET_HEADS = 4
HEAD_DIM = W_GROUP // RET_HEADS
RET_CHUNK = 128
ROPE_BASE = 10000.0
D_FF = 5632
EPS = 1e-6
PAST_LEN = 16384

SUBLANES = 8
VMEM_LIMIT_BYTES = 56 * 1024 * 1024

TM = 2048
TN = 512
TM_DOWN = 1024
TM_OUT = 512
FFN_ROWS = 512
MM_ROWS = 512
TM_NORM = 1024
TL = 256
W_CAST_ROWS = 256
CONV_A_ROWS = 32
CONV_A_HALO = 32
BT = 16


def _cparams(*semantics):
    return pltpu.CompilerParams(dimension_semantics=semantics, vmem_limit_bytes=VMEM_LIMIT_BYTES)


def _rms_scale(x):
    return x * lax.rsqrt(jnp.mean(x * x, axis=-1, keepdims=True) + EPS)


def _row_chunks(n_rows):
    chunk = min(MM_ROWS, n_rows // 2)
    return [slice(r, r + chunk) for r in range(0, n_rows, chunk)]


def _skip_refs(body, n_in, n_skip):
    def wrapped(*refs):
        return body(*refs[:n_in], *refs[n_in + n_skip:])
    return wrapped


def _dot(a, b):
    return jnp.dot(a, b, preferred_element_type=F32)


def _dot_nt(a, b):
    return lax.dot_general(a, b, (((1,), (1,)), ((), ())), preferred_element_type=F32)


def _dot_tn(a, b):
    return lax.dot_general(a, b, (((0,), (0,)), ((), ())), preferred_element_type=F32)


def _rmsnorm_kernel(x_ref, g_ref, o_ref):
    o_ref[...] = (_rms_scale(x_ref[...]) * g_ref[...]).astype(o_ref.dtype)


def _rmsnorm(x, g_all, layer, out_dtype, tm):
    rows = x.shape[0]
    return pl.pallas_call(
        _rmsnorm_kernel,
        grid=(rows // tm,),
        in_specs=[pl.BlockSpec((tm, D_MODEL), lambda i: (i, 0)),
                  pl.BlockSpec((None, 1, D_MODEL), lambda i: (layer, 0, 0))],
        out_specs=pl.BlockSpec((tm, D_MODEL), lambda i: (i, 0)),
        out_shape=jax.ShapeDtypeStruct((rows, D_MODEL), out_dtype),
        compiler_params=_cparams("arbitrary"),
        name="rmsnorm",
    )(x, g_all)


def _resproj_kernel(ap_ref, as_ref, w_ref, xp_ref, xs_ref, op_ref, os_ref, wb_ref):
    @pl.when(pl.program_id(1) == 0)
    def _():
        wb_ref[...] = w_ref[...].astype(BF16)
        os_ref[...] = xs_ref[...] + _dot(as_ref[...], wb_ref[...])

    for rows in _row_chunks(ap_ref.shape[0]):
        op_ref[rows, :] = xp_ref[rows, :] + _dot(ap_ref[rows, :], wb_ref[...])


def _resproj(ap, a_s, w_all, layer, xp, xs, tm, name):
    rows, n_s = ap.shape[0], a_s.shape[0]
    k_dim = ap.shape[1]
    return pl.pallas_call(
        _resproj_kernel,
        grid=(D_MODEL // TN, rows // tm),
        in_specs=[pl.BlockSpec((tm, k_dim), lambda j, i: (i, 0)),
                  pl.BlockSpec((n_s, k_dim), lambda j, i: (0, 0)),
                  pl.BlockSpec((None, k_dim, TN), lambda j, i: (layer, 0, j), pipeline_mode=pl.Buffered(1)),
                  pl.BlockSpec((tm, TN), lambda j, i: (i, j)),
                  pl.BlockSpec((n_s, TN), lambda j, i: (0, j))],
        out_specs=[pl.BlockSpec((tm, TN), lambda j, i: (i, j)),
                   pl.BlockSpec((n_s, TN), lambda j, i: (0, j))],
        out_shape=[jax.ShapeDtypeStruct((rows, D_MODEL), F32),
                   jax.ShapeDtypeStruct((n_s, D_MODEL), F32)],
        scratch_shapes=[pltpu.VMEM((k_dim, TN), BF16)],
        compiler_params=_cparams("arbitrary", "arbitrary"),
        name=name,
    )(ap, a_s, w_all, xp, xs)


def _ffn_up_kernel(tiles_per_seq, hp_ref, xs_ref, g_ref, wg_ref, wu_ref, wc_ref, st_ref,
                   hidp_ref, hids_ref, pst_ref, sst_ref, wgb_ref, wub_ref, carry_ref):
    i = pl.program_id(1)
    w0 = wc_ref[0:1, :]
    w1 = wc_ref[1:2, :]
    w2 = wc_ref[2:3, :]

    @pl.when(i == 0)
    def _():
        wgb_ref[...] = wg_ref[...].astype(BF16)
        wub_ref[...] = wu_ref[...].astype(BF16)
        hs = (_rms_scale(xs_ref[...]) * g_ref[...]).astype(BF16)
        gate = _dot(hs, wgb_ref[...])
        up = _dot(hs, wub_ref[...])
        s0 = st_ref[:, 0, :]
        s1 = st_ref[:, 1, :]
        gpre = w0 * s0 + w1 * s1 + w2 * gate
        hids_ref[...] = (jax.nn.silu(gpre) * up).astype(BF16)
        sst_ref[:, 0, :] = s1
        sst_ref[:, 1, :] = gate

    tm = hp_ref.shape[0]
    first = (i % tiles_per_seq) == 0
    prev = jnp.where(first, 0.0, carry_ref[...])
    row = lax.broadcasted_iota(jnp.int32, prev.shape, 0)
    for c in range(tm // FFN_ROWS):
        rows = slice(c * FFN_ROWS, (c + 1) * FFN_ROWS)
        hp = hp_ref[rows, :]
        gate = _dot(hp, wgb_ref[...])
        up = _dot(hp, wub_ref[...])
        gpre = w0 * _shift_rows(gate, prev, 2, row) + w1 * _shift_rows(gate, prev, 1, row) + w2 * gate
        hidp_ref[rows, :] = (jax.nn.silu(gpre) * up).astype(BF16)
        prev = gate[FFN_ROWS - SUBLANES:, :]
    carry_ref[...] = prev
    pst_ref[0] = prev[SUBLANES - (CONV_FFN_WIDTH - 1):, :]


def _ffn_up(hp, xs, g_all, wg_all, wu_all, wc_all, st_all, layer, n_seq, stacked):
    rows, n_s = hp.shape[0], xs.shape[0]
    depth = st_all.shape[0]
    tiles_per_seq = rows // n_seq // TM
    w_spec = pl.BlockSpec((None, D_MODEL, TN), lambda j, i: (layer, 0, j))
    in_specs = [pl.BlockSpec((TM, D_MODEL), lambda j, i: (i, 0)),
                pl.BlockSpec((n_s, D_MODEL), lambda j, i: (0, 0)),
                pl.BlockSpec((None, 1, D_MODEL), lambda j, i: (layer, 0, 0)),
                w_spec, w_spec,
                pl.BlockSpec((None, CONV_FFN_WIDTH, TN), lambda j, i: (layer, 0, j)),
                pl.BlockSpec((None, n_s, CONV_FFN_WIDTH - 1, TN), lambda j, i: (layer, 0, 0, j))]
    n_in = len(in_specs)
    stacked = () if stacked is None else (stacked,)
    in_specs = in_specs + [pl.BlockSpec(memory_space=pl.ANY)] * len(stacked)
    return pl.pallas_call(
        _skip_refs(functools.partial(_ffn_up_kernel, tiles_per_seq), n_in, len(stacked)),
        grid=(D_FF // TN, rows // TM),
        in_specs=in_specs,
        out_specs=[pl.BlockSpec((TM, TN), lambda j, i: (i, j)),
                   pl.BlockSpec((n_s, TN), lambda j, i: (0, j)),
                   pl.BlockSpec((1, CONV_FFN_WIDTH - 1, TN), lambda j, i: (i // tiles_per_seq, 0, j)),
                   pl.BlockSpec((None, n_s, CONV_FFN_WIDTH - 1, TN), lambda j, i: (layer, 0, 0, j))],
        out_shape=[jax.ShapeDtypeStruct((rows, D_FF), BF16),
                   jax.ShapeDtypeStruct((n_s, D_FF), BF16),
                   jax.ShapeDtypeStruct((n_seq, CONV_FFN_WIDTH - 1, D_FF), F32),
                   jax.ShapeDtypeStruct((depth, n_s, CONV_FFN_WIDTH - 1, D_FF), F32)],
        input_output_aliases={n_in + k: 3 for k in range(len(stacked))},
        scratch_shapes=[pltpu.VMEM((D_MODEL, TN), BF16),
                        pltpu.VMEM((D_MODEL, TN), BF16),
                        pltpu.VMEM((SUBLANES, TN), F32)],
        compiler_params=_cparams("arbitrary", "arbitrary"),
        name="ffn_up",
    )(hp, xs, g_all, wg_all, wu_all, wc_all, st_all, *stacked)


def _shift_rows(x, prev, d, row):
    r = pltpu.roll(x, d, axis=0)
    top = jnp.where(row < d, pltpu.roll(prev, d, axis=0), r[0:SUBLANES])
    return jnp.concatenate([top, r[SUBLANES:]], axis=0)


def _layernorm_silu(y, g, b):
    mu = jnp.mean(y, axis=-1, keepdims=True)
    yc = y - mu
    yn = yc * lax.rsqrt(jnp.mean(yc * yc, axis=-1, keepdims=True) + EPS)
    return jax.nn.silu(yn * g + b)


def _lru_gate_dots(xr, wga_ref, wgx_ref):
    ra, rx = [], []
    for h in range(LRU_HEADS):
        xh = xr[:, h * HEAD_DIM:(h + 1) * HEAD_DIM].astype(BF16)
        ra.append(_dot(xh, wga_ref[h].astype(BF16)))
        rx.append(_dot(xh, wgx_ref[h].astype(BF16)))
    return jnp.concatenate(ra, axis=-1), jnp.concatenate(rx, axis=-1)


def _lru_decay(xr, ra, rx, bga_ref, bgx_ref, lam_ref):
    r = jax.nn.sigmoid(ra + bga_ref[...])
    ig = jax.nn.sigmoid(rx + bgx_ref[...])
    log_a = -LRU_C * r * jax.nn.softplus(-lam_ref[...])
    a = jnp.exp(log_a)
    one_minus_a2 = -jnp.tanh(log_a) * (a * a + 1.0)
    b = jnp.sqrt(jnp.maximum(one_minus_a2, 0.0)) * (ig * xr)
    return a, b


def _lru_gates(xr, wga_ref, bga_ref, wgx_ref, bgx_ref, lam_ref):
    ra, rx = _lru_gate_dots(xr, wga_ref, wgx_ref)
    return _lru_decay(xr, ra, rx, bga_ref, bgx_ref, lam_ref)


def _rope(t, cc, ss):
    return t * cc + pltpu.roll(t, HEAD_DIM // 2, axis=1) * ss


def _retention_tables(chunk):
    log_g = jnp.log(1.0 - 2.0 ** (-5.0 - jnp.arange(RET_HEADS, dtype=F32)))
    idx = jnp.arange(chunk, dtype=F32)
    rel = idx[:, None] - idx[None, :]
    dmask = jnp.where(rel >= 0, jnp.exp(log_g[:, None, None] * jnp.maximum(rel, 0.0)), 0.0)
    inner = jnp.exp(log_g[:, None] * (idx + 1.0))
    sdec = jnp.exp(log_g[:, None] * (chunk - 1.0 - idx))
    cdec = jnp.exp(log_g * chunk)
    return dmask, inner, sdec, cdec


def _rope_tables(pos):
    half = HEAD_DIM // 2
    inv = ROPE_BASE ** (-jnp.arange(half, dtype=F32) / half)
    ang = pos.astype(F32)[:, None] * inv[None, :]
    cos, sin = jnp.cos(ang), jnp.sin(ang)
    return jnp.concatenate([cos, cos], axis=-1), jnp.concatenate([-sin, sin], axis=-1)


def _conv_a_rows(ub_ref, wa_ref, base, row):
    n_sub = CONV_A_ROWS // SUBLANES
    n_tiles = n_sub + CONV_A_HALO // SUBLANES
    tap0 = CONV_A_HALO - (CONV_A_WIDTH - 1)
    tiles = [ub_ref[base + i * SUBLANES:base + (i + 1) * SUBLANES, :] for i in range(n_tiles)]
    acc = [None] * n_sub
    for s in range(SUBLANES):
        if s == 0:
            win = tiles
        else:
            rolled = [pltpu.roll(t, SUBLANES - s, axis=0) for t in tiles]
            win = [jnp.where(row < SUBLANES - s, rolled[i], rolled[i + 1]) for i in range(n_tiles - 1)]
        for k in range(CONV_A_WIDTH):
            blk, sk = divmod(tap0 + k, SUBLANES)
            if sk != s:
                continue
            wk = wa_ref[k]
            for j in range(n_sub):
                term = wk * win[blk + j]
                acc[j] = term if acc[j] is None else acc[j] + term
    return jnp.concatenate(acc, axis=0)


def _mixer_kernel(n_cast, tiles_per_seq,
                  xn_ref, xs_ref, g1_ref, win_ref,
                  wa_ref, lng_ref, lnb_ref, wb_ref, wc_ref, bcc_ref,
                  wga_ref, bga_ref, wgx_ref, bgx_ref, lam_ref,
                  cc_ref, ss_ref, dmask_ref, inner_ref, sdec_ref, cdec_ref,
                  mix_ref, zs_ref, pca_ref, pcb_ref, pcc_ref, plru_ref, pret_ref,
                  wbin_ref, h_ref, z_ref, ub_ref, vb_ref, xb_ref, hl_ref, s_ref):
    step = pl.program_id(0)
    w = W_GROUP
    tl = xn_ref.shape[0]

    @pl.when(step == 0)
    def _():
        ub_ref[...] = jnp.zeros(ub_ref.shape, F32)
        vb_ref[...] = jnp.zeros(vb_ref.shape, F32)
        xb_ref[...] = jnp.zeros(xb_ref.shape, F32)
        hl_ref[...] = jnp.zeros(hl_ref.shape, F32)
        s_ref[...] = jnp.zeros(s_ref.shape, F32)

    @pl.when(step < n_cast)
    def _():
        r0 = pl.multiple_of(step * W_CAST_ROWS, W_CAST_ROWS)
        wbin_ref[pl.ds(r0, W_CAST_ROWS), :] = win_ref[...].astype(BF16)

    @pl.when(step == n_cast - 1)
    def _():
        h_ref[...] = (_rms_scale(xn_ref[...]) * g1_ref[...]).astype(BF16)
        hs = (_rms_scale(xs_ref[...]) * g1_ref[...]).astype(BF16)
        for k in range(N_PROJ):
            zs_ref[:, k * w:(k + 1) * w] = _dot(hs, wbin_ref[:, k * w:(k + 1) * w])

    @pl.when(step >= n_cast)
    def _():
        first = ((step - n_cast) % tiles_per_seq) == 0
        row = lax.broadcasted_iota(jnp.int32, (SUBLANES, w), 0)
        heads = range(RET_HEADS)
        chunks = range(tl // RET_CHUNK)

        def zcol(k, hd=None, c=None):
            if hd is None:
                return z_ref[:, k * w:(k + 1) * w]
            return z_ref[c * RET_CHUNK:(c + 1) * RET_CHUNK, k * w + hd * HEAD_DIM:k * w + (hd + 1) * HEAD_DIM]

        def inproj(k):
            z_ref[:, k * w:(k + 1) * w] = _dot(h_ref[...], wbin_ref[:, k * w:(k + 1) * w])

        def conv_a_chunks(lo, hi):
            for ci in range(lo, hi):
                base = ci * CONV_A_ROWS
                ya = _layernorm_silu(_conv_a_rows(ub_ref, wa_ref, base, row), lng_ref[...], lnb_ref[...])
                mix_ref[base:base + CONV_A_ROWS, 0:w] = ya.astype(BF16)

        assert tl // CONV_A_ROWS == 8

        inproj(0)
        inproj(1)
        u = zcol(0) * jax.nn.sigmoid(zcol(1))
        ub_ref[0:CONV_A_HALO, :] = jnp.where(first, 0.0, ub_ref[tl:tl + CONV_A_HALO, :])
        ub_ref[CONV_A_HALO:CONV_A_HALO + tl, :] = u
        pca_ref[0] = u[tl - (CONV_A_WIDTH - 1):, :]
        inproj(7)
        conv_a_chunks(0, 1)
        inproj(8)
        conv_a_chunks(1, 2)
        inproj(6)
        scale = HEAD_DIM ** -0.5
        qb, kb, ksd = {}, {}, {}
        for c in chunks:
            cc = cc_ref[c * RET_CHUNK:(c + 1) * RET_CHUNK, :]
            ss = ss_ref[c * RET_CHUNK:(c + 1) * RET_CHUNK, :]
            for hd in heads:
                qr = _rope(zcol(7, hd, c), cc, ss)
                kr = _rope(zcol(8, hd, c), cc, ss) * scale
                qb[c, hd] = qr.astype(BF16)
                kb[c, hd] = kr.astype(BF16)
                ksd[c, hd] = (kr * sdec_ref[hd]).astype(BF16)
        inproj(9)
        cx = zcol(6)
        c_prev = jnp.where(first, 0.0, xb_ref[...])
        xr = (wc_ref[0:1, :] * _shift_rows(cx, c_prev, 3, row)
              + wc_ref[1:2, :] * _shift_rows(cx, c_prev, 2, row)
              + wc_ref[2:3, :] * _shift_rows(cx, c_prev, 1, row)
              + wc_ref[3:4, :] * cx) + bcc_ref[...]
        xb_ref[...] = cx[tl - SUBLANES:, :]
        pcc_ref[0] = cx[tl - (CONV_C_WIDTH - 1):, :]
        ra, rx = _lru_gate_dots(xr, wga_ref, wgx_ref)
        vv = {(c, hd): zcol(9, hd, c).astype(BF16) for c in chunks for hd in heads}
        conv_a_chunks(2, 3)

        s_cur = [jnp.where(first, 0.0, s_ref[hd]) for hd in heads]

        def retention_dots(c):
            scores = [_dot_nt(qb[c, hd], kb[c, hd]) for hd in heads]
            q_s = [_dot(qb[c, hd], s_cur[hd].astype(BF16)) for hd in heads]
            k_v = [_dot_tn(ksd[c, hd], vv[c, hd]) for hd in heads]
            return scores, q_s, k_v

        def retention_sv(c, scores):
            masked = [(scores[hd] * dmask_ref[hd]).astype(BF16) for hd in heads]
            return [_dot(masked[hd], vv[c, hd]) for hd in heads]

        def retention_finish(c, s_v, q_s, k_v):
            for hd in heads:
                o = s_v[hd] + q_s[hd] * inner_ref[hd]
                s_cur[hd] = s_cur[hd] * cdec_ref[hd] + k_v[hd]
                on = o * lax.rsqrt(jnp.mean(o * o, axis=-1, keepdims=True) + EPS)
                mix_ref[c * RET_CHUNK:(c + 1) * RET_CHUNK, 3 * w + hd * HEAD_DIM:3 * w + (hd + 1) * HEAD_DIM] = (
                    jax.nn.silu(zcol(10, hd, c)) * on).astype(BF16)

        scores, q_s, k_v = retention_dots(0)
        inproj(5)
        a, b = _lru_decay(xr, ra, rx, bga_ref, bgx_ref, lam_ref)
        s_v = retention_sv(0, scores)
        inproj(10)
        nblk = tl // SUBLANES
        a3 = a.reshape(nblk, SUBLANES, w)
        b3 = b.reshape(nblk, SUBLANES, w)
        row3 = lax.broadcasted_iota(jnp.int32, a3.shape, 1)
        for d in (1, 2, 4):
            keep_rows = row3 >= d
            a_sh = jnp.where(keep_rows, pltpu.roll(a3, d, axis=1), 1.0)
            b_sh = jnp.where(keep_rows, pltpu.roll(b3, d, axis=1), 0.0)
            b3 = a3 * b_sh + b3
            a3 = a3 * a_sh
        h_in = jnp.where(first, 0.0, hl_ref[...])
        hs = []
        for i in range(nblk):
            h_blk = a3[i] * h_in + b3[i]
            hs.append(h_blk)
            h_in = jnp.broadcast_to(h_blk[SUBLANES - 1:SUBLANES, :], (SUBLANES, w))
        hl_ref[...] = h_in
        plru_ref[0] = h_in[0:1, :]
        mix_ref[:, 2 * w:3 * w] = (jnp.concatenate(hs, axis=0) * jax.nn.gelu(zcol(5))).astype(BF16)
        conv_a_chunks(3, 4)
        retention_finish(0, s_v, q_s, k_v)
        scores, q_s, k_v = retention_dots(1)
        inproj(2)
        conv_a_chunks(4, 5)
        s_v = retention_sv(1, scores)
        inproj(3)
        conv_a_chunks(5, 6)
        retention_finish(1, s_v, q_s, k_v)
        for hd in heads:
            s_ref[hd] = s_cur[hd]
            pret_ref[0, hd] = s_cur[hd]
        inproj(4)
        conv_a_chunks(6, 8)
        h_ref[...] = (_rms_scale(xn_ref[...]) * g1_ref[...]).astype(BF16)
        v = zcol(3) * zcol(4)
        v_prev = jnp.where(first, 0.0, vb_ref[...])
        yb = (wb_ref[0:1, :] * _shift_rows(v, v_prev, 2, row)
              + wb_ref[1:2, :] * _shift_rows(v, v_prev, 1, row)
              + wb_ref[2:3, :] * v)
        vb_ref[...] = v[tl - SUBLANES:, :]
        pcb_ref[0] = v[tl - (CONV_B_WIDTH - 1):, :]
        mix_ref[:, w:2 * w] = (zcol(2) * yb).astype(BF16)


def _mixer(xp, xs, n_seq, layer, g_mix, w_in, p):
    rows, n_s = xp.shape[0], xs.shape[0]
    seq = rows // n_seq
    w = W_GROUP
    n_cast = D_MODEL // W_CAST_ROWS
    n_tiles = rows // TL
    tiles_per_seq = seq // TL
    assert TL // RET_CHUNK == 2
    dmask, inner, sdec, cdec = _retention_tables(RET_CHUNK)
    inner = jnp.broadcast_to(inner[:, :, None], (RET_HEADS, RET_CHUNK, HEAD_DIM))
    sdec = jnp.broadcast_to(sdec[:, :, None], (RET_HEADS, RET_CHUNK, HEAD_DIM))
    cdec = jnp.broadcast_to(cdec[:, None, None], (RET_HEADS, 1, HEAD_DIM))
    cc, ss = _rope_tables(jnp.arange(seq, dtype=jnp.int32))
    wa8 = jnp.broadcast_to(p["w_conv_a"][layer][:, None, :], (CONV_A_WIDTH, SUBLANES, w))

    def tile(s):
        return jnp.clip(s - n_cast, 0, n_tiles - 1)

    def next_tile(s):
        return jnp.clip(s - n_cast + 1, 0, n_tiles - 1)

    def cast_chunk(s):
        return jnp.minimum(s, n_cast - 1)

    def vec_spec(n):
        return pl.BlockSpec((None, 1, n), lambda s: (layer, 0, 0))

    def full_spec(shape):
        nd = len(shape)
        return pl.BlockSpec(shape, lambda s: (0,) * nd)

    def layer_spec(shape):
        nd = len(shape)
        return pl.BlockSpec((None,) + shape, lambda s: (layer,) + (0,) * nd)

    def seq_spec(shape):
        nd = len(shape)
        return pl.BlockSpec((1,) + shape, lambda s: (tile(s) // tiles_per_seq,) + (0,) * nd)

    rope_spec = pl.BlockSpec((TL, HEAD_DIM), lambda s: (tile(s) % tiles_per_seq, 0))
    in_specs = [
        pl.BlockSpec((TL, D_MODEL), lambda s: (next_tile(s), 0)),
        pl.BlockSpec((n_s, D_MODEL), lambda s: (0, 0)),
        vec_spec(D_MODEL),
        pl.BlockSpec((None, W_CAST_ROWS, N_PROJ * w), lambda s: (layer, cast_chunk(s), 0)),
        full_spec((CONV_A_WIDTH, SUBLANES, w)), vec_spec(w), vec_spec(w),
        layer_spec((CONV_B_WIDTH, w)), layer_spec((CONV_C_WIDTH, w)), vec_spec(w),
        layer_spec((LRU_HEADS, HEAD_DIM, HEAD_DIM)), vec_spec(w),
        layer_spec((LRU_HEADS, HEAD_DIM, HEAD_DIM)), vec_spec(w), vec_spec(w),
        rope_spec, rope_spec,
        full_spec((RET_HEADS, RET_CHUNK, RET_CHUNK)), full_spec((RET_HEADS, RET_CHUNK, HEAD_DIM)),
        full_spec((RET_HEADS, RET_CHUNK, HEAD_DIM)), full_spec((RET_HEADS, 1, HEAD_DIM)),
    ]
    state_shapes = [(CONV_A_WIDTH - 1, w), (CONV_B_WIDTH - 1, w), (CONV_C_WIDTH - 1, w), (1, w),
                    (RET_HEADS, HEAD_DIM, HEAD_DIM)]
    out_specs = [pl.BlockSpec((TL, D_MODEL), lambda s: (tile(s), 0)),
                 pl.BlockSpec((n_s, N_PROJ * w), lambda s: (0, 0))] + [seq_spec(sh) for sh in state_shapes]
    out_shape = [jax.ShapeDtypeStruct((rows, D_MODEL), BF16),
                 jax.ShapeDtypeStruct((n_s, N_PROJ * w), F32)] + [
        jax.ShapeDtypeStruct((n_seq,) + sh, F32) for sh in state_shapes]
    scratch = [
        pltpu.VMEM((D_MODEL, N_PROJ * w), BF16),
        pltpu.VMEM((TL, D_MODEL), BF16),
        pltpu.VMEM((TL, N_PROJ * w), F32),
        pltpu.VMEM((CONV_A_HALO + TL, w), F32),
        pltpu.VMEM((SUBLANES, w), F32),
        pltpu.VMEM((SUBLANES, w), F32),
        pltpu.VMEM((SUBLANES, w), F32),
        pltpu.VMEM((RET_HEADS, HEAD_DIM, HEAD_DIM), F32),
    ]
    return pl.pallas_call(
        functools.partial(_mixer_kernel, n_cast, tiles_per_seq),
        grid=(n_cast + n_tiles,),
        in_specs=in_specs,
        out_specs=out_specs,
        out_shape=out_shape,
        scratch_shapes=scratch,
        compiler_params=_cparams("arbitrary"),
        name="mixer",
    )(xp, xs, g_mix, w_in, wa8, p["ln_a_g"], p["ln_a_b"], p["w_conv_b"], p["w_conv_c"], p["b_conv_c"],
      p["w_gate_a"], p["b_gate_a"], p["w_gate_x"], p["b_gate_x"], p["lru_lambda"],
      cc, ss, dmask, inner, sdec, cdec)


def _outproj_kernel(n_cast, mixp_ref, xp_ref, mixs_ref, xs_ref, w_ref, g_ref,
                    xo_ref, h2_ref, xso_ref, wb_ref):
    step = pl.program_id(0)

    @pl.when(step < n_cast)
    def _():
        r0 = pl.multiple_of(step * W_CAST_ROWS, W_CAST_ROWS)
        wb_ref[pl.ds(r0, W_CAST_ROWS), :] = w_ref[...].astype(BF16)

    @pl.when(step == n_cast)
    def _():
        xso_ref[...] = xs_ref[...] + _dot(mixs_ref[...], wb_ref[...])

    @pl.when(step >= n_cast)
    def _():
        for rows in _row_chunks(mixp_ref.shape[0]):
            x_new = xp_ref[rows, :] + _dot(mixp_ref[rows, :], wb_ref[...])
            xo_ref[rows, :] = x_new
            h2_ref[rows, :] = (_rms_scale(x_new) * g_ref[...]).astype(BF16)


def _outproj(mixp, xp, mixs, xs, w_all, g_all, layer):
    rows, n_s = xp.shape[0], xs.shape[0]
    n_cast = D_MODEL // W_CAST_ROWS
    n_tiles = rows // TM_OUT

    def tile(s):
        return jnp.maximum(s - n_cast, 0)

    row_spec = pl.BlockSpec((TM_OUT, D_MODEL), lambda s: (tile(s), 0))
    s_spec = pl.BlockSpec((n_s, D_MODEL), lambda s: (0, 0))
    return pl.pallas_call(
        functools.partial(_outproj_kernel, n_cast),
        grid=(n_cast + n_tiles,),
        in_specs=[row_spec, row_spec, s_spec, s_spec,
                  pl.BlockSpec((None, W_CAST_ROWS, D_MODEL), lambda s: (layer, jnp.minimum(s, n_cast - 1), 0)),
                  pl.BlockSpec((None, 1, D_MODEL), lambda s: (layer, 0, 0))],
        out_specs=[row_spec, row_spec, s_spec],
        out_shape=[jax.ShapeDtypeStruct((rows, D_MODEL), F32),
                   jax.ShapeDtypeStruct((rows, D_MODEL), BF16),
                   jax.ShapeDtypeStruct((n_s, D_MODEL), F32)],
        scratch_shapes=[pltpu.VMEM((D_MODEL, D_MODEL), BF16)],
        compiler_params=_cparams("arbitrary"),
        name="outproj",
    )(mixp, xp, mixs, xs, w_all, g_all)


def _mix_sample_kernel(z_ref, sca_ref, scb_ref, scc_ref, slru_ref, sret_ref,
                       wa_ref, lng_ref, lnb_ref, wb_ref, wc_ref, bcc_ref,
                       wga_ref, bga_ref, wgx_ref, bgx_ref, lam_ref,
                       cc_ref, ss_ref, gdec_ref,
                       mix_ref, oca_ref, ocb_ref, occ_ref, olru_ref, oret_ref,
                       o_ref):
    w = W_GROUP
    bt = z_ref.shape[0]

    def zs(k):
        return z_ref[:, k * w:(k + 1) * w]

    u = zs(0) * jax.nn.sigmoid(zs(1))
    y = wa_ref[CONV_A_WIDTH - 1:CONV_A_WIDTH, :] * u
    for k in range(CONV_A_WIDTH - 1):
        y = y + wa_ref[k:k + 1, :] * sca_ref[k]
    mix_ref[:, 0:w] = _layernorm_silu(y, lng_ref[...], lnb_ref[...]).astype(BF16)
    for k in range(CONV_A_WIDTH - 2):
        oca_ref[k] = sca_ref[k + 1]
    oca_ref[CONV_A_WIDTH - 2] = u

    v = zs(3) * zs(4)
    yb = wb_ref[0:1, :] * scb_ref[:, 0, :] + wb_ref[1:2, :] * scb_ref[:, 1, :] + wb_ref[2:3, :] * v
    mix_ref[:, w:2 * w] = (zs(2) * yb).astype(BF16)
    ocb_ref[:, 0, :] = scb_ref[:, 1, :]
    ocb_ref[:, 1, :] = v

    cx = zs(6)
    xr = (wc_ref[0:1, :] * scc_ref[:, 0, :] + wc_ref[1:2, :] * scc_ref[:, 1, :]
          + wc_ref[2:3, :] * scc_ref[:, 2, :] + wc_ref[3:4, :] * cx) + bcc_ref[...]
    occ_ref[:, 0, :] = scc_ref[:, 1, :]
    occ_ref[:, 1, :] = scc_ref[:, 2, :]
    occ_ref[:, 2, :] = cx
    a, b = _lru_gates(xr, wga_ref, bga_ref, wgx_ref, bgx_ref, lam_ref)
    h_new = a * slru_ref[...] + b
    olru_ref[...] = h_new
    mix_ref[:, 2 * w:3 * w] = (h_new * jax.nn.gelu(zs(5))).astype(BF16)

    scale = HEAD_DIM ** -0.5
    row0 = lax.broadcasted_iota(jnp.int32, (SUBLANES, HEAD_DIM), 0) == 0
    for h in range(RET_HEADS):
        cols = slice(h * HEAD_DIM, (h + 1) * HEAD_DIM)
        qr = _rope(z_ref[:, 7 * w + h * HEAD_DIM:7 * w + (h + 1) * HEAD_DIM], cc_ref[...], ss_ref[...])
        kr = _rope(z_ref[:, 8 * w + h * HEAD_DIM:8 * w + (h + 1) * HEAD_DIM], cc_ref[...], ss_ref[...]) * scale
        vv = z_ref[:, 9 * w + h * HEAD_DIM:9 * w + (h + 1) * HEAD_DIM]
        g = gdec_ref[h]
        for bi in range(bt):
            k8 = jnp.where(row0, jnp.broadcast_to(kr[bi:bi + 1, :], (SUBLANES, HEAD_DIM)), 0.0).astype(BF16)
            v8 = jnp.broadcast_to(vv[bi:bi + 1, :], (SUBLANES, HEAD_DIM)).astype(BF16)
            q8 = jnp.broadcast_to(qr[bi:bi + 1, :], (SUBLANES, HEAD_DIM)).astype(BF16)
            s_new = sret_ref[bi, h] * g + _dot_tn(k8, v8)
            oret_ref[bi, h] = s_new
            o8 = _dot(q8, s_new.astype(BF16))
            o_ref[bi:bi + 1, cols] = o8[0:1, :]
    for h in range(RET_HEADS):
        cols = slice(h * HEAD_DIM, (h + 1) * HEAD_DIM)
        o = o_ref[:, cols]
        on = o * lax.rsqrt(jnp.mean(o * o, axis=-1, keepdims=True) + EPS)
        mix_ref[:, 3 * w + h * HEAD_DIM:3 * w + (h + 1) * HEAD_DIM] = (
            jax.nn.silu(z_ref[:, 10 * w + h * HEAD_DIM:10 * w + (h + 1) * HEAD_DIM]) * on).astype(BF16)


def _mix_sample(z, layer, p, st, stacked):
    n_s = z.shape[0]
    depth = st[0].shape[0]
    w = W_GROUP
    _, _, _, gdec = _retention_tables(1)
    gdec = jnp.broadcast_to(gdec[:, None, None], (RET_HEADS, 1, HEAD_DIM))
    cc, ss = _rope_tables(PAST_LEN + jnp.arange(1, dtype=jnp.int32))

    def vec_spec():
        return pl.BlockSpec((None, 1, w), lambda i: (layer, 0, 0))

    def layer_spec(shape):
        nd = len(shape)
        return pl.BlockSpec((None,) + shape, lambda i: (layer,) + (0,) * nd)

    def state_spec(shape):
        nd = len(shape)
        return pl.BlockSpec((None, BT) + shape, lambda i: (layer, i) + (0,) * nd)

    state_shapes = [(CONV_A_WIDTH - 1, w), (CONV_B_WIDTH - 1, w), (CONV_C_WIDTH - 1, w), (w,),
                    (RET_HEADS, HEAD_DIM, HEAD_DIM)]
    conv_a_spec = pl.BlockSpec((None, CONV_A_WIDTH - 1, BT, w), lambda i: (layer, 0, i, 0))
    in_specs = [pl.BlockSpec((BT, N_PROJ * w), lambda i: (i, 0)), conv_a_spec] + [
        state_spec(s) for s in state_shapes[1:]] + [
        layer_spec((CONV_A_WIDTH, w)), vec_spec(), vec_spec(),
        layer_spec((CONV_B_WIDTH, w)), layer_spec((CONV_C_WIDTH, w)), vec_spec(),
        layer_spec((LRU_HEADS, HEAD_DIM, HEAD_DIM)), vec_spec(),
        layer_spec((LRU_HEADS, HEAD_DIM, HEAD_DIM)), vec_spec(), vec_spec(),
        pl.BlockSpec((1, HEAD_DIM), lambda i: (0, 0)), pl.BlockSpec((1, HEAD_DIM), lambda i: (0, 0)),
        pl.BlockSpec((RET_HEADS, 1, HEAD_DIM), lambda i: (0, 0, 0)),
    ]
    out_specs = [pl.BlockSpec((BT, D_MODEL), lambda i: (i, 0)), conv_a_spec] + [
        state_spec(s) for s in state_shapes[1:]]
    out_shape = [jax.ShapeDtypeStruct((n_s, D_MODEL), BF16),
                 jax.ShapeDtypeStruct((depth, CONV_A_WIDTH - 1, n_s, w), F32)] + [
        jax.ShapeDtypeStruct((depth, n_s) + s, F32) for s in state_shapes[1:]]
    n_in = len(in_specs)
    stacked = () if stacked is None else tuple(stacked)
    in_specs = in_specs + [pl.BlockSpec(memory_space=pl.ANY)] * len(stacked)
    return pl.pallas_call(
        _skip_refs(_mix_sample_kernel, n_in, len(stacked)),
        grid=(n_s // BT,),
        in_specs=in_specs,
        input_output_aliases={n_in + k: 1 + k for k in range(len(stacked))},
        out_specs=out_specs,
        out_shape=out_shape,
        scratch_shapes=[pltpu.VMEM((BT, w), F32)],
        compiler_params=_cparams("arbitrary"),
        name="mix_sample",
    )(z, *st, p["w_conv_a"], p["ln_a_g"], p["ln_a_b"], p["w_conv_b"], p["w_conv_c"], p["b_conv_c"],
      p["w_gate_a"], p["b_gate_a"], p["w_gate_x"], p["b_gate_x"], p["lru_lambda"], cc, ss, gdec, *stacked)


def kernel(x_prompt, x_sample, state_conv_a, state_conv_b, state_conv_c, state_lru_c, state_ret_d, state_conv_ffn,
           g_mix, w_in, w_conv_a, ln_a_g, ln_a_b, w_conv_b, w_conv_c, b_conv_c, w_gate_a, b_gate_a, w_gate_x,
           b_gate_x, lru_lambda, w_out, g_ffn, w_ffn_gate, w_ffn_up, w_conv_ffn, w_ffn_down, g_final):
    n_seq, seq, _ = x_prompt.shape
    n_s = x_sample.shape[0]
    depth = w_in.shape[0]
    assert x_sample.shape[1] == 1 and seq % TL == 0 and seq % TM == 0 and n_s % BT == 0

    def rows3(v):
        return v.reshape(depth, 1, v.shape[-1])

    p = dict(w_conv_a=w_conv_a, ln_a_g=rows3(ln_a_g), ln_a_b=rows3(ln_a_b), w_conv_b=w_conv_b,
             w_conv_c=w_conv_c, b_conv_c=rows3(b_conv_c), w_gate_a=w_gate_a, b_gate_a=rows3(b_gate_a),
             w_gate_x=w_gate_x, b_gate_x=rows3(b_gate_x), lru_lambda=rows3(lru_lambda))
    g_mix, g_ffn, g_final = rows3(g_mix), rows3(g_ffn), g_final.reshape(1, 1, D_MODEL)
    xp = x_prompt.reshape(n_seq * seq, D_MODEL)
    xs = x_sample.reshape(n_s, D_MODEL)
    sample_states = (jnp.transpose(state_conv_a, (0, 2, 1, 3)), state_conv_b, state_conv_c, state_lru_c, state_ret_d)
    new_p = [[] for _ in range(6)]
    mix_states, scf = None, None
    for layer in range(depth):
        mixp, zs, pca, pcb, pcc, plru, pret = _mixer(xp, xs, n_seq, layer, g_mix, w_in, p)
        mixs, *mix_states = _mix_sample(zs, layer, p, sample_states, mix_states)
        xp, h2p, xs = _outproj(mixp, xp, mixs, xs, w_out, g_ffn, layer)
        hidp, hids, pcf, scf = _ffn_up(h2p, xs, g_ffn, w_ffn_gate, w_ffn_up, w_conv_ffn, state_conv_ffn,
                                       layer, n_seq, scf)
        xp, xs = _resproj(hidp, hids, w_ffn_down, layer, xp, xs, TM_DOWN, "ffn_down")
        for lst, s in zip(new_p, (pca, pcb, pcc, plru.reshape(n_seq, W_GROUP), pret, pcf)):
            lst.append(s)
    y_prompt = _rmsnorm(xp, g_final, 0, F32, TM_NORM).reshape(n_seq, seq, D_MODEL)
    y_sample = _rmsnorm(xs, g_final, 0, F32, n_s).reshape(n_s, 1, D_MODEL)
    s_conv_a = jnp.transpose(mix_states[0], (0, 2, 1, 3))
    return (y_prompt, y_sample, *(jnp.stack(l) for l in new_p), s_conv_a, *mix_states[1:], scf)
```

```python
import functools

import jax
import jax.numpy as jnp
from jax import lax
from jax.experimental import pallas as pl
from jax.experimental.pallas import tpu as pltpu

F32 = jnp.float32
BF16 = jnp.bfloat16

D_MODEL = 2048
N_GROUPS = 4
W_GROUP = D_MODEL // N_GROUPS
N_PROJ = 11
CONV_A_WIDTH = 31
CONV_B_WIDTH = 3
CONV_C_WIDTH = 4
CONV_FFN_WIDTH = 3
LRU_HEADS = 4
LRU_C = 8.0
RET_HEADS = 4
HEAD_DIM = W_GROUP // RET_HEADS
RET_CHUNK = 128
ROPE_BASE = 10000.0
D_FF = 5632
EPS = 1e-6
PAST_LEN = 16384

SUBLANES = 8
VMEM_LIMIT_BYTES = 56 * 1024 * 1024

TM = 2048
TN = 512
TM_DOWN = 1024
TM_OUT = 512
FFN_ROWS = 512
MM_ROWS = 512
TM_NORM = 1024
TL = 256
W_CAST_ROWS = 256
CONV_A_ROWS = 32
CONV_A_HALO = 32
BT = 16


def _cparams(*semantics):
    return pltpu.CompilerParams(dimension_semantics=semantics, vmem_limit_bytes=VMEM_LIMIT_BYTES)


def _rms_scale(x):
    return x * lax.rsqrt(jnp.mean(x * x, axis=-1, keepdims=True) + EPS)


def _row_chunks(n_rows):
    chunk = min(MM_ROWS, n_rows // 2)
    return [slice(r, r + chunk) for r in range(0, n_rows, chunk)]


def _skip_refs(body, n_in, n_skip):
    def wrapped(*refs):
        return body(*refs[:n_in], *refs[n_in + n_skip:])
    return wrapped


def _layer_rows(body, layer, positions):
    def wrapped(*refs):
        refs = list(refs)
        for i in positions:
            refs[i] = refs[i].at[layer:layer + 1, :]
        return body(*refs)
    return wrapped


def _vec_spec(v):
    return pl.BlockSpec(v.shape, lambda *_: (0, 0))


def _dot(a, b):
    return jnp.dot(a, b, preferred_element_type=F32)


def _dot_nt(a, b):
    return lax.dot_general(a, b, (((1,), (1,)), ((), ())), preferred_element_type=F32)


def _dot_tn(a, b):
    return lax.dot_general(a, b, (((0,), (0,)), ((), ())), preferred_element_type=F32)


def _rmsnorm_kernel(x_ref, g_ref, o_ref):
    o_ref[...] = (_rms_scale(x_ref[...]) * g_ref[...]).astype(o_ref.dtype)


def _rmsnorm(x, g_all, layer, out_dtype, tm):
    rows = x.shape[0]
    return pl.pallas_call(
        _layer_rows(_rmsnorm_kernel, layer, (1,)),
        grid=(rows // tm,),
        in_specs=[pl.BlockSpec((tm, D_MODEL), lambda i: (i, 0)),
                  _vec_spec(g_all)],
        out_specs=pl.BlockSpec((tm, D_MODEL), lambda i: (i, 0)),
        out_shape=jax.ShapeDtypeStruct((rows, D_MODEL), out_dtype),
        compiler_params=_cparams("arbitrary"),
        name="rmsnorm",
    )(x, g_all)


def _resproj_kernel(ap_ref, as_ref, w_ref, xp_ref, xs_ref, op_ref, os_ref, wb_ref):
    @pl.when(pl.program_id(1) == 0)
    def _():
        wb_ref[...] = w_ref[...].astype(BF16)
        os_ref[...] = xs_ref[...] + _dot(as_ref[...], wb_ref[...])

    for rows in _row_chunks(ap_ref.shape[0]):
        op_ref[rows, :] = xp_ref[rows, :] + _dot(ap_ref[rows, :], wb_ref[...])


def _resproj(ap, a_s, w_all, layer, xp, xs, tm, name):
    rows, n_s = ap.shape[0], a_s.shape[0]
    k_dim = ap.shape[1]
    return pl.pallas_call(
        _resproj_kernel,
        grid=(D_MODEL // TN, rows // tm),
        in_specs=[pl.BlockSpec((tm, k_dim), lambda j, i: (i, 0)),
                  pl.BlockSpec((n_s, k_dim), lambda j, i: (0, 0)),
                  pl.BlockSpec((None, k_dim, TN), lambda j, i: (layer, 0, j), pipeline_mode=pl.Buffered(1)),
                  pl.BlockSpec((tm, TN), lambda j, i: (i, j)),
                  pl.BlockSpec((n_s, TN), lambda j, i: (0, j))],
        out_specs=[pl.BlockSpec((tm, TN), lambda j, i: (i, j)),
                   pl.BlockSpec((n_s, TN), lambda j, i: (0, j))],
        out_shape=[jax.ShapeDtypeStruct((rows, D_MODEL), F32),
                   jax.ShapeDtypeStruct((n_s, D_MODEL), F32)],
        scratch_shapes=[pltpu.VMEM((k_dim, TN), BF16)],
        compiler_params=_cparams("arbitrary", "arbitrary"),
        name=name,
    )(ap, a_s, w_all, xp, xs)


def _ffn_up_kernel(tiles_per_seq, hp_ref, xs_ref, g_ref, wg_ref, wu_ref, wc_ref, st_ref,
                   hidp_ref, hids_ref, pst_ref, sst_ref, wgb_ref, wub_ref, carry_ref):
    i = pl.program_id(1)
    w0 = wc_ref[0:1, :]
    w1 = wc_ref[1:2, :]
    w2 = wc_ref[2:3, :]

    @pl.when(i == 0)
    def _():
        wgb_ref[...] = wg_ref[...].astype(BF16)
        wub_ref[...] = wu_ref[...].astype(BF16)
        hs = (_rms_scale(xs_ref[...]) * g_ref[...]).astype(BF16)
        gate = _dot(hs, wgb_ref[...])
        up = _dot(hs, wub_ref[...])
        s0 = st_ref[:, 0, :]
        s1 = st_ref[:, 1, :]
        gpre = w0 * s0 + w1 * s1 + w2 * gate
        hids_ref[...] = (jax.nn.silu(gpre) * up).astype(BF16)
        sst_ref[:, 0, :] = s1
        sst_ref[:, 1, :] = gate

    tm = hp_ref.shape[0]
    first = (i % tiles_per_seq) == 0
    prev = jnp.where(first, 0.0, carry_ref[...])
    row = lax.broadcasted_iota(jnp.int32, prev.shape, 0)
    for c in range(tm // FFN_ROWS):
        rows = slice(c * FFN_ROWS, (c + 1) * FFN_ROWS)
        hp = hp_ref[rows, :]
        gate = _dot(hp, wgb_ref[...])
        up = _dot(hp, wub_ref[...])
        gpre = w0 * _shift_rows(gate, prev, 2, row) + w1 * _shift_rows(gate, prev, 1, row) + w2 * gate
        hidp_ref[rows, :] = (jax.nn.silu(gpre) * up).astype(BF16)
        prev = gate[FFN_ROWS - SUBLANES:, :]
    carry_ref[...] = prev
    pst_ref[0] = prev[SUBLANES - (CONV_FFN_WIDTH - 1):, :]


def _ffn_up(hp, xs, g_all, wg_all, wu_all, wc_all, st_all, layer, n_seq, stacked):
    rows, n_s = hp.shape[0], xs.shape[0]
    depth = st_all.shape[0]
    tiles_per_seq = rows // n_seq // TM
    w_spec = pl.BlockSpec((None, D_MODEL, TN), lambda j, i: (layer, 0, j))
    in_specs = [pl.BlockSpec((TM, D_MODEL), lambda j, i: (i, 0)),
                pl.BlockSpec((n_s, D_MODEL), lambda j, i: (0, 0)),
                _vec_spec(g_all),
                w_spec, w_spec,
                pl.BlockSpec((None, CONV_FFN_WIDTH, TN), lambda j, i: (layer, 0, j)),
                pl.BlockSpec((None, n_s, CONV_FFN_WIDTH - 1, TN), lambda j, i: (layer, 0, 0, j))]
    n_in = len(in_specs)
    stacked = () if stacked is None else (stacked,)
    in_specs = in_specs + [pl.BlockSpec(memory_space=pl.ANY)] * len(stacked)
    return pl.pallas_call(
        _layer_rows(_skip_refs(functools.partial(_ffn_up_kernel, tiles_per_seq), n_in, len(stacked)), layer, (2,)),
        grid=(D_FF // TN, rows // TM),
        in_specs=in_specs,
        out_specs=[pl.BlockSpec((TM, TN), lambda j, i: (i, j)),
                   pl.BlockSpec((n_s, TN), lambda j, i: (0, j)),
                   pl.BlockSpec((1, CONV_FFN_WIDTH - 1, TN), lambda j, i: (i // tiles_per_seq, 0, j)),
                   pl.BlockSpec((None, n_s, CONV_FFN_WIDTH - 1, TN), lambda j, i: (layer, 0, 0, j))],
        out_shape=[jax.ShapeDtypeStruct((rows, D_FF), BF16),
                   jax.ShapeDtypeStruct((n_s, D_FF), BF16),
                   jax.ShapeDtypeStruct((n_seq, CONV_FFN_WIDTH - 1, D_FF), F32),
                   jax.ShapeDtypeStruct((depth, n_s, CONV_FFN_WIDTH - 1, D_FF), F32)],
        input_output_aliases={n_in + k: 3 for k in range(len(stacked))},
        scratch_shapes=[pltpu.VMEM((D_MODEL, TN), BF16),
                        pltpu.VMEM((D_MODEL, TN), BF16),
                        pltpu.VMEM((SUBLANES, TN), F32)],
        compiler_params=_cparams("arbitrary", "arbitrary"),
        name="ffn_up",
    )(hp, xs, g_all, wg_all, wu_all, wc_all, st_all, *stacked)


def _shift_rows(x, prev, d, row):
    r = pltpu.roll(x, d, axis=0)
    top = jnp.where(row < d, pltpu.roll(prev, d, axis=0), r[0:SUBLANES])
    return jnp.concatenate([top, r[SUBLANES:]], axis=0)


def _layernorm_silu(y, g, b):
    mu = jnp.mean(y, axis=-1, keepdims=True)
    yc = y - mu
    yn = yc * lax.rsqrt(jnp.mean(yc * yc, axis=-1, keepdims=True) + EPS)
    return jax.nn.silu(yn * g + b)


def _lru_gate_dots(xr, wga_ref, wgx_ref):
    ra, rx = [], []
    for h in range(LRU_HEADS):
        xh = xr[:, h * HEAD_DIM:(h + 1) * HEAD_DIM].astype(BF16)
        ra.append(_dot(xh, wga_ref[h].astype(BF16)))
        rx.append(_dot(xh, wgx_ref[h].astype(BF16)))
    return jnp.concatenate(ra, axis=-1), jnp.concatenate(rx, axis=-1)


def _lru_decay(xr, ra, rx, bga_ref, bgx_ref, lam_ref):
    r = jax.nn.sigmoid(ra + bga_ref[...])
    ig = jax.nn.sigmoid(rx + bgx_ref[...])
    log_a = -LRU_C * r * jax.nn.softplus(-lam_ref[...])
    a = jnp.exp(log_a)
    one_minus_a2 = -jnp.tanh(log_a) * (a * a + 1.0)
    b = jnp.sqrt(jnp.maximum(one_minus_a2, 0.0)) * (ig * xr)
    return a, b


def _lru_gates(xr, wga_ref, bga_ref, wgx_ref, bgx_ref, lam_ref):
    ra, rx = _lru_gate_dots(xr, wga_ref, wgx_ref)
    return _lru_decay(xr, ra, rx, bga_ref, bgx_ref, lam_ref)


def _rope(t, cc, ss):
    return t * cc + pltpu.roll(t, HEAD_DIM // 2, axis=1) * ss


def _retention_tables(chunk):
    log_g = jnp.log(1.0 - 2.0 ** (-5.0 - jnp.arange(RET_HEADS, dtype=F32)))
    idx = jnp.arange(chunk, dtype=F32)
    rel = idx[:, None] - idx[None, :]
    dmask = jnp.where(rel >= 0, jnp.exp(log_g[:, None, None] * jnp.maximum(rel, 0.0)), 0.0)
    inner = jnp.exp(log_g[:, None] * (idx + 1.0))
    sdec = jnp.exp(log_g[:, None] * (chunk - 1.0 - idx))
    cdec = jnp.exp(log_g * chunk)
    return dmask, inner, sdec, cdec


def _rope_tables(pos):
    half = HEAD_DIM // 2
    inv = ROPE_BASE ** (-jnp.arange(half, dtype=F32) / half)
    ang = pos.astype(F32)[:, None] * inv[None, :]
    cos, sin = jnp.cos(ang), jnp.sin(ang)
    return jnp.concatenate([cos, cos], axis=-1), jnp.concatenate([-sin, sin], axis=-1)


def _conv_a_rows(ub_ref, wa_ref, base, row):
    n_sub = CONV_A_ROWS // SUBLANES
    n_tiles = n_sub + CONV_A_HALO // SUBLANES
    tap0 = CONV_A_HALO - (CONV_A_WIDTH - 1)
    tiles = [ub_ref[base + i * SUBLANES:base + (i + 1) * SUBLANES, :] for i in range(n_tiles)]
    acc = [None] * n_sub
    for s in range(SUBLANES):
        if s == 0:
            win = tiles
        else:
            rolled = [pltpu.roll(t, SUBLANES - s, axis=0) for t in tiles]
            win = [jnp.where(row < SUBLANES - s, rolled[i], rolled[i + 1]) for i in range(n_tiles - 1)]
        for k in range(CONV_A_WIDTH):
            blk, sk = divmod(tap0 + k, SUBLANES)
            if sk != s:
                continue
            wk = wa_ref[k]
            for j in range(n_sub):
                term = wk * win[blk + j]
                acc[j] = term if acc[j] is None else acc[j] + term
    return jnp.concatenate(acc, axis=0)


def _mixer_kernel(n_cast, tiles_per_seq,
                  xn_ref, xs_ref, g1_ref, win_ref,
                  wa_ref, lng_ref, lnb_ref, wb_ref, wc_ref, bcc_ref,
                  wga_ref, bga_ref, wgx_ref, bgx_ref, lam_ref,
                  cc_ref, ss_ref, dmask_ref, inner_ref, sdec_ref, cdec_ref,
                  mix_ref, zs_ref, pca_ref, pcb_ref, pcc_ref, plru_ref, pret_ref,
                  wbin_ref, h_ref, z_ref, ub_ref, vb_ref, xb_ref, hl_ref, s_ref):
    step = pl.program_id(0)
    w = W_GROUP
    tl = xn_ref.shape[0]

    @pl.when(step == 0)
    def _():
        ub_ref[...] = jnp.zeros(ub_ref.shape, F32)
        vb_ref[...] = jnp.zeros(vb_ref.shape, F32)
        xb_ref[...] = jnp.zeros(xb_ref.shape, F32)
        hl_ref[...] = jnp.zeros(hl_ref.shape, F32)
        s_ref[...] = jnp.zeros(s_ref.shape, F32)

    @pl.when(step < n_cast)
    def _():
        r0 = pl.multiple_of(step * W_CAST_ROWS, W_CAST_ROWS)
        wbin_ref[pl.ds(r0, W_CAST_ROWS), :] = win_ref[...].astype(BF16)

    @pl.when(step == n_cast - 1)
    def _():
        h_ref[...] = (_rms_scale(xn_ref[...]) * g1_ref[...]).astype(BF16)
        hs = (_rms_scale(xs_ref[...]) * g1_ref[...]).astype(BF16)
        for k in range(N_PROJ):
            zs_ref[:, k * w:(k + 1) * w] = _dot(hs, wbin_ref[:, k * w:(k + 1) * w])

    @pl.when(step >= n_cast)
    def _():
        first = ((step - n_cast) % tiles_per_seq) == 0
        row = lax.broadcasted_iota(jnp.int32, (SUBLANES, w), 0)
        heads = range(RET_HEADS)
        chunks = range(tl // RET_CHUNK)

        def zcol(k, hd=None, c=None):
            if hd is None:
                return z_ref[:, k * w:(k + 1) * w]
            return z_ref[c * RET_CHUNK:(c + 1) * RET_CHUNK, k * w + hd * HEAD_DIM:k * w + (hd + 1) * HEAD_DIM]

        def inproj(k):
            z_ref[:, k * w:(k + 1) * w] = _dot(h_ref[...], wbin_ref[:, k * w:(k + 1) * w])

        def conv_a_chunks(lo, hi):
            for ci in range(lo, hi):
                base = ci * CONV_A_ROWS
                ya = _layernorm_silu(_conv_a_rows(ub_ref, wa_ref, base, row), lng_ref[...], lnb_ref[...])
                mix_ref[base:base + CONV_A_ROWS, 0:w] = ya.astype(BF16)

        assert tl // CONV_A_ROWS == 8

        inproj(0)
        inproj(1)
        u = zcol(0) * jax.nn.sigmoid(zcol(1))
        ub_ref[0:CONV_A_HALO, :] = jnp.where(first, 0.0, ub_ref[tl:tl + CONV_A_HALO, :])
        ub_ref[CONV_A_HALO:CONV_A_HALO + tl, :] = u
        pca_ref[0] = u[tl - (CONV_A_WIDTH - 1):, :]
        inproj(7)
        conv_a_chunks(0, 1)
        inproj(8)
        conv_a_chunks(1, 2)
        inproj(6)
        scale = HEAD_DIM ** -0.5
        qb, kb, ksd = {}, {}, {}
        for c in chunks:
            cc = cc_ref[c * RET_CHUNK:(c + 1) * RET_CHUNK, :]
            ss = ss_ref[c * RET_CHUNK:(c + 1) * RET_CHUNK, :]
            for hd in heads:
                qr = _rope(zcol(7, hd, c), cc, ss)
                kr = _rope(zcol(8, hd, c), cc, ss) * scale
                qb[c, hd] = qr.astype(BF16)
                kb[c, hd] = kr.astype(BF16)
                ksd[c, hd] = (kr * sdec_ref[hd]).astype(BF16)
        inproj(9)
        cx = zcol(6)
        c_prev = jnp.where(first, 0.0, xb_ref[...])
        xr = (wc_ref[0:1, :] * _shift_rows(cx, c_prev, 3, row)
              + wc_ref[1:2, :] * _shift_rows(cx, c_prev, 2, row)
              + wc_ref[2:3, :] * _shift_rows(cx, c_prev, 1, row)
              + wc_ref[3:4, :] * cx) + bcc_ref[...]
        xb_ref[...] = cx[tl - SUBLANES:, :]
        pcc_ref[0] = cx[tl - (CONV_C_WIDTH - 1):, :]
        ra, rx = _lru_gate_dots(xr, wga_ref, wgx_ref)
        vv = {(c, hd): zcol(9, hd, c).astype(BF16) for c in chunks for hd in heads}
        conv_a_chunks(2, 3)

        s_cur = [jnp.where(first, 0.0, s_ref[hd]) for hd in heads]

        def retention_dots(c):
            scores = [_dot_nt(qb[c, hd], kb[c, hd]) for hd in heads]
            q_s = [_dot(qb[c, hd], s_cur[hd].astype(BF16)) for hd in heads]
            k_v = [_dot_tn(ksd[c, hd], vv[c, hd]) for hd in heads]
            return scores, q_s, k_v

        def retention_sv(c, scores):
            masked = [(scores[hd] * dmask_ref[hd]).astype(BF16) for hd in heads]
            return [_dot(masked[hd], vv[c, hd]) for hd in heads]

        def retention_finish(c, s_v, q_s, k_v):
            for hd in heads:
                o = s_v[hd] + q_s[hd] * inner_ref[hd]
                s_cur[hd] = s_cur[hd] * cdec_ref[hd] + k_v[hd]
                on = o * lax.rsqrt(jnp.mean(o * o, axis=-1, keepdims=True) + EPS)
                mix_ref[c * RET_CHUNK:(c + 1) * RET_CHUNK, 3 * w + hd * HEAD_DIM:3 * w + (hd + 1) * HEAD_DIM] = (
                    jax.nn.silu(zcol(10, hd, c)) * on).astype(BF16)

        scores, q_s, k_v = retention_dots(0)
        inproj(5)
        a, b = _lru_decay(xr, ra, rx, bga_ref, bgx_ref, lam_ref)
        s_v = retention_sv(0, scores)
        inproj(10)
        nblk = tl // SUBLANES
        a3 = a.reshape(nblk, SUBLANES, w)
        b3 = b.reshape(nblk, SUBLANES, w)
        row3 = lax.broadcasted_iota(jnp.int32, a3.shape, 1)
        for d in (1, 2, 4):
            keep_rows = row3 >= d
            a_sh = jnp.where(keep_rows, pltpu.roll(a3, d, axis=1), 1.0)
            b_sh = jnp.where(keep_rows, pltpu.roll(b3, d, axis=1), 0.0)
            b3 = a3 * b_sh + b3
            a3 = a3 * a_sh
        h_in = jnp.where(first, 0.0, hl_ref[...])
        hs = []
        for i in range(nblk):
            h_blk = a3[i] * h_in + b3[i]
            hs.append(h_blk)
            h_in = jnp.broadcast_to(h_blk[SUBLANES - 1:SUBLANES, :], (SUBLANES, w))
        hl_ref[...] = h_in
        plru_ref[0] = h_in[0:1, :]
        mix_ref[:, 2 * w:3 * w] = (jnp.concatenate(hs, axis=0) * jax.nn.gelu(zcol(5))).astype(BF16)
        conv_a_chunks(3, 4)
        retention_finish(0, s_v, q_s, k_v)
        scores, q_s, k_v = retention_dots(1)
        inproj(2)
        conv_a_chunks(4, 5)
        s_v = retention_sv(1, scores)
        inproj(3)
        conv_a_chunks(5, 6)
        retention_finish(1, s_v, q_s, k_v)
        for hd in heads:
            s_ref[hd] = s_cur[hd]
            pret_ref[0, hd] = s_cur[hd]
        inproj(4)
        conv_a_chunks(6, 8)
        h_ref[...] = (_rms_scale(xn_ref[...]) * g1_ref[...]).astype(BF16)
        v = zcol(3) * zcol(4)
        v_prev = jnp.where(first, 0.0, vb_ref[...])
        yb = (wb_ref[0:1, :] * _shift_rows(v, v_prev, 2, row)
              + wb_ref[1:2, :] * _shift_rows(v, v_prev, 1, row)
              + wb_ref[2:3, :] * v)
        vb_ref[...] = v[tl - SUBLANES:, :]
        pcb_ref[0] = v[tl - (CONV_B_WIDTH - 1):, :]
        mix_ref[:, w:2 * w] = (zcol(2) * yb).astype(BF16)


def _mixer(xp, xs, n_seq, layer, g_mix, w_in, p):
    rows, n_s = xp.shape[0], xs.shape[0]
    seq = rows // n_seq
    w = W_GROUP
    n_cast = D_MODEL // W_CAST_ROWS
    n_tiles = rows // TL
    tiles_per_seq = seq // TL
    assert TL // RET_CHUNK == 2
    dmask, inner, sdec, cdec = _retention_tables(RET_CHUNK)
    inner = jnp.broadcast_to(inner[:, :, None], (RET_HEADS, RET_CHUNK, HEAD_DIM))
    sdec = jnp.broadcast_to(sdec[:, :, None], (RET_HEADS, RET_CHUNK, HEAD_DIM))
    cdec = jnp.broadcast_to(cdec[:, None, None], (RET_HEADS, 1, HEAD_DIM))
    cc, ss = _rope_tables(jnp.arange(seq, dtype=jnp.int32))
    wa8 = jnp.broadcast_to(p["w_conv_a"][layer][:, None, :], (CONV_A_WIDTH, SUBLANES, w))

    def tile(s):
        return jnp.clip(s - n_cast, 0, n_tiles - 1)

    def next_tile(s):
        return jnp.clip(s - n_cast + 1, 0, n_tiles - 1)

    def cast_chunk(s):
        return jnp.minimum(s, n_cast - 1)

    def vec_spec(n):
        return pl.BlockSpec((g_mix.shape[0], n), lambda s: (0, 0))

    def full_spec(shape):
        nd = len(shape)
        return pl.BlockSpec(shape, lambda s: (0,) * nd)

    def layer_spec(shape):
        nd = len(shape)
        return pl.BlockSpec((None,) + shape, lambda s: (layer,) + (0,) * nd)

    def seq_spec(shape):
        nd = len(shape)
        return pl.BlockSpec((1,) + shape, lambda s: (tile(s) // tiles_per_seq,) + (0,) * nd)

    rope_spec = pl.BlockSpec((TL, HEAD_DIM), lambda s: (tile(s) % tiles_per_seq, 0))
    in_specs = [
        pl.BlockSpec((TL, D_MODEL), lambda s: (next_tile(s), 0)),
        pl.BlockSpec((n_s, D_MODEL), lambda s: (0, 0)),
        vec_spec(D_MODEL),
        pl.BlockSpec((None, W_CAST_ROWS, N_PROJ * w), lambda s: (layer, cast_chunk(s), 0)),
        full_spec((CONV_A_WIDTH, SUBLANES, w)), vec_spec(w), vec_spec(w),
        layer_spec((CONV_B_WIDTH, w)), layer_spec((CONV_C_WIDTH, w)), vec_spec(w),
        layer_spec((LRU_HEADS, HEAD_DIM, HEAD_DIM)), vec_spec(w),
        layer_spec((LRU_HEADS, HEAD_DIM, HEAD_DIM)), vec_spec(w), vec_spec(w),
        rope_spec, rope_spec,
        full_spec((RET_HEADS, RET_CHUNK, RET_CHUNK)), full_spec((RET_HEADS, RET_CHUNK, HEAD_DIM)),
        full_spec((RET_HEADS, RET_CHUNK, HEAD_DIM)), full_spec((RET_HEADS, 1, HEAD_DIM)),
    ]
    state_shapes = [(CONV_A_WIDTH - 1, w), (CONV_B_WIDTH - 1, w), (CONV_C_WIDTH - 1, w), (1, w),
                    (RET_HEADS, HEAD_DIM, HEAD_DIM)]
    out_specs = [pl.BlockSpec((TL, D_MODEL), lambda s: (tile(s), 0)),
                 pl.BlockSpec((n_s, N_PROJ * w), lambda s: (0, 0))] + [seq_spec(sh) for sh in state_shapes]
    out_shape = [jax.ShapeDtypeStruct((rows, D_MODEL), BF16),
                 jax.ShapeDtypeStruct((n_s, N_PROJ * w), F32)] + [
        jax.ShapeDtypeStruct((n_seq,) + sh, F32) for sh in state_shapes]
    scratch = [
        pltpu.VMEM((D_MODEL, N_PROJ * w), BF16),
        pltpu.VMEM((TL, D_MODEL), BF16),
        pltpu.VMEM((TL, N_PROJ * w), F32),
        pltpu.VMEM((CONV_A_HALO + TL, w), F32),
        pltpu.VMEM((SUBLANES, w), F32),
        pltpu.VMEM((SUBLANES, w), F32),
        pltpu.VMEM((SUBLANES, w), F32),
        pltpu.VMEM((RET_HEADS, HEAD_DIM, HEAD_DIM), F32),
    ]
    return pl.pallas_call(
        _layer_rows(functools.partial(_mixer_kernel, n_cast, tiles_per_seq), layer, (2, 5, 6, 9, 11, 13, 14)),
        grid=(n_cast + n_tiles,),
        in_specs=in_specs,
        out_specs=out_specs,
        out_shape=out_shape,
        scratch_shapes=scratch,
        compiler_params=_cparams("arbitrary"),
        name="mixer",
    )(xp, xs, g_mix, w_in, wa8, p["ln_a_g"], p["ln_a_b"], p["w_conv_b"], p["w_conv_c"], p["b_conv_c"],
      p["w_gate_a"], p["b_gate_a"], p["w_gate_x"], p["b_gate_x"], p["lru_lambda"],
      cc, ss, dmask, inner, sdec, cdec)


def _outproj_kernel(n_cast, mixp_ref, xp_ref, mixs_ref, xs_ref, w_ref, g_ref,
                    xo_ref, h2_ref, xso_ref, wb_ref):
    step = pl.program_id(0)

    @pl.when(step < n_cast)
    def _():
        r0 = pl.multiple_of(step * W_CAST_ROWS, W_CAST_ROWS)
        wb_ref[pl.ds(r0, W_CAST_ROWS), :] = w_ref[...].astype(BF16)

    @pl.when(step == n_cast)
    def _():
        xso_ref[...] = xs_ref[...] + _dot(mixs_ref[...], wb_ref[...])

    @pl.when(step >= n_cast)
    def _():
        for rows in _row_chunks(mixp_ref.shape[0]):
            x_new = xp_ref[rows, :] + _dot(mixp_ref[rows, :], wb_ref[...])
            xo_ref[rows, :] = x_new
            h2_ref[rows, :] = (_rms_scale(x_new) * g_ref[...]).astype(BF16)


def _outproj(mixp, xp, mixs, xs, w_all, g_all, layer):
    rows, n_s = xp.shape[0], xs.shape[0]
    n_cast = D_MODEL // W_CAST_ROWS
    n_tiles = rows // TM_OUT

    def tile(s):
        return jnp.maximum(s - n_cast, 0)

    row_spec = pl.BlockSpec((TM_OUT, D_MODEL), lambda s: (tile(s), 0))
    s_spec = pl.BlockSpec((n_s, D_MODEL), lambda s: (0, 0))
    return pl.pallas_call(
        _layer_rows(functools.partial(_outproj_kernel, n_cast), layer, (5,)),
        grid=(n_cast + n_tiles,),
        in_specs=[row_spec, row_spec, s_spec, s_spec,
                  pl.BlockSpec((None, W_CAST_ROWS, D_MODEL), lambda s: (layer, jnp.minimum(s, n_cast - 1), 0)),
                  _vec_spec(g_all)],
        out_specs=[row_spec, row_spec, s_spec],
        out_shape=[jax.ShapeDtypeStruct((rows, D_MODEL), F32),
                   jax.ShapeDtypeStruct((rows, D_MODEL), BF16),
                   jax.ShapeDtypeStruct((n_s, D_MODEL), F32)],
        scratch_shapes=[pltpu.VMEM((D_MODEL, D_MODEL), BF16)],
        compiler_params=_cparams("arbitrary"),
        name="outproj",
    )(mixp, xp, mixs, xs, w_all, g_all)


def _mix_sample_kernel(z_ref, sca_ref, scb_ref, scc_ref, slru_ref, sret_ref,
                       wa_ref, lng_ref, lnb_ref, wb_ref, wc_ref, bcc_ref,
                       wga_ref, bga_ref, wgx_ref, bgx_ref, lam_ref,
                       cc_ref, ss_ref, gdec_ref,
                       mix_ref, oca_ref, ocb_ref, occ_ref, olru_ref, oret_ref,
                       o_ref):
    w = W_GROUP
    bt = z_ref.shape[0]

    def zs(k):
        return z_ref[:, k * w:(k + 1) * w]

    u = zs(0) * jax.nn.sigmoid(zs(1))
    y = wa_ref[CONV_A_WIDTH - 1:CONV_A_WIDTH, :] * u
    for k in range(CONV_A_WIDTH - 1):
        y = y + wa_ref[k:k + 1, :] * sca_ref[k]
    mix_ref[:, 0:w] = _layernorm_silu(y, lng_ref[...], lnb_ref[...]).astype(BF16)
    for k in range(CONV_A_WIDTH - 2):
        oca_ref[k] = sca_ref[k + 1]
    oca_ref[CONV_A_WIDTH - 2] = u

    v = zs(3) * zs(4)
    yb = wb_ref[0:1, :] * scb_ref[:, 0, :] + wb_ref[1:2, :] * scb_ref[:, 1, :] + wb_ref[2:3, :] * v
    mix_ref[:, w:2 * w] = (zs(2) * yb).astype(BF16)
    ocb_ref[:, 0, :] = scb_ref[:, 1, :]
    ocb_ref[:, 1, :] = v

    cx = zs(6)
    xr = (wc_ref[0:1, :] * scc_ref[:, 0, :] + wc_ref[1:2, :] * scc_ref[:, 1, :]
          + wc_ref[2:3, :] * scc_ref[:, 2, :] + wc_ref[3:4, :] * cx) + bcc_ref[...]
    occ_ref[:, 0, :] = scc_ref[:, 1, :]
    occ_ref[:, 1, :] = scc_ref[:, 2, :]
    occ_ref[:, 2, :] = cx
    a, b = _lru_gates(xr, wga_ref, bga_ref, wgx_ref, bgx_ref, lam_ref)
    h_new = a * slru_ref[...] + b
    olru_ref[...] = h_new
    mix_ref[:, 2 * w:3 * w] = (h_new * jax.nn.gelu(zs(5))).astype(BF16)

    scale = HEAD_DIM ** -0.5
    row0 = lax.broadcasted_iota(jnp.int32, (SUBLANES, HEAD_DIM), 0) == 0
    for h in range(RET_HEADS):
        cols = slice(h * HEAD_DIM, (h + 1) * HEAD_DIM)
        qr = _rope(z_ref[:, 7 * w + h * HEAD_DIM:7 * w + (h + 1) * HEAD_DIM], cc_ref[...], ss_ref[...])
        kr = _rope(z_ref[:, 8 * w + h * HEAD_DIM:8 * w + (h + 1) * HEAD_DIM], cc_ref[...], ss_ref[...]) * scale
        vv = z_ref[:, 9 * w + h * HEAD_DIM:9 * w + (h + 1) * HEAD_DIM]
        g = gdec_ref[h]
        for bi in range(bt):
            k8 = jnp.where(row0, jnp.broadcast_to(kr[bi:bi + 1, :], (SUBLANES, HEAD_DIM)), 0.0).astype(BF16)
            v8 = jnp.broadcast_to(vv[bi:bi + 1, :], (SUBLANES, HEAD_DIM)).astype(BF16)
            q8 = jnp.broadcast_to(qr[bi:bi + 1, :], (SUBLANES, HEAD_DIM)).astype(BF16)
            s_new = sret_ref[bi, h] * g + _dot_tn(k8, v8)
            oret_ref[bi, h] = s_new
            o8 = _dot(q8, s_new.astype(BF16))
            o_ref[bi:bi + 1, cols] = o8[0:1, :]
    for h in range(RET_HEADS):
        cols = slice(h * HEAD_DIM, (h + 1) * HEAD_DIM)
        o = o_ref[:, cols]
        on = o * lax.rsqrt(jnp.mean(o * o, axis=-1, keepdims=True) + EPS)
        mix_ref[:, 3 * w + h * HEAD_DIM:3 * w + (h + 1) * HEAD_DIM] = (
            jax.nn.silu(z_ref[:, 10 * w + h * HEAD_DIM:10 * w + (h + 1) * HEAD_DIM]) * on).astype(BF16)


def _mix_sample(z, layer, p, st, stacked):
    n_s = z.shape[0]
    depth = st[0].shape[0]
    w = W_GROUP
    _, _, _, gdec = _retention_tables(1)
    gdec = jnp.broadcast_to(gdec[:, None, None], (RET_HEADS, 1, HEAD_DIM))
    cc, ss = _rope_tables(PAST_LEN + jnp.arange(1, dtype=jnp.int32))

    def vec_spec():
        return pl.BlockSpec((depth, w), lambda i: (0, 0))

    def layer_spec(shape):
        nd = len(shape)
        return pl.BlockSpec((None,) + shape, lambda i: (layer,) + (0,) * nd)

    def state_spec(shape):
        nd = len(shape)
        return pl.BlockSpec((None, BT) + shape, lambda i: (layer, i) + (0,) * nd)

    state_shapes = [(CONV_A_WIDTH - 1, w), (CONV_B_WIDTH - 1, w), (CONV_C_WIDTH - 1, w), (w,),
                    (RET_HEADS, HEAD_DIM, HEAD_DIM)]
    conv_a_spec = pl.BlockSpec((None, CONV_A_WIDTH - 1, BT, w), lambda i: (layer, 0, i, 0))
    in_specs = [pl.BlockSpec((BT, N_PROJ * w), lambda i: (i, 0)), conv_a_spec] + [
        state_spec(s) for s in state_shapes[1:]] + [
        layer_spec((CONV_A_WIDTH, w)), vec_spec(), vec_spec(),
        layer_spec((CONV_B_WIDTH, w)), layer_spec((CONV_C_WIDTH, w)), vec_spec(),
        layer_spec((LRU_HEADS, HEAD_DIM, HEAD_DIM)), vec_spec(),
        layer_spec((LRU_HEADS, HEAD_DIM, HEAD_DIM)), vec_spec(), vec_spec(),
        pl.BlockSpec((1, HEAD_DIM), lambda i: (0, 0)), pl.BlockSpec((1, HEAD_DIM), lambda i: (0, 0)),
        pl.BlockSpec((RET_HEADS, 1, HEAD_DIM), lambda i: (0, 0, 0)),
    ]
    out_specs = [pl.BlockSpec((BT, D_MODEL), lambda i: (i, 0)), conv_a_spec] + [
        state_spec(s) for s in state_shapes[1:]]
    out_shape = [jax.ShapeDtypeStruct((n_s, D_MODEL), BF16),
                 jax.ShapeDtypeStruct((depth, CONV_A_WIDTH - 1, n_s, w), F32)] + [
        jax.ShapeDtypeStruct((depth, n_s) + s, F32) for s in state_shapes[1:]]
    n_in = len(in_specs)
    stacked = () if stacked is None else tuple(stacked)
    in_specs = in_specs + [pl.BlockSpec(memory_space=pl.ANY)] * len(stacked)
    return pl.pallas_call(
        _layer_rows(_skip_refs(_mix_sample_kernel, n_in, len(stacked)), layer, (7, 8, 11, 13, 15, 16)),
        grid=(n_s // BT,),
        in_specs=in_specs,
        input_output_aliases={n_in + k: 1 + k for k in range(len(stacked))},
        out_specs=out_specs,
        out_shape=out_shape,
        scratch_shapes=[pltpu.VMEM((BT, w), F32)],
        compiler_params=_cparams("arbitrary"),
        name="mix_sample",
    )(z, *st, p["w_conv_a"], p["ln_a_g"], p["ln_a_b"], p["w_conv_b"], p["w_conv_c"], p["b_conv_c"],
      p["w_gate_a"], p["b_gate_a"], p["w_gate_x"], p["b_gate_x"], p["lru_lambda"], cc, ss, gdec, *stacked)


def kernel(x_prompt, x_sample, state_conv_a, state_conv_b, state_conv_c, state_lru_c, state_ret_d, state_conv_ffn,
           g_mix, w_in, w_conv_a, ln_a_g, ln_a_b, w_conv_b, w_conv_c, b_conv_c, w_gate_a, b_gate_a, w_gate_x,
           b_gate_x, lru_lambda, w_out, g_ffn, w_ffn_gate, w_ffn_up, w_conv_ffn, w_ffn_down, g_final):
    n_seq, seq, _ = x_prompt.shape
    n_s = x_sample.shape[0]
    depth = w_in.shape[0]
    assert x_sample.shape[1] == 1 and seq % TL == 0 and seq % TM == 0 and n_s % BT == 0

    p = dict(w_conv_a=w_conv_a, ln_a_g=ln_a_g, ln_a_b=ln_a_b, w_conv_b=w_conv_b, w_conv_c=w_conv_c,
             b_conv_c=b_conv_c, w_gate_a=w_gate_a, b_gate_a=b_gate_a, w_gate_x=w_gate_x, b_gate_x=b_gate_x,
             lru_lambda=lru_lambda)
    g_final = g_final.reshape(1, D_MODEL)
    xp = x_prompt.reshape(n_seq * seq, D_MODEL)
    xs = x_sample.reshape(n_s, D_MODEL)
    sample_states = (jnp.transpose(state_conv_a, (0, 2, 1, 3)), state_conv_b, state_conv_c, state_lru_c, state_ret_d)
    new_p = [[] for _ in range(6)]
    mix_states, scf = None, None
    for layer in range(depth):
        mixp, zs, pca, pcb, pcc, plru, pret = _mixer(xp, xs, n_seq, layer, g_mix, w_in, p)
        mixs, *mix_states = _mix_sample(zs, layer, p, sample_states, mix_states)
        xp, h2p, xs = _outproj(mixp, xp, mixs, xs, w_out, g_ffn, layer)
        hidp, hids, pcf, scf = _ffn_up(h2p, xs, g_ffn, w_ffn_gate, w_ffn_up, w_conv_ffn, state_conv_ffn,
                                       layer, n_seq, scf)
        xp, xs = _resproj(hidp, hids, w_ffn_down, layer, xp, xs, TM_DOWN, "ffn_down")
        for lst, s in zip(new_p, (pca, pcb, pcc, plru.reshape(n_seq, W_GROUP), pret, pcf)):
            lst.append(s)
    y_prompt = _rmsnorm(xp, g_final, 0, F32, TM_NORM).reshape(n_seq, seq, D_MODEL)
    y_sample = _rmsnorm(xs, g_final, 0, F32, n_s).reshape(n_s, 1, D_MODEL)
    s_conv_a = jnp.transpose(mix_states[0], (0, 2, 1, 3))
    return (y_prompt, y_sample, *(jnp.stack(l) for l in new_p), s_conv_a, *mix_states[1:], scf)
```

```python
import functools

import jax
import jax.numpy as jnp
import numpy as np
from jax import lax
from jax.experimental import pallas as pl
from jax.experimental.pallas import tpu as pltpu

F32 = jnp.float32
BF16 = jnp.bfloat16

D_MODEL = 2048
N_GROUPS = 4
W_GROUP = D_MODEL // N_GROUPS
N_PROJ = 11
CONV_A_WIDTH = 31
CONV_B_WIDTH = 3
CONV_C_WIDTH = 4
CONV_FFN_WIDTH = 3
LRU_HEADS = 4
LRU_C = 8.0
RET_HEADS = 4
HEAD_DIM = W_GROUP // RET_HEADS
RET_CHUNK = 128
ROPE_BASE = 10000.0
D_FF = 5632
EPS = 1e-6
PAST_LEN = 16384

SUBLANES = 8
VMEM_LIMIT_BYTES = 56 * 1024 * 1024

TM = 2048
TN = 512
TM_DOWN = 1024
TM_OUT = 512
FFN_ROWS = 512
MM_ROWS = 512
TM_NORM = 1024
TL = 256
W_CAST_ROWS = 256
CONV_A_ROWS = 32
CONV_A_HALO = 32
BT = 16


def _cparams(*semantics):
    return pltpu.CompilerParams(dimension_semantics=semantics, vmem_limit_bytes=VMEM_LIMIT_BYTES)


def _rms_scale(x):
    return x * lax.rsqrt(jnp.mean(x * x, axis=-1, keepdims=True) + EPS)


def _row_chunks(n_rows):
    chunk = min(MM_ROWS, n_rows // 2)
    return [slice(r, r + chunk) for r in range(0, n_rows, chunk)]


def _skip_refs(body, n_in, n_skip):
    def wrapped(*refs):
        return body(*refs[:n_in], *refs[n_in + n_skip:])
    return wrapped


def _layer_rows(body, layer, positions):
    def wrapped(*refs):
        refs = list(refs)
        for i in positions:
            refs[i] = refs[i].at[layer:layer + 1, :]
        return body(*refs)
    return wrapped


def _vec_spec(v):
    return pl.BlockSpec(v.shape, lambda *_: (0, 0))


def _dot(a, b):
    return jnp.dot(a, b, preferred_element_type=F32)


def _dot_nt(a, b):
    return lax.dot_general(a, b, (((1,), (1,)), ((), ())), preferred_element_type=F32)


def _dot_tn(a, b):
    return lax.dot_general(a, b, (((0,), (0,)), ((), ())), preferred_element_type=F32)


def _rmsnorm_kernel(x_ref, g_ref, o_ref):
    o_ref[...] = (_rms_scale(x_ref[...]) * g_ref[...]).astype(o_ref.dtype)


def _rmsnorm(x, g_all, layer, out_dtype, tm):
    rows = x.shape[0]
    return pl.pallas_call(
        _layer_rows(_rmsnorm_kernel, layer, (1,)),
        grid=(rows // tm,),
        in_specs=[pl.BlockSpec((tm, D_MODEL), lambda i: (i, 0)),
                  _vec_spec(g_all)],
        out_specs=pl.BlockSpec((tm, D_MODEL), lambda i: (i, 0)),
        out_shape=jax.ShapeDtypeStruct((rows, D_MODEL), out_dtype),
        compiler_params=_cparams("arbitrary"),
        name="rmsnorm",
    )(x, g_all)


def _resproj_kernel(ap_ref, as_ref, w_ref, xp_ref, xs_ref, op_ref, os_ref, wb_ref):
    @pl.when(pl.program_id(1) == 0)
    def _():
        wb_ref[...] = w_ref[...].astype(BF16)
        os_ref[...] = xs_ref[...] + _dot(as_ref[...], wb_ref[...])

    for rows in _row_chunks(ap_ref.shape[0]):
        op_ref[rows, :] = xp_ref[rows, :] + _dot(ap_ref[rows, :], wb_ref[...])


def _resproj(ap, a_s, w_all, layer, xp, xs, tm, name):
    rows, n_s = ap.shape[0], a_s.shape[0]
    k_dim = ap.shape[1]
    return pl.pallas_call(
        _resproj_kernel,
        grid=(D_MODEL // TN, rows // tm),
        in_specs=[pl.BlockSpec((tm, k_dim), lambda j, i: (i, 0)),
                  pl.BlockSpec((n_s, k_dim), lambda j, i: (0, 0)),
                  pl.BlockSpec((None, k_dim, TN), lambda j, i: (layer, 0, j), pipeline_mode=pl.Buffered(1)),
                  pl.BlockSpec((tm, TN), lambda j, i: (i, j)),
                  pl.BlockSpec((n_s, TN), lambda j, i: (0, j))],
        out_specs=[pl.BlockSpec((tm, TN), lambda j, i: (i, j)),
                   pl.BlockSpec((n_s, TN), lambda j, i: (0, j))],
        out_shape=[jax.ShapeDtypeStruct((rows, D_MODEL), F32),
                   jax.ShapeDtypeStruct((n_s, D_MODEL), F32)],
        scratch_shapes=[pltpu.VMEM((k_dim, TN), BF16)],
        compiler_params=_cparams("arbitrary", "arbitrary"),
        name=name,
    )(ap, a_s, w_all, xp, xs)


def _ffn_up_kernel(tiles_per_seq, hp_ref, xs_ref, g_ref, wg_ref, wu_ref, wc_ref, st_ref,
                   hidp_ref, hids_ref, pst_ref, sst_ref, wgb_ref, wub_ref, carry_ref):
    i = pl.program_id(1)
    w0 = wc_ref[0:1, :]
    w1 = wc_ref[1:2, :]
    w2 = wc_ref[2:3, :]

    @pl.when(i == 0)
    def _():
        wgb_ref[...] = wg_ref[...].astype(BF16)
        wub_ref[...] = wu_ref[...].astype(BF16)
        hs = (_rms_scale(xs_ref[...]) * g_ref[...]).astype(BF16)
        gate = _dot(hs, wgb_ref[...])
        up = _dot(hs, wub_ref[...])
        s0 = st_ref[:, 0, :]
        s1 = st_ref[:, 1, :]
        gpre = w0 * s0 + w1 * s1 + w2 * gate
        hids_ref[...] = (jax.nn.silu(gpre) * up).astype(BF16)
        sst_ref[:, 0, :] = s1
        sst_ref[:, 1, :] = gate

    tm = hp_ref.shape[0]
    first = (i % tiles_per_seq) == 0
    prev = jnp.where(first, 0.0, carry_ref[...])
    row = lax.broadcasted_iota(jnp.int32, prev.shape, 0)
    for c in range(tm // FFN_ROWS):
        rows = slice(c * FFN_ROWS, (c + 1) * FFN_ROWS)
        hp = hp_ref[rows, :]
        gate = _dot(hp, wgb_ref[...])
        up = _dot(hp, wub_ref[...])
        gpre = w0 * _shift_rows(gate, prev, 2, row) + w1 * _shift_rows(gate, prev, 1, row) + w2 * gate
        hidp_ref[rows, :] = (jax.nn.silu(gpre) * up).astype(BF16)
        prev = gate[FFN_ROWS - SUBLANES:, :]
    carry_ref[...] = prev
    pst_ref[0] = prev[SUBLANES - (CONV_FFN_WIDTH - 1):, :]


def _ffn_up(hp, xs, g_all, wg_all, wu_all, wc_all, st_all, layer, n_seq, stacked):
    rows, n_s = hp.shape[0], xs.shape[0]
    depth = st_all.shape[0]
    tiles_per_seq = rows // n_seq // TM
    w_spec = pl.BlockSpec((None, D_MODEL, TN), lambda j, i: (layer, 0, j))
    in_specs = [pl.BlockSpec((TM, D_MODEL), lambda j, i: (i, 0)),
                pl.BlockSpec((n_s, D_MODEL), lambda j, i: (0, 0)),
                _vec_spec(g_all),
                w_spec, w_spec,
                pl.BlockSpec((None, CONV_FFN_WIDTH, TN), lambda j, i: (layer, 0, j)),
                pl.BlockSpec((None, n_s, CONV_FFN_WIDTH - 1, TN), lambda j, i: (layer, 0, 0, j))]
    n_in = len(in_specs)
    stacked = () if stacked is None else (stacked,)
    in_specs = in_specs + [pl.BlockSpec(memory_space=pl.ANY)] * len(stacked)
    return pl.pallas_call(
        _layer_rows(_skip_refs(functools.partial(_ffn_up_kernel, tiles_per_seq), n_in, len(stacked)), layer, (2,)),
        grid=(D_FF // TN, rows // TM),
        in_specs=in_specs,
        out_specs=[pl.BlockSpec((TM, TN), lambda j, i: (i, j)),
                   pl.BlockSpec((n_s, TN), lambda j, i: (0, j)),
                   pl.BlockSpec((1, CONV_FFN_WIDTH - 1, TN), lambda j, i: (i // tiles_per_seq, 0, j)),
                   pl.BlockSpec((None, n_s, CONV_FFN_WIDTH - 1, TN), lambda j, i: (layer, 0, 0, j))],
        out_shape=[jax.ShapeDtypeStruct((rows, D_FF), BF16),
                   jax.ShapeDtypeStruct((n_s, D_FF), BF16),
                   jax.ShapeDtypeStruct((n_seq, CONV_FFN_WIDTH - 1, D_FF), F32),
                   jax.ShapeDtypeStruct((depth, n_s, CONV_FFN_WIDTH - 1, D_FF), F32)],
        input_output_aliases={n_in + k: 3 for k in range(len(stacked))},
        scratch_shapes=[pltpu.VMEM((D_MODEL, TN), BF16),
                        pltpu.VMEM((D_MODEL, TN), BF16),
                        pltpu.VMEM((SUBLANES, TN), F32)],
        compiler_params=_cparams("arbitrary", "arbitrary"),
        name="ffn_up",
    )(hp, xs, g_all, wg_all, wu_all, wc_all, st_all, *stacked)


def _shift_rows(x, prev, d, row):
    r = pltpu.roll(x, d, axis=0)
    top = jnp.where(row < d, pltpu.roll(prev, d, axis=0), r[0:SUBLANES])
    return jnp.concatenate([top, r[SUBLANES:]], axis=0)


def _layernorm_silu(y, g, b):
    mu = jnp.mean(y, axis=-1, keepdims=True)
    yc = y - mu
    yn = yc * lax.rsqrt(jnp.mean(yc * yc, axis=-1, keepdims=True) + EPS)
    return jax.nn.silu(yn * g + b)


def _lru_gate_dots(xr, wga_ref, wgx_ref):
    ra, rx = [], []
    for h in range(LRU_HEADS):
        xh = xr[:, h * HEAD_DIM:(h + 1) * HEAD_DIM].astype(BF16)
        ra.append(_dot(xh, wga_ref[h].astype(BF16)))
        rx.append(_dot(xh, wgx_ref[h].astype(BF16)))
    return jnp.concatenate(ra, axis=-1), jnp.concatenate(rx, axis=-1)


def _lru_decay(xr, ra, rx, bga_ref, bgx_ref, lam_ref):
    r = jax.nn.sigmoid(ra + bga_ref[...])
    ig = jax.nn.sigmoid(rx + bgx_ref[...])
    log_a = -LRU_C * r * jax.nn.softplus(-lam_ref[...])
    a = jnp.exp(log_a)
    one_minus_a2 = -jnp.tanh(log_a) * (a * a + 1.0)
    b = jnp.sqrt(jnp.maximum(one_minus_a2, 0.0)) * (ig * xr)
    return a, b


def _lru_gates(xr, wga_ref, bga_ref, wgx_ref, bgx_ref, lam_ref):
    ra, rx = _lru_gate_dots(xr, wga_ref, wgx_ref)
    return _lru_decay(xr, ra, rx, bga_ref, bgx_ref, lam_ref)


def _rope(t, cc, ss):
    return t * cc + pltpu.roll(t, HEAD_DIM // 2, axis=1) * ss


def _retention_tables(chunk):
    f32 = np.float32
    log_g = np.log(f32(1.0) - f32(2.0) ** (f32(-5.0) - np.arange(RET_HEADS, dtype=f32)))
    idx = np.arange(chunk, dtype=f32)
    rel = idx[:, None] - idx[None, :]
    dmask = np.where(rel >= 0, np.exp(log_g[:, None, None] * np.maximum(rel, f32(0.0))), f32(0.0)).astype(f32)
    inner = np.exp(log_g[:, None] * (idx + f32(1.0))).astype(f32)
    sdec = np.exp(log_g[:, None] * (f32(chunk) - f32(1.0) - idx)).astype(f32)
    cdec = np.exp(log_g * f32(chunk)).astype(f32)
    return dmask, inner, sdec, cdec


def _rope_tables(pos):
    f32 = np.float32
    half = HEAD_DIM // 2
    inv = f32(ROPE_BASE) ** (-np.arange(half, dtype=f32) / f32(half))
    ang = (np.asarray(pos, dtype=f32)[:, None] * inv[None, :]).astype(f32)
    cos, sin = np.cos(ang).astype(f32), np.sin(ang).astype(f32)
    return np.concatenate([cos, cos], axis=-1), np.concatenate([-sin, sin], axis=-1)


def _conv_a_rows(ub_ref, wa_ref, base, row):
    n_sub = CONV_A_ROWS // SUBLANES
    n_tiles = n_sub + CONV_A_HALO // SUBLANES
    tap0 = CONV_A_HALO - (CONV_A_WIDTH - 1)
    tiles = [ub_ref[base + i * SUBLANES:base + (i + 1) * SUBLANES, :] for i in range(n_tiles)]
    acc = [None] * n_sub
    for s in range(SUBLANES):
        if s == 0:
            win = tiles
        else:
            rolled = [pltpu.roll(t, SUBLANES - s, axis=0) for t in tiles]
            win = [jnp.where(row < SUBLANES - s, rolled[i], rolled[i + 1]) for i in range(n_tiles - 1)]
        for k in range(CONV_A_WIDTH):
            blk, sk = divmod(tap0 + k, SUBLANES)
            if sk != s:
                continue
            wk = wa_ref[k]
            for j in range(n_sub):
                term = wk * win[blk + j]
                acc[j] = term if acc[j] is None else acc[j] + term
    return jnp.concatenate(acc, axis=0)


def _mixer_kernel(n_cast, tiles_per_seq,
                  xn_ref, xs_ref, g1_ref, win_ref,
                  wa_ref, lng_ref, lnb_ref, wb_ref, wc_ref, bcc_ref,
                  wga_ref, bga_ref, wgx_ref, bgx_ref, lam_ref,
                  cc_ref, ss_ref, dmask_ref, inner_ref, sdec_ref, cdec_ref,
                  mix_ref, zs_ref, pca_ref, pcb_ref, pcc_ref, plru_ref, pret_ref,
                  wbin_ref, h_ref, z_ref, ub_ref, vb_ref, xb_ref, hl_ref, s_ref):
    step = pl.program_id(0)
    w = W_GROUP
    tl = xn_ref.shape[0]

    @pl.when(step == 0)
    def _():
        ub_ref[...] = jnp.zeros(ub_ref.shape, F32)
        vb_ref[...] = jnp.zeros(vb_ref.shape, F32)
        xb_ref[...] = jnp.zeros(xb_ref.shape, F32)
        hl_ref[...] = jnp.zeros(hl_ref.shape, F32)
        s_ref[...] = jnp.zeros(s_ref.shape, F32)

    @pl.when(step < n_cast)
    def _():
        r0 = pl.multiple_of(step * W_CAST_ROWS, W_CAST_ROWS)
        wbin_ref[pl.ds(r0, W_CAST_ROWS), :] = win_ref[...].astype(BF16)

    @pl.when(step == n_cast - 1)
    def _():
        h_ref[...] = (_rms_scale(xn_ref[...]) * g1_ref[...]).astype(BF16)
        hs = (_rms_scale(xs_ref[...]) * g1_ref[...]).astype(BF16)
        for k in range(N_PROJ):
            zs_ref[:, k * w:(k + 1) * w] = _dot(hs, wbin_ref[:, k * w:(k + 1) * w])

    @pl.when(step >= n_cast)
    def _():
        first = ((step - n_cast) % tiles_per_seq) == 0
        row = lax.broadcasted_iota(jnp.int32, (SUBLANES, w), 0)
        heads = range(RET_HEADS)
        chunks = range(tl // RET_CHUNK)

        def zcol(k, hd=None, c=None):
            if hd is None:
                return z_ref[:, k * w:(k + 1) * w]
            return z_ref[c * RET_CHUNK:(c + 1) * RET_CHUNK, k * w + hd * HEAD_DIM:k * w + (hd + 1) * HEAD_DIM]

        def inproj(k):
            z_ref[:, k * w:(k + 1) * w] = _dot(h_ref[...], wbin_ref[:, k * w:(k + 1) * w])

        def conv_a_chunks(lo, hi):
            for ci in range(lo, hi):
                base = ci * CONV_A_ROWS
                ya = _layernorm_silu(_conv_a_rows(ub_ref, wa_ref, base, row), lng_ref[...], lnb_ref[...])
                mix_ref[base:base + CONV_A_ROWS, 0:w] = ya.astype(BF16)

        assert tl // CONV_A_ROWS == 8

        inproj(0)
        inproj(1)
        u = zcol(0) * jax.nn.sigmoid(zcol(1))
        ub_ref[0:CONV_A_HALO, :] = jnp.where(first, 0.0, ub_ref[tl:tl + CONV_A_HALO, :])
        ub_ref[CONV_A_HALO:CONV_A_HALO + tl, :] = u
        pca_ref[0] = u[tl - (CONV_A_WIDTH - 1):, :]
        inproj(7)
        conv_a_chunks(0, 1)
        inproj(8)
        conv_a_chunks(1, 2)
        inproj(6)
        scale = HEAD_DIM ** -0.5
        qb, kb, ksd = {}, {}, {}
        for c in chunks:
            cc = cc_ref[c * RET_CHUNK:(c + 1) * RET_CHUNK, :]
            ss = ss_ref[c * RET_CHUNK:(c + 1) * RET_CHUNK, :]
            for hd in heads:
                qr = _rope(zcol(7, hd, c), cc, ss)
                kr = _rope(zcol(8, hd, c), cc, ss) * scale
                qb[c, hd] = qr.astype(BF16)
                kb[c, hd] = kr.astype(BF16)
                ksd[c, hd] = (kr * sdec_ref[hd]).astype(BF16)
        inproj(9)
        cx = zcol(6)
        c_prev = jnp.where(first, 0.0, xb_ref[...])
        xr = (wc_ref[0:1, :] * _shift_rows(cx, c_prev, 3, row)
              + wc_ref[1:2, :] * _shift_rows(cx, c_prev, 2, row)
              + wc_ref[2:3, :] * _shift_rows(cx, c_prev, 1, row)
              + wc_ref[3:4, :] * cx) + bcc_ref[...]
        xb_ref[...] = cx[tl - SUBLANES:, :]
        pcc_ref[0] = cx[tl - (CONV_C_WIDTH - 1):, :]
        ra, rx = _lru_gate_dots(xr, wga_ref, wgx_ref)
        vv = {(c, hd): zcol(9, hd, c).astype(BF16) for c in chunks for hd in heads}
        conv_a_chunks(2, 3)

        s_cur = [jnp.where(first, 0.0, s_ref[hd]) for hd in heads]

        def retention_dots(c):
            scores = [_dot_nt(qb[c, hd], kb[c, hd]) for hd in heads]
            q_s = [_dot(qb[c, hd], s_cur[hd].astype(BF16)) for hd in heads]
            k_v = [_dot_tn(ksd[c, hd], vv[c, hd]) for hd in heads]
            return scores, q_s, k_v

        def retention_sv(c, scores):
            masked = [(scores[hd] * dmask_ref[hd]).astype(BF16) for hd in heads]
            return [_dot(masked[hd], vv[c, hd]) for hd in heads]

        def retention_finish(c, s_v, q_s, k_v):
            for hd in heads:
                o = s_v[hd] + q_s[hd] * inner_ref[hd]
                s_cur[hd] = s_cur[hd] * cdec_ref[hd] + k_v[hd]
                on = o * lax.rsqrt(jnp.mean(o * o, axis=-1, keepdims=True) + EPS)
                mix_ref[c * RET_CHUNK:(c + 1) * RET_CHUNK, 3 * w + hd * HEAD_DIM:3 * w + (hd + 1) * HEAD_DIM] = (
                    jax.nn.silu(zcol(10, hd, c)) * on).astype(BF16)

        scores, q_s, k_v = retention_dots(0)
        inproj(5)
        a, b = _lru_decay(xr, ra, rx, bga_ref, bgx_ref, lam_ref)
        s_v = retention_sv(0, scores)
        inproj(10)
        nblk = tl // SUBLANES
        a3 = a.reshape(nblk, SUBLANES, w)
        b3 = b.reshape(nblk, SUBLANES, w)
        row3 = lax.broadcasted_iota(jnp.int32, a3.shape, 1)
        for d in (1, 2, 4):
            keep_rows = row3 >= d
            a_sh = jnp.where(keep_rows, pltpu.roll(a3, d, axis=1), 1.0)
            b_sh = jnp.where(keep_rows, pltpu.roll(b3, d, axis=1), 0.0)
            b3 = a3 * b_sh + b3
            a3 = a3 * a_sh
        h_in = jnp.where(first, 0.0, hl_ref[...])
        hs = []
        for i in range(nblk):
            h_blk = a3[i] * h_in + b3[i]
            hs.append(h_blk)
            h_in = jnp.broadcast_to(h_blk[SUBLANES - 1:SUBLANES, :], (SUBLANES, w))
        hl_ref[...] = h_in
        plru_ref[0] = h_in[0:1, :]
        mix_ref[:, 2 * w:3 * w] = (jnp.concatenate(hs, axis=0) * jax.nn.gelu(zcol(5))).astype(BF16)
        conv_a_chunks(3, 4)
        retention_finish(0, s_v, q_s, k_v)
        scores, q_s, k_v = retention_dots(1)
        inproj(2)
        conv_a_chunks(4, 5)
        s_v = retention_sv(1, scores)
        inproj(3)
        conv_a_chunks(5, 6)
        retention_finish(1, s_v, q_s, k_v)
        for hd in heads:
            s_ref[hd] = s_cur[hd]
            pret_ref[0, hd] = s_cur[hd]
        inproj(4)
        conv_a_chunks(6, 8)
        h_ref[...] = (_rms_scale(xn_ref[...]) * g1_ref[...]).astype(BF16)
        v = zcol(3) * zcol(4)
        v_prev = jnp.where(first, 0.0, vb_ref[...])
        yb = (wb_ref[0:1, :] * _shift_rows(v, v_prev, 2, row)
              + wb_ref[1:2, :] * _shift_rows(v, v_prev, 1, row)
              + wb_ref[2:3, :] * v)
        vb_ref[...] = v[tl - SUBLANES:, :]
        pcb_ref[0] = v[tl - (CONV_B_WIDTH - 1):, :]
        mix_ref[:, w:2 * w] = (zcol(2) * yb).astype(BF16)


def _mixer(xp, xs, n_seq, layer, g_mix, w_in, p):
    rows, n_s = xp.shape[0], xs.shape[0]
    seq = rows // n_seq
    w = W_GROUP
    n_cast = D_MODEL // W_CAST_ROWS
    n_tiles = rows // TL
    tiles_per_seq = seq // TL
    assert TL // RET_CHUNK == 2
    dmask, inner, sdec, cdec = _retention_tables(RET_CHUNK)
    inner = np.ascontiguousarray(np.broadcast_to(inner[:, :, None], (RET_HEADS, RET_CHUNK, HEAD_DIM)))
    sdec = np.ascontiguousarray(np.broadcast_to(sdec[:, :, None], (RET_HEADS, RET_CHUNK, HEAD_DIM)))
    cdec = np.ascontiguousarray(np.broadcast_to(cdec[:, None, None], (RET_HEADS, 1, HEAD_DIM)))
    cc, ss = _rope_tables(np.arange(seq))
    wa8 = jnp.broadcast_to(p["w_conv_a"][layer][:, None, :], (CONV_A_WIDTH, SUBLANES, w))

    def tile(s):
        return jnp.clip(s - n_cast, 0, n_tiles - 1)

    def next_tile(s):
        return jnp.clip(s - n_cast + 1, 0, n_tiles - 1)

    def cast_chunk(s):
        return jnp.minimum(s, n_cast - 1)

    def vec_spec(n):
        return pl.BlockSpec((g_mix.shape[0], n), lambda s: (0, 0))

    def full_spec(shape):
        nd = len(shape)
        return pl.BlockSpec(shape, lambda s: (0,) * nd)

    def layer_spec(shape):
        nd = len(shape)
        return pl.BlockSpec((None,) + shape, lambda s: (layer,) + (0,) * nd)

    def seq_spec(shape):
        nd = len(shape)
        return pl.BlockSpec((1,) + shape, lambda s: (tile(s) // tiles_per_seq,) + (0,) * nd)

    rope_spec = pl.BlockSpec((TL, HEAD_DIM), lambda s: (tile(s) % tiles_per_seq, 0))
    in_specs = [
        pl.BlockSpec((TL, D_MODEL), lambda s: (next_tile(s), 0)),
        pl.BlockSpec((n_s, D_MODEL), lambda s: (0, 0)),
        vec_spec(D_MODEL),
        pl.BlockSpec((None, W_CAST_ROWS, N_PROJ * w), lambda s: (layer, cast_chunk(s), 0)),
        full_spec((CONV_A_WIDTH, SUBLANES, w)), vec_spec(w), vec_spec(w),
        layer_spec((CONV_B_WIDTH, w)), layer_spec((CONV_C_WIDTH, w)), vec_spec(w),
        layer_spec((LRU_HEADS, HEAD_DIM, HEAD_DIM)), vec_spec(w),
        layer_spec((LRU_HEADS, HEAD_DIM, HEAD_DIM)), vec_spec(w), vec_spec(w),
        rope_spec, rope_spec,
        full_spec((RET_HEADS, RET_CHUNK, RET_CHUNK)), full_spec((RET_HEADS, RET_CHUNK, HEAD_DIM)),
        full_spec((RET_HEADS, RET_CHUNK, HEAD_DIM)), full_spec((RET_HEADS, 1, HEAD_DIM)),
    ]
    state_shapes = [(CONV_A_WIDTH - 1, w), (CONV_B_WIDTH - 1, w), (CONV_C_WIDTH - 1, w), (1, w),
                    (RET_HEADS, HEAD_DIM, HEAD_DIM)]
    out_specs = [pl.BlockSpec((TL, D_MODEL), lambda s: (tile(s), 0)),
                 pl.BlockSpec((n_s, N_PROJ * w), lambda s: (0, 0))] + [seq_spec(sh) for sh in state_shapes]
    out_shape = [jax.ShapeDtypeStruct((rows, D_MODEL), BF16),
                 jax.ShapeDtypeStruct((n_s, N_PROJ * w), F32)] + [
        jax.ShapeDtypeStruct((n_seq,) + sh, F32) for sh in state_shapes]
    scratch = [
        pltpu.VMEM((D_MODEL, N_PROJ * w), BF16),
        pltpu.VMEM((TL, D_MODEL), BF16),
        pltpu.VMEM((TL, N_PROJ * w), F32),
        pltpu.VMEM((CONV_A_HALO + TL, w), F32),
        pltpu.VMEM((SUBLANES, w), F32),
        pltpu.VMEM((SUBLANES, w), F32),
        pltpu.VMEM((SUBLANES, w), F32),
        pltpu.VMEM((RET_HEADS, HEAD_DIM, HEAD_DIM), F32),
    ]
    return pl.pallas_call(
        _layer_rows(functools.partial(_mixer_kernel, n_cast, tiles_per_seq), layer, (2, 5, 6, 9, 11, 13, 14)),
        grid=(n_cast + n_tiles,),
        in_specs=in_specs,
        out_specs=out_specs,
        out_shape=out_shape,
        scratch_shapes=scratch,
        compiler_params=_cparams("arbitrary"),
        name="mixer",
    )(xp, xs, g_mix, w_in, wa8, p["ln_a_g"], p["ln_a_b"], p["w_conv_b"], p["w_conv_c"], p["b_conv_c"],
      p["w_gate_a"], p["b_gate_a"], p["w_gate_x"], p["b_gate_x"], p["lru_lambda"],
      cc, ss, dmask, inner, sdec, cdec)


def _outproj_kernel(n_cast, mixp_ref, xp_ref, mixs_ref, xs_ref, w_ref, g_ref,
                    xo_ref, h2_ref, xso_ref, wb_ref):
    step = pl.program_id(0)

    @pl.when(step < n_cast)
    def _():
        r0 = pl.multiple_of(step * W_CAST_ROWS, W_CAST_ROWS)
        wb_ref[pl.ds(r0, W_CAST_ROWS), :] = w_ref[...].astype(BF16)

    @pl.when(step == n_cast)
    def _():
        xso_ref[...] = xs_ref[...] + _dot(mixs_ref[...], wb_ref[...])

    @pl.when(step >= n_cast)
    def _():
        for rows in _row_chunks(mixp_ref.shape[0]):
            x_new = xp_ref[rows, :] + _dot(mixp_ref[rows, :], wb_ref[...])
            xo_ref[rows, :] = x_new
            h2_ref[rows, :] = (_rms_scale(x_new) * g_ref[...]).astype(BF16)


def _outproj(mixp, xp, mixs, xs, w_all, g_all, layer):
    rows, n_s = xp.shape[0], xs.shape[0]
    n_cast = D_MODEL // W_CAST_ROWS
    n_tiles = rows // TM_OUT

    def tile(s):
        return jnp.maximum(s - n_cast, 0)

    row_spec = pl.BlockSpec((TM_OUT, D_MODEL), lambda s: (tile(s), 0))
    s_spec = pl.BlockSpec((n_s, D_MODEL), lambda s: (0, 0))
    return pl.pallas_call(
        _layer_rows(functools.partial(_outproj_kernel, n_cast), layer, (5,)),
        grid=(n_cast + n_tiles,),
        in_specs=[row_spec, row_spec, s_spec, s_spec,
                  pl.BlockSpec((None, W_CAST_ROWS, D_MODEL), lambda s: (layer, jnp.minimum(s, n_cast - 1), 0)),
                  _vec_spec(g_all)],
        out_specs=[row_spec, row_spec, s_spec],
        out_shape=[jax.ShapeDtypeStruct((rows, D_MODEL), F32),
                   jax.ShapeDtypeStruct((rows, D_MODEL), BF16),
                   jax.ShapeDtypeStruct((n_s, D_MODEL), F32)],
        scratch_shapes=[pltpu.VMEM((D_MODEL, D_MODEL), BF16)],
        compiler_params=_cparams("arbitrary"),
        name="outproj",
    )(mixp, xp, mixs, xs, w_all, g_all)


def _mix_sample_kernel(z_ref, sca_ref, scb_ref, scc_ref, slru_ref, sret_ref,
                       wa_ref, lng_ref, lnb_ref, wb_ref, wc_ref, bcc_ref,
                       wga_ref, bga_ref, wgx_ref, bgx_ref, lam_ref,
                       cc_ref, ss_ref, gdec_ref,
                       mix_ref, oca_ref, ocb_ref, occ_ref, olru_ref, oret_ref,
                       o_ref):
    w = W_GROUP
    bt = z_ref.shape[0]

    def zs(k):
        return z_ref[:, k * w:(k + 1) * w]

    u = zs(0) * jax.nn.sigmoid(zs(1))
    y = wa_ref[CONV_A_WIDTH - 1:CONV_A_WIDTH, :] * u
    for k in range(CONV_A_WIDTH - 1):
        y = y + wa_ref[k:k + 1, :] * sca_ref[k]
    mix_ref[:, 0:w] = _layernorm_silu(y, lng_ref[...], lnb_ref[...]).astype(BF16)
    for k in range(CONV_A_WIDTH - 2):
        oca_ref[k] = sca_ref[k + 1]
    oca_ref[CONV_A_WIDTH - 2] = u

    v = zs(3) * zs(4)
    yb = wb_ref[0:1, :] * scb_ref[:, 0, :] + wb_ref[1:2, :] * scb_ref[:, 1, :] + wb_ref[2:3, :] * v
    mix_ref[:, w:2 * w] = (zs(2) * yb).astype(BF16)
    ocb_ref[:, 0, :] = scb_ref[:, 1, :]
    ocb_ref[:, 1, :] = v

    cx = zs(6)
    xr = (wc_ref[0:1, :] * scc_ref[:, 0, :] + wc_ref[1:2, :] * scc_ref[:, 1, :]
          + wc_ref[2:3, :] * scc_ref[:, 2, :] + wc_ref[3:4, :] * cx) + bcc_ref[...]
    occ_ref[:, 0, :] = scc_ref[:, 1, :]
    occ_ref[:, 1, :] = scc_ref[:, 2, :]
    occ_ref[:, 2, :] = cx
    a, b = _lru_gates(xr, wga_ref, bga_ref, wgx_ref, bgx_ref, lam_ref)
    h_new = a * slru_ref[...] + b
    olru_ref[...] = h_new
    mix_ref[:, 2 * w:3 * w] = (h_new * jax.nn.gelu(zs(5))).astype(BF16)

    scale = HEAD_DIM ** -0.5
    row0 = lax.broadcasted_iota(jnp.int32, (SUBLANES, HEAD_DIM), 0) == 0
    for h in range(RET_HEADS):
        cols = slice(h * HEAD_DIM, (h + 1) * HEAD_DIM)
        qr = _rope(z_ref[:, 7 * w + h * HEAD_DIM:7 * w + (h + 1) * HEAD_DIM], cc_ref[...], ss_ref[...])
        kr = _rope(z_ref[:, 8 * w + h * HEAD_DIM:8 * w + (h + 1) * HEAD_DIM], cc_ref[...], ss_ref[...]) * scale
        vv = z_ref[:, 9 * w + h * HEAD_DIM:9 * w + (h + 1) * HEAD_DIM]
        g = gdec_ref[h]
        for bi in range(bt):
            k8 = jnp.where(row0, jnp.broadcast_to(kr[bi:bi + 1, :], (SUBLANES, HEAD_DIM)), 0.0).astype(BF16)
            v8 = jnp.broadcast_to(vv[bi:bi + 1, :], (SUBLANES, HEAD_DIM)).astype(BF16)
            q8 = jnp.broadcast_to(qr[bi:bi + 1, :], (SUBLANES, HEAD_DIM)).astype(BF16)
            s_new = sret_ref[bi, h] * g + _dot_tn(k8, v8)
            oret_ref[bi, h] = s_new
            o8 = _dot(q8, s_new.astype(BF16))
            o_ref[bi:bi + 1, cols] = o8[0:1, :]
    for h in range(RET_HEADS):
        cols = slice(h * HEAD_DIM, (h + 1) * HEAD_DIM)
        o = o_ref[:, cols]
        on = o * lax.rsqrt(jnp.mean(o * o, axis=-1, keepdims=True) + EPS)
        mix_ref[:, 3 * w + h * HEAD_DIM:3 * w + (h + 1) * HEAD_DIM] = (
            jax.nn.silu(z_ref[:, 10 * w + h * HEAD_DIM:10 * w + (h + 1) * HEAD_DIM]) * on).astype(BF16)


def _mix_sample(z, layer, p, st, stacked):
    n_s = z.shape[0]
    depth = st[0].shape[0]
    w = W_GROUP
    _, _, _, gdec = _retention_tables(1)
    gdec = np.ascontiguousarray(np.broadcast_to(gdec[:, None, None], (RET_HEADS, 1, HEAD_DIM)))
    cc, ss = _rope_tables(PAST_LEN + np.arange(1))

    def vec_spec():
        return pl.BlockSpec((depth, w), lambda i: (0, 0))

    def layer_spec(shape):
        nd = len(shape)
        return pl.BlockSpec((None,) + shape, lambda i: (layer,) + (0,) * nd)

    def state_spec(shape):
        nd = len(shape)
        return pl.BlockSpec((None, BT) + shape, lambda i: (layer, i) + (0,) * nd)

    state_shapes = [(CONV_A_WIDTH - 1, w), (CONV_B_WIDTH - 1, w), (CONV_C_WIDTH - 1, w), (w,),
                    (RET_HEADS, HEAD_DIM, HEAD_DIM)]
    conv_a_spec = pl.BlockSpec((None, CONV_A_WIDTH - 1, BT, w), lambda i: (layer, 0, i, 0))
    in_specs = [pl.BlockSpec((BT, N_PROJ * w), lambda i: (i, 0)), conv_a_spec] + [
        state_spec(s) for s in state_shapes[1:]] + [
        layer_spec((CONV_A_WIDTH, w)), vec_spec(), vec_spec(),
        layer_spec((CONV_B_WIDTH, w)), layer_spec((CONV_C_WIDTH, w)), vec_spec(),
        layer_spec((LRU_HEADS, HEAD_DIM, HEAD_DIM)), vec_spec(),
        layer_spec((LRU_HEADS, HEAD_DIM, HEAD_DIM)), vec_spec(), vec_spec(),
        pl.BlockSpec((1, HEAD_DIM), lambda i: (0, 0)), pl.BlockSpec((1, HEAD_DIM), lambda i: (0, 0)),
        pl.BlockSpec((RET_HEADS, 1, HEAD_DIM), lambda i: (0, 0, 0)),
    ]
    out_specs = [pl.BlockSpec((BT, D_MODEL), lambda i: (i, 0)), conv_a_spec] + [
        state_spec(s) for s in state_shapes[1:]]
    out_shape = [jax.ShapeDtypeStruct((n_s, D_MODEL), BF16),
                 jax.ShapeDtypeStruct((depth, CONV_A_WIDTH - 1, n_s, w), F32)] + [
        jax.ShapeDtypeStruct((depth, n_s) + s, F32) for s in state_shapes[1:]]
    n_in = len(in_specs)
    stacked = () if stacked is None else tuple(stacked)
    in_specs = in_specs + [pl.BlockSpec(memory_space=pl.ANY)] * len(stacked)
    return pl.pallas_call(
        _layer_rows(_skip_refs(_mix_sample_kernel, n_in, len(stacked)), layer, (7, 8, 11, 13, 15, 16)),
        grid=(n_s // BT,),
        in_specs=in_specs,
        input_output_aliases={n_in + k: 1 + k for k in range(len(stacked))},
        out_specs=out_specs,
        out_shape=out_shape,
        scratch_shapes=[pltpu.VMEM((BT, w), F32)],
        compiler_params=_cparams("arbitrary"),
        name="mix_sample",
    )(z, *st, p["w_conv_a"], p["ln_a_g"], p["ln_a_b"], p["w_conv_b"], p["w_conv_c"], p["b_conv_c"],
      p["w_gate_a"], p["b_gate_a"], p["w_gate_x"], p["b_gate_x"], p["lru_lambda"], cc, ss, gdec, *stacked)


def kernel(x_prompt, x_sample, state_conv_a, state_conv_b, state_conv_c, state_lru_c, state_ret_d, state_conv_ffn,
           g_mix, w_in, w_conv_a, ln_a_g, ln_a_b, w_conv_b, w_conv_c, b_conv_c, w_gate_a, b_gate_a, w_gate_x,
           b_gate_x, lru_lambda, w_out, g_ffn, w_ffn_gate, w_ffn_up, w_conv_ffn, w_ffn_down, g_final):
    n_seq, seq, _ = x_prompt.shape
    n_s = x_sample.shape[0]
    depth = w_in.shape[0]
    assert x_sample.shape[1] == 1 and seq % TL == 0 and seq % TM == 0 and n_s % BT == 0

    p = dict(w_conv_a=w_conv_a, ln_a_g=ln_a_g, ln_a_b=ln_a_b, w_conv_b=w_conv_b, w_conv_c=w_conv_c,
             b_conv_c=b_conv_c, w_gate_a=w_gate_a, b_gate_a=b_gate_a, w_gate_x=w_gate_x, b_gate_x=b_gate_x,
             lru_lambda=lru_lambda)
    g_final = g_final.reshape(1, D_MODEL)
    xp = x_prompt.reshape(n_seq * seq, D_MODEL)
    xs = x_sample.reshape(n_s, D_MODEL)
    sample_states = (jnp.transpose(state_conv_a, (0, 2, 1, 3)), state_conv_b, state_conv_c, state_lru_c, state_ret_d)
    new_p = [[] for _ in range(6)]
    mix_states, scf = None, None
    for layer in range(depth):
        mixp, zs, pca, pcb, pcc, plru, pret = _mixer(xp, xs, n_seq, layer, g_mix, w_in, p)
        mixs, *mix_states = _mix_sample(zs, layer, p, sample_states, mix_states)
        xp, h2p, xs = _outproj(mixp, xp, mixs, xs, w_out, g_ffn, layer)
        hidp, hids, pcf, scf = _ffn_up(h2p, xs, g_ffn, w_ffn_gate, w_ffn_up, w_conv_ffn, state_conv_ffn,
                                       layer, n_seq, scf)
        xp, xs = _resproj(hidp, hids, w_ffn_down, layer, xp, xs, TM_DOWN, "ffn_down")
        for lst, s in zip(new_p, (pca, pcb, pcc, plru.reshape(n_seq, W_GROUP), pret, pcf)):
            lst.append(s)
    y_prompt = _rmsnorm(xp, g_final, 0, F32, TM_NORM).reshape(n_seq, seq, D_MODEL)
    y_sample = _rmsnorm(xs, g_final, 0, F32, n_s).reshape(n_s, 1, D_MODEL)
    s_conv_a = jnp.transpose(mix_states[0], (0, 2, 1, 3))
    return (y_prompt, y_sample, *(jnp.stack(l) for l in new_p), s_conv_a, *mix_states[1:], scf)
```
